```python
import math
import jax
import jax.numpy as jnp
from jax import lax
import numpy as np

D_MODEL = 1024
BATCH = 8
SEQ = 4096
DEPTH = 2

GRID_W = 64
CTX_LEN = 256
N_SUB = 3
FF_HIDDEN = 2816
LN_EPS = 1e-5
RMS_EPS = 1e-5
DEEPNORM_ALPHA = (2 * DEPTH) ** 0.25
DEEPNORM_BETA = (8 * DEPTH) ** -0.25
N_EVEN = (DEPTH + 1) // 2
N_ODD = DEPTH // 2

RW_HEADS = 8
RW_HEAD = 64
RW_W = RW_HEADS * RW_HEAD
RW_DECAY_LORA = 64
RW_A_LORA = 64
RW_GATE_LORA = 160
RW_GN_EPS = 64e-5
RW_SPLITS = (RW_W, 2 * RW_W, 3 * RW_W, 3 * RW_W + 2 * RW_DECAY_LORA,
             3 * RW_W + 2 * RW_DECAY_LORA + RW_A_LORA)
A_TOTAL = 3 * RW_W + 2 * RW_DECAY_LORA + RW_A_LORA + RW_GATE_LORA

HG_HEADS = 4
HG_DK = 128
HG_DV = 128
HG_KW = HG_HEADS * HG_DK
HG_VW = HG_HEADS * HG_DV
HG_CHUNK = 64
HG_SPLITS = (HG_KW, 3 * HG_KW, 3 * HG_KW + HG_VW)
B_TOTAL = 3 * HG_KW + 2 * HG_VW
EVEN_PROJ = A_TOTAL + B_TOTAL
EVEN_MIX = RW_W + HG_VW

DA_HEADS = 8
DA_HEAD = 64
DA_W = DA_HEADS * 2 * DA_HEAD
Q_BLOCK = 128
ROPE_BASE = 10000.0

kernel_name = 'hybrid_rwkv7_hgrn2_diffattn_prefix_dit'


def _layernorm(t, g, b):
    tf = t.astype(jnp.float32)
    mu = jnp.mean(tf, axis=-1, keepdims=True)
    var = jnp.mean(jnp.square(tf - mu), axis=-1, keepdims=True)
    return (tf - mu) * lax.rsqrt(var + LN_EPS) * g + b


def _rmsnorm(t, g, eps):
    tf = t.astype(jnp.float32)
    return tf * lax.rsqrt(jnp.mean(tf * tf, axis=-1, keepdims=True) + eps) * g


def _modulation(cvec, w, b):
    m = jax.nn.silu(cvec) @ w + b
    return m.reshape(m.shape[:-1] + (N_SUB, 3, D_MODEL))


def _modulate(h, m, j):
    return h * (1.0 + m[..., j, 1, :]) + m[..., j, 0, :]


def _post_norm(h, out, gate, weight, g, b):
    return _layernorm(DEEPNORM_ALPHA * h + weight * gate * out, g, b)


def _swiglu(h, w_in, w_out):
    gt, up = jnp.split(h @ w_in, 2, axis=-1)
    return (jax.nn.silu(gt) * up) @ w_out


def _ffn_sublayer(h, m, j, w_in, w_out, g, b):
    out = _swiglu(_modulate(h, m, j), w_in, w_out)
    return _post_norm(h, out, m[..., j, 2, :], 0.5, g, b)


def _centred_shift(u):
    p = jnp.pad(u, ((0, 0), (1, 1), (0, 0)))
    return 0.5 * (p[:, :-2] + p[:, 2:])


def _rwkv7_prep(u, mu, w0, w2, a0, a2, g2, k_k, k_a):
    B, T, _ = u.shape
    u = u.astype(jnp.float32)
    u = u + mu * (_centred_shift(u) - u)
    r, k, v, wd, ad, gd = jnp.split(u, RW_SPLITS, axis=-1)
    wd = wd.reshape(B, T, 2, RW_DECAY_LORA)
    w = -jax.nn.softplus(-(w0 + jnp.einsum('btdr,drc->btdc', jnp.tanh(wd), w2))) - 0.5
    decay = jnp.exp(-jnp.exp(w)).reshape(B, T, 2, RW_HEADS, RW_HEAD)
    a = jax.nn.sigmoid(a0 + ad @ a2)
    g = jax.nn.sigmoid(gd) @ g2
    kk = (k * k_k).reshape(B, T, RW_HEADS, RW_HEAD)
    kk = kk / jnp.maximum(jnp.sqrt(jnp.sum(kk * kk, axis=-1, keepdims=True)), 1e-12)
    k = k * (1.0 + (a - 1.0) * k_a)
    heads = lambda t: t.reshape(B, T, RW_HEADS, RW_HEAD)
    return dict(r=heads(r), k=heads(k), v=heads(v), kk=kk, a=heads(a), decay=decay, g=g)


def _rwkv7_scan(s0, p, d, reverse):
    seq = lambda t: jnp.moveaxis(t, 1, 0)
    xs = (seq(p['r']), seq(p['decay'][:, :, d]), seq(p['k']), seq(p['v']),
          seq(-p['kk']), seq(p['kk'] * p['a']))

    def step(S, inp):
        r_t, w_t, k_t, v_t, z_t, b_t = inp
        sa = jnp.einsum('bhvk,bhk->bhv', S, z_t)
        S = S * w_t[:, :, None, :] + sa[..., None] * b_t[:, :, None, :] + v_t[..., None] * k_t[:, :, None, :]
        return S, jnp.einsum('bhvk,bhk->bhv', S, r_t)

    S, y = lax.scan(step, s0, xs, reverse=reverse)
    return S, jnp.moveaxis(y, 0, 1)


def _rwkv7_readout(y, p, r_k, gn_g, gn_b):
    B, T, H, N = y.shape
    mu = jnp.mean(y, axis=-1, keepdims=True)
    var = jnp.mean(jnp.square(y - mu), axis=-1, keepdims=True)
    y = (y - mu) * lax.rsqrt(var + RW_GN_EPS) * gn_g.reshape(H, N) + gn_b.reshape(H, N)
    y = y + jnp.sum(p['r'] * p['k'] * r_k, axis=-1, keepdims=True) * p['v']
    return y.reshape(B, T, H * N) * p['g']


def _rwkv7_mixer(u_ctx, u_lat, mu, w0, w2, a0, a2, g2, k_k, k_a, r_k, gn_g, gn_b, with_ctx_out):
    pc = _rwkv7_prep(u_ctx, mu, w0, w2, a0, a2, g2, k_k, k_a)
    pl = _rwkv7_prep(u_lat, mu, w0, w2, a0, a2, g2, k_k, k_a)
    B = u_lat.shape[0]
    s0 = jnp.zeros((B, RW_HEADS, RW_HEAD, RW_HEAD), jnp.float32)
    y_lat, y_ctx = 0.0, 0.0
    for d in range(2):
        s_ctx, yc = _rwkv7_scan(s0, pc, d, d == 1)
        _, yl = _rwkv7_scan(s_ctx, pl, d, d == 1)
        y_lat = y_lat + yl
        if with_ctx_out:
            y_ctx = y_ctx + yc
    out_lat = _rwkv7_readout(y_lat, pl, r_k, gn_g, gn_b)
    out_ctx = _rwkv7_readout(y_ctx, pc, r_k, gn_g, gn_b) if with_ctx_out else None
    return out_ctx, out_lat


def _hgrn2_prep(u, lb):
    B, T, _ = u.shape
    u = u.astype(jnp.float32)
    q, f, i, g = jnp.split(u, HG_SPLITS, axis=-1)
    fg = lb + (1.0 - lb) * jax.nn.sigmoid(f.reshape(B, T, 2, HG_KW))
    fg = fg.reshape(B, T, 2, HG_HEADS, HG_DK)
    return dict(q=jax.nn.silu(q).reshape(B, T, HG_HEADS, HG_DK), logf=jnp.log(fg), k=1.0 - fg,
                i=i.reshape(B, T, HG_HEADS, HG_DV), g=g.reshape(B, T, HG_HEADS, HG_DV))


def _gla_chunked(q, k, v, logf, s0):
    B, T, H, K = q.shape
    n, L = T // HG_CHUNK, HG_CHUNK
    q, k, v, logf = [t.reshape(B, n, L, H, -1) for t in (q, k, v, logf)]
    b = jnp.cumsum(logf, axis=2)
    b_last = b[:, :, -1]
    qb = q * jnp.exp(b)
    kb = k * jnp.exp(-b)
    kd = k * jnp.exp(b_last[:, :, None] - b)
    scores = jnp.einsum('bnlhk,bnshk->bnhls', qb, kb)
    scores = jnp.where(jnp.tril(jnp.ones((L, L), bool)), scores, 0.0)
    o_intra = jnp.einsum('bnhls,bnshv->bnlhv', scores, v)

    def step(S, inp):
        qb_c, kd_c, v_c, gl_c = inp
        o = jnp.einsum('blhk,bhkv->blhv', qb_c, S)
        S = S * gl_c[..., None] + jnp.einsum('blhk,blhv->bhkv', kd_c, v_c)
        return S, o

    xs = (jnp.moveaxis(qb, 1, 0), jnp.moveaxis(kd, 1, 0), jnp.moveaxis(v, 1, 0),
          jnp.moveaxis(jnp.exp(b_last), 1, 0))
    S, o_inter = lax.scan(step, s0, xs)
    o = o_intra + jnp.moveaxis(o_inter, 0, 1)
    return S, o.reshape(B, T, H, -1)


def _hgrn2_readout(o, p, norm_g):
    B, T = o.shape[:2]
    return (_rmsnorm(o, norm_g, RMS_EPS) * jax.nn.silu(p['g'])).reshape(B, T, HG_VW)


def _hgrn2_mixer(u_ctx, u_lat, lb, norm_g, with_ctx_out):
    pc = _hgrn2_prep(u_ctx, lb)
    pl = _hgrn2_prep(u_lat, lb)
    B = u_lat.shape[0]
    s0 = jnp.zeros((B, HG_HEADS, HG_DK, HG_DV), jnp.float32)
    o_lat, o_ctx = 0.0, 0.0
    for d in range(2):
        fl = (lambda t: jnp.flip(t, axis=1)) if d == 1 else (lambda t: t)
        s_ctx, oc = _gla_chunked(fl(pc['q']), fl(pc['k'][:, :, d]), fl(pc['i']), fl(pc['logf'][:, :, d]), s0)
        _, ol = _gla_chunked(fl(pl['q']), fl(pl['k'][:, :, d]), fl(pl['i']), fl(pl['logf'][:, :, d]), s_ctx)
        o_lat = o_lat + fl(ol)
        if with_ctx_out:
            o_ctx = o_ctx + fl(oc)
    out_lat = _hgrn2_readout(o_lat, pl, norm_g)
    out_ctx = _hgrn2_readout(o_ctx, pc, norm_g) if with_ctx_out else None
    return out_ctx, out_lat


def _even_mixer(h_ctx, h_lat, w_in, w_out, rw, lb, hg_norm_g, with_ctx_out):
    u_ctx = h_ctx @ w_in
    u_lat = h_lat @ w_in
    ra_ctx, ra_lat = _rwkv7_mixer(u_ctx[..., :A_TOTAL], u_lat[..., :A_TOTAL], *rw, with_ctx_out)
    hb_ctx, hb_lat = _hgrn2_mixer(u_ctx[..., A_TOTAL:], u_lat[..., A_TOTAL:], lb, hg_norm_g, with_ctx_out)
    out_lat = jnp.concatenate([ra_lat, hb_lat], axis=-1) @ w_out
    out_ctx = jnp.concatenate([ra_ctx, hb_ctx], axis=-1) @ w_out if with_ctx_out else None
    return out_ctx, out_lat


def _axial_rope_tables(n_tokens):
    n_rows = n_tokens // GRID_W
    row = jnp.repeat(jnp.arange(n_rows), GRID_W).astype(jnp.float32)
    col = jnp.tile(jnp.arange(GRID_W), n_rows).astype(jnp.float32)
    n_freq = DA_HEAD // 4
    inv = ROPE_BASE ** (-jnp.arange(n_freq, dtype=jnp.float32) / n_freq)
    ar = row[:, None] * inv
    ac = col[:, None] * inv
    ang = jnp.concatenate([ar, ar, ac, ac], axis=-1)
    return jnp.cos(ang), jnp.sin(ang)


def _rotate_half(t):
    h = t.shape[-1] // 2
    return jnp.concatenate([-t[..., h:], t[..., :h]], axis=-1)


def _apply_axial_rope(t, cos, sin):
    h = DA_HEAD // 2
    rot = jnp.concatenate([_rotate_half(t[..., :h]), _rotate_half(t[..., h:])], axis=-1)
    return t * cos[None, :, None, None, :] + rot * sin[None, :, None, None, :]


def _diff_split(u):
    B, T, _ = u.shape
    q, k, v = jnp.split(u, 3, axis=-1)
    return (q.reshape(B, T, DA_HEADS, 2, DA_HEAD), k.reshape(B, T, DA_HEADS, 2, DA_HEAD),
            v.reshape(B, T, DA_HEADS, 2 * DA_HEAD))


def _diff_softmax(q, k, v, lam):
    s = jnp.einsum('bqhmd,bshmd->bhmqs', q, k).astype(jnp.float32) * (DA_HEAD ** -0.5)
    p = jax.nn.softmax(s, axis=-1)
    a = p[:, :, 0] - lam * p[:, :, 1]
    return jnp.einsum('bhqs,bshe->bqhe', a, v.astype(jnp.float32))


def _diff_readout(o, norm_g, lam_init, w_out):
    B, T = o.shape[:2]
    o = _rmsnorm(o, norm_g, RMS_EPS) * (1.0 - lam_init)
    return o.reshape(B, T, DA_W) @ w_out


def _diff_attn_mixer(h_ctx, h_lat, w_in, w_out, lam_vecs, norm_g, lam_init, cos, sin, with_ctx_out):
    B, T, _ = h_lat.shape
    qc, kc, vc = _diff_split(h_ctx @ w_in)
    ql, kl, vl = _diff_split(h_lat @ w_in)
    ql = _apply_axial_rope(ql, cos, sin)
    kl = _apply_axial_rope(kl, cos, sin)
    lv = lam_vecs.astype(jnp.float32)
    lam = jnp.exp(jnp.sum(lv[0] * lv[1])) - jnp.exp(jnp.sum(lv[2] * lv[3])) + lam_init
    k_all = jnp.concatenate([kc, kl], axis=1)
    v_all = jnp.concatenate([vc, vl], axis=1)
    n_blk = T // Q_BLOCK
    q_blocks = jnp.moveaxis(ql.reshape(B, n_blk, Q_BLOCK, DA_HEADS, 2, DA_HEAD), 1, 0)
    o_lat = lax.map(lambda qb: _diff_softmax(qb, k_all, v_all, lam), q_blocks)
    o_lat = jnp.moveaxis(o_lat, 0, 1).reshape(B, T, DA_HEADS, 2 * DA_HEAD)
    out_lat = _diff_readout(o_lat, norm_g, lam_init, w_out)
    out_ctx = _diff_readout(_diff_softmax(qc, kc, vc, lam), norm_g, lam_init, w_out) if with_ctx_out else None
    return out_ctx, out_lat


def setup_inputs(seed: int = 0) -> dict:
    key = jax.random.key(seed)
    ks = jax.random.split(key, 29)
    D = D_MODEL
    nrm = lambda i, shape, scale: scale * jax.random.normal(ks[i], shape, jnp.float32)
    return {
        'x': nrm(0, (BATCH, SEQ, D), 1.0),
        'c': nrm(1, (BATCH, D), 1.0),
        'ctx': nrm(2, (BATCH, CTX_LEN, D), 1.0),
        'c_ctx': nrm(3, (D,), 1.0),
        'w_mod': nrm(4, (DEPTH, D, N_SUB * 3 * D), 0.5 * D ** -0.5),
        'b_mod': nrm(5, (DEPTH, N_SUB * 3 * D), 0.02),
        'ln_g': 1.0 + nrm(6, (DEPTH, N_SUB, D), 0.02),
        'ln_b': nrm(7, (DEPTH, N_SUB, D), 0.02),
        'ffn_w_in': nrm(8, (DEPTH, 2, D, 2 * FF_HIDDEN), D ** -0.5),
        'ffn_w_out': nrm(9, (DEPTH, 2, FF_HIDDEN, D), DEEPNORM_BETA * FF_HIDDEN ** -0.5),
        'ev_w_in': nrm(10, (N_EVEN, D, EVEN_PROJ), D ** -0.5),
        'ev_w_out': nrm(11, (N_EVEN, EVEN_MIX, D), DEEPNORM_BETA * EVEN_MIX ** -0.5),
        'rw_mu': jax.random.uniform(ks[12], (N_EVEN, A_TOTAL), jnp.float32),
        'rw_w0': jnp.linspace(-6.0, -1.0, RW_W, dtype=jnp.float32) + nrm(13, (N_EVEN, 2, RW_W), 0.1),
        'rw_w2': nrm(14, (N_EVEN, 2, RW_DECAY_LORA, RW_W), 0.1 * RW_DECAY_LORA ** -0.5),
        'rw_a0': nrm(15, (N_EVEN, RW_W), 0.1),
        'rw_a2': nrm(16, (N_EVEN, RW_A_LORA, RW_W), 0.1 * RW_A_LORA ** -0.5),
        'rw_g2': nrm(17, (N_EVEN, RW_GATE_LORA, RW_W), RW_GATE_LORA ** -0.5),
        'rw_k_k': 0.85 + nrm(18, (N_EVEN, RW_W), 0.05),
        'rw_k_a': 1.0 + nrm(19, (N_EVEN, RW_W), 0.05),
        'rw_r_k': nrm(20, (N_EVEN, RW_HEADS, RW_HEAD), 0.1),
        'rw_gn_g': 1.0 + nrm(21, (N_EVEN, RW_W), 0.02),
        'rw_gn_b': nrm(22, (N_EVEN, RW_W), 0.02),
        'hg_lb': nrm(23, (DEPTH + 1, HG_KW), 0.1),
        'hg_norm_g': 1.0 + nrm(24, (N_EVEN, HG_DV), 0.02),
        'od_w_in': nrm(25, (N_ODD, D, 3 * DA_W), D ** -0.5),
        'od_w_out': nrm(26, (N_ODD, DA_W, D), DEEPNORM_BETA * DA_W ** -0.5),
        'da_lambda': nrm(27, (N_ODD, 4, DA_HEAD), 0.1),
        'da_norm_g': 1.0 + nrm(28, (N_ODD, 2 * DA_HEAD), 0.02),
    }


def reference(x, c, ctx, c_ctx, w_mod, b_mod, ln_g, ln_b, ffn_w_in, ffn_w_out, ev_w_in, ev_w_out,
              rw_mu, rw_w0, rw_w2, rw_a0, rw_a2, rw_g2, rw_k_k, rw_k_a, rw_r_k, rw_gn_g, rw_gn_b,
              hg_lb, hg_norm_g, od_w_in, od_w_out, da_lambda, da_norm_g):
    T = x.shape[1]
    cos, sin = _axial_rope_tables(T)
    lower_bounds = jnp.cumsum(jax.nn.softmax(hg_lb.astype(jnp.float32), axis=0), axis=0)
    h_lat, h_ctx = x, ctx
    for i in range(DEPTH):
        with_ctx_out = i < DEPTH - 1
        j = i // 2
        m_lat = _modulation(c, w_mod[i], b_mod[i])[:, None]
        m_ctx = _modulation(c_ctx, w_mod[i], b_mod[i])[None, None]
        h_lat = _ffn_sublayer(h_lat, m_lat, 0, ffn_w_in[i, 0], ffn_w_out[i, 0], ln_g[i, 0], ln_b[i, 0])
        h_ctx = _ffn_sublayer(h_ctx, m_ctx, 0, ffn_w_in[i, 0], ffn_w_out[i, 0], ln_g[i, 0], ln_b[i, 0])
        a_lat = _modulate(h_lat, m_lat, 1)
        a_ctx = _modulate(h_ctx, m_ctx, 1)
        if i % 2 == 0:
            rw = (rw_mu[j], rw_w0[j], rw_w2[j], rw_a0[j], rw_a2[j], rw_g2[j], rw_k_k[j], rw_k_a[j],
                  rw_r_k[j], rw_gn_g[j], rw_gn_b[j])
            o_ctx, o_lat = _even_mixer(a_ctx, a_lat, ev_w_in[j], ev_w_out[j], rw, lower_bounds[i],
                                       hg_norm_g[j], with_ctx_out)
        else:
            lam_init = 0.8 - 0.6 * math.exp(-0.3 * i)
            o_ctx, o_lat = _diff_attn_mixer(a_ctx, a_lat, od_w_in[j], od_w_out[j], da_lambda[j], da_norm_g[j],
                                            lam_init, cos, sin, with_ctx_out)
        h_lat = _post_norm(h_lat, o_lat, m_lat[..., 1, 2, :], 1.0, ln_g[i, 1], ln_b[i, 1])
        h_lat = _ffn_sublayer(h_lat, m_lat, 2, ffn_w_in[i, 1], ffn_w_out[i, 1], ln_g[i, 2], ln_b[i, 2])
        if with_ctx_out:
            h_ctx = _post_norm(h_ctx, o_ctx, m_ctx[..., 1, 2, :], 1.0, ln_g[i, 1], ln_b[i, 1])
            h_ctx = _ffn_sublayer(h_ctx, m_ctx, 2, ffn_w_in[i, 1], ffn_w_out[i, 1], ln_g[i, 2], ln_b[i, 2])
    return h_lat
```

```python
import functools
import math

import numpy as np
import jax
import jax.numpy as jnp
from jax import lax
from jax.experimental import pallas as pl
from jax.experimental.pallas import tpu as pltpu

F32 = jnp.float32
BF16 = jnp.bfloat16
HIGHEST = lax.Precision.HIGHEST

N_SUB = 3
LN_EPS = 1e-5
RMS_EPS = 1e-5
RW_HEADS = 8
RW_HEAD = 64
RW_W = RW_HEADS * RW_HEAD
RW_DECAY_LORA = 64
RW_A_LORA = 64
RW_GATE_LORA = 160
RW_GN_EPS = 64e-5
RW_MAIN = 3 * RW_W
RW_LORA = 2 * RW_DECAY_LORA + RW_A_LORA + RW_GATE_LORA
RW_LORA_PAD = 384
A_TOTAL = RW_MAIN + RW_LORA
A_PAD = RW_MAIN + RW_LORA_PAD
HG_HEADS = 4
HG_DK = 128
HG_KW = HG_HEADS * HG_DK
B_TOTAL = 5 * HG_KW
DA_HEADS = 8
DA_HEAD = 64
DA_W = DA_HEADS * 2 * DA_HEAD
GRID_W = 64
ROPE_BASE = 10000.0

LANES = 128
SUBLANES = 8
ROW_BLK = 256
CHUNK = 64
VMEM_LIMIT = 56 * 1024 * 1024


def _cparams(sem):
    return pltpu.CompilerParams(dimension_semantics=sem, vmem_limit_bytes=VMEM_LIMIT)


def _const_spec(shape):
    nd = len(shape)
    return pl.BlockSpec(shape, lambda *_: (0,) * nd, pipeline_mode=pl.Buffered(1))


def _dot(a, b):
    return jnp.dot(a, b, preferred_element_type=F32)


def _dot_nt(a, b):
    return lax.dot_general(a, b, (((1,), (1,)), ((), ())), preferred_element_type=F32)


def _dot_tn(a, b):
    return lax.dot_general(a, b, (((0,), (0,)), ((), ())), preferred_element_type=F32)


def _dot_hi(a, b):
    return jnp.dot(a, b, preferred_element_type=F32, precision=HIGHEST)


def _silu(x):
    return x * jax.nn.sigmoid(x)


def _layernorm(y, g, b):
    mu = jnp.mean(y, axis=-1, keepdims=True)
    yc = y - mu
    var = jnp.mean(yc * yc, axis=-1, keepdims=True)
    return yc * lax.rsqrt(var + LN_EPS) * g + b


def _mod_kernel(c_ref, w_ref, b_ref, o_ref):
    o_ref[...] = _dot_hi(_silu(c_ref[...]), w_ref[...]) + b_ref[...]


def _modulation(cvec, w_mod, b_mod):
    depth, d, width = w_mod.shape
    rows = cvec.shape[0]
    tn = 1024
    return pl.pallas_call(
        _mod_kernel,
        grid=(depth, width // tn),
        in_specs=[
            pl.BlockSpec((rows, d), lambda l, j: (0, 0)),
            pl.BlockSpec((None, d, tn), lambda l, j: (l, 0, j)),
            pl.BlockSpec((None, 1, tn), lambda l, j: (l, 0, j)),
        ],
        out_specs=pl.BlockSpec((None, rows, tn), lambda l, j: (l, 0, j)),
        out_shape=jax.ShapeDtypeStruct((depth, rows, width), F32),
        compiler_params=_cparams(("parallel", "parallel")),
    )(cvec, w_mod, b_mod.reshape(depth, 1, width))


def _modulate_rows(h_ref, mod_ref, xm_ref, nblk):
    for s in range(nblk):
        rows = pl.ds(s * ROW_BLK, ROW_BLK)
        x = h_ref[rows, :]
        xm_ref[rows, :] = (x * (1.0 + mod_ref[s, 1:2, :]) + mod_ref[s, 0:1, :]).astype(xm_ref.dtype)


def _post_norm_rows(h_ref, out_ref, acc_ref, mod_ref, g_ref, b_ref, nblk, alpha, weight):
    for s in range(nblk):
        rows = pl.ds(s * ROW_BLK, ROW_BLK)
        y = alpha * h_ref[rows, :] + (weight * mod_ref[s, 2:3, :]) * acc_ref[rows, :]
        out_ref[rows, :] = _layernorm(y, g_ref[...], b_ref[...])


def _ffn_kernel(h_ref, mod_ref, wg_ref, wu_ref, wo_ref, g_ref, b_ref, out_ref, xm_ref, acc_ref,
                *, nblk, nf, alpha):
    _modulate_rows(h_ref, mod_ref, xm_ref, nblk)
    acc_ref[...] = jnp.zeros_like(acc_ref)

    def body(fc, carry):
        xm = xm_ref[...]
        gt = _dot(xm, wg_ref[fc])
        up = _dot(xm, wu_ref[fc])
        act = (_silu(gt) * up).astype(BF16)
        acc_ref[...] += _dot(act, wo_ref[fc])
        return carry

    lax.fori_loop(0, nf, body, 0)
    _post_norm_rows(h_ref, out_ref, acc_ref, mod_ref, g_ref, b_ref, nblk, alpha, 0.5)


def _ffn_sublayer(h, mod3, w_in, w_out, ln_g, ln_b, alpha, tm=512, tf=256):
    n, d = h.shape
    f = w_out.shape[0]
    nf = f // tf
    nblk = tm // ROW_BLK
    wg = w_in[:, :f].reshape(d, nf, tf).transpose(1, 0, 2).astype(BF16)
    wu = w_in[:, f:].reshape(d, nf, tf).transpose(1, 0, 2).astype(BF16)
    wo = w_out.reshape(nf, tf, d).astype(BF16)
    kern = functools.partial(_ffn_kernel, nblk=nblk, nf=nf, alpha=alpha)
    return pl.pallas_call(
        kern,
        grid=(n // tm,),
        in_specs=[
            pl.BlockSpec((tm, d), lambda i: (i, 0)),
            pl.BlockSpec((nblk, 3, d), lambda i: (i, 0, 0)),
            _const_spec((nf, d, tf)),
            _const_spec((nf, d, tf)),
            _const_spec((nf, tf, d)),
            _const_spec((1, d)),
            _const_spec((1, d)),
        ],
        out_specs=pl.BlockSpec((tm, d), lambda i: (i, 0)),
        out_shape=jax.ShapeDtypeStruct((n, d), F32),
        scratch_shapes=[pltpu.VMEM((tm, d), BF16), pltpu.VMEM((tm, d), F32)],
        compiler_params=_cparams(("parallel",)),
    )(h, mod3, wg, wu, wo, ln_g.reshape(1, d), ln_b.reshape(1, d))


def _even_proj_kernel(h_ref, mod_ref, w_ref, ua_ref, ub_ref, xm_ref, *, nblk):
    _modulate_rows(h_ref, mod_ref, xm_ref, nblk)
    xm = xm_ref[...]
    ua_ref[...] = _dot(xm, w_ref[:, :A_PAD])
    ub_ref[...] = _dot(xm, w_ref[:, A_PAD:])


def _even_proj(h, mod3, w_pad, tm=512):
    n, d = h.shape
    nblk = tm // ROW_BLK
    kern = functools.partial(_even_proj_kernel, nblk=nblk)
    return pl.pallas_call(
        kern,
        grid=(n // tm,),
        in_specs=[
            pl.BlockSpec((tm, d), lambda i: (i, 0)),
            pl.BlockSpec((nblk, 3, d), lambda i: (i, 0, 0)),
            _const_spec((d, A_PAD + B_TOTAL)),
        ],
        out_specs=[pl.BlockSpec((tm, A_PAD), lambda i: (i, 0)),
                   pl.BlockSpec((tm, B_TOTAL), lambda i: (i, 0))],
        out_shape=[jax.ShapeDtypeStruct((n, A_PAD), F32), jax.ShapeDtypeStruct((n, B_TOTAL), F32)],
        scratch_shapes=[pltpu.VMEM((tm, d), BF16)],
        compiler_params=_cparams(("parallel",)),
    )(h, mod3, w_pad)


def _rwkv_prep_kernel(u_ref, prev_ref, next_ref, mu_ref, wl_ref, w0_ref, a0_ref, kk_ref, ka_ref, rk_ref,
                      gsum_ref, r_out, k_out, v_out, z_out, b_out, lw_out, g_out, bonus_out,
                      *, blocks_per_seq, ctx_blocks):
    i = pl.program_id(0)
    j = i % blocks_per_seq
    at_start = jnp.logical_or(j == 0, j == ctx_blocks)
    at_end = jnp.logical_or(j == ctx_blocks - 1, j == blocks_per_seq - 1)
    u = u_ref[...]
    prev_row = jnp.where(at_start, 0.0, prev_ref[SUBLANES - 1:SUBLANES, :])
    next_row = jnp.where(at_end, 0.0, next_ref[0:1, :])
    rid = lax.broadcasted_iota(jnp.int32, u.shape, 0)
    up = jnp.where(rid == 0, prev_row, pltpu.roll(u, 1, 0))
    dn = jnp.where(rid == ROW_BLK - 1, next_row, pltpu.roll(u, ROW_BLK - 1, 0))
    u = u + mu_ref[...] * (0.5 * (up + dn) - u)

    r = u[:, 0:RW_W]
    k = u[:, RW_W:2 * RW_W]
    v = u[:, 2 * RW_W:3 * RW_W]
    lo = u[:, RW_MAIN:A_PAD]
    lane = lax.broadcasted_iota(jnp.int32, lo.shape, 1)
    lo = jnp.where(lane < 2 * RW_DECAY_LORA, jnp.tanh(lo),
                   jnp.where(lane < 2 * RW_DECAY_LORA + RW_A_LORA, lo, jax.nn.sigmoid(lo)))
    lora = _dot(lo.astype(BF16), wl_ref[...])
    a = jax.nn.sigmoid(a0_ref[...] + lora[:, 2 * RW_W:3 * RW_W])
    g_out[...] = lora[:, 3 * RW_W:4 * RW_W]
    for dd in range(2):
        x = -(w0_ref[dd:dd + 1, :] + lora[:, dd * RW_W:(dd + 1) * RW_W])
        softplus = jnp.maximum(x, 0.0) + jnp.log(1.0 + jnp.exp(-jnp.abs(x)))
        lw_out[dd] = -jnp.exp(-softplus - 0.5)
    kk = k * kk_ref[...]
    ss = _dot_hi(kk * kk, gsum_ref[...])
    kk = kk / jnp.maximum(jnp.sqrt(ss), 1e-12)
    k = k * (1.0 + (a - 1.0) * ka_ref[...])
    r_out[...] = r
    k_out[...] = k
    v_out[...] = v
    z_out[...] = -kk
    b_out[...] = kk * a
    bonus_out[...] = _dot_hi(r * k * rk_ref[...], gsum_ref[...]) * v


def _rwkv_prep(ua, mu_pad, w_lora, w0, a0, k_k, k_a, r_k, gsum, blocks_per_seq, ctx_blocks):
    n = ua.shape[0]
    nb = n // ROW_BLK
    halo = ROW_BLK // SUBLANES
    last8 = n // SUBLANES - 1
    kern = functools.partial(_rwkv_prep_kernel, blocks_per_seq=blocks_per_seq, ctx_blocks=ctx_blocks)
    row = lambda i: (i, 0)
    vec = _const_spec((1, RW_W))
    out512 = jax.ShapeDtypeStruct((n, RW_W), F32)
    return pl.pallas_call(
        kern,
        grid=(nb,),
        in_specs=[
            pl.BlockSpec((ROW_BLK, A_PAD), row),
            pl.BlockSpec((SUBLANES, A_PAD), lambda i: (jnp.maximum(i * halo - 1, 0), 0)),
            pl.BlockSpec((SUBLANES, A_PAD), lambda i: (jnp.minimum((i + 1) * halo, last8), 0)),
            _const_spec((1, A_PAD)),
            _const_spec((RW_LORA_PAD, 4 * RW_W)),
            _const_spec((2, RW_W)),
            vec, vec, vec, vec,
            _const_spec((RW_W, RW_W)),
        ],
        out_specs=[pl.BlockSpec((ROW_BLK, RW_W), row)] * 5
        + [pl.BlockSpec((2, ROW_BLK, RW_W), lambda i: (0, i, 0))]
        + [pl.BlockSpec((ROW_BLK, RW_W), row)] * 2,
        out_shape=[out512] * 5 + [jax.ShapeDtypeStruct((2, n, RW_W), F32)] + [out512] * 2,
        compiler_params=_cparams(("parallel",)),
    )(ua, ua, ua, mu_pad, w_lora, w0, a0.reshape(1, RW_W), k_k.reshape(1, RW_W), k_a.reshape(1, RW_W),
      r_k.reshape(1, RW_W), gsum)


def _chunk_index(d, c, ctx_chunks, n_chunks):
    back = jnp.where(c < ctx_chunks, ctx_chunks - 1 - c, n_chunks + ctx_chunks - 1 - c)
    return jnp.where(d == 0, c, back)


def _order_masks(d):
    row = lax.broadcasted_iota(jnp.int32, (CHUNK, CHUNK), 0)
    col = lax.broadcasted_iota(jnp.int32, (CHUNK, CHUNK), 1)
    diff = (row - col) * (1 - 2 * d)
    return diff >= 0, diff > 0


def _rwkv_scan_kernel(r_ref, k_ref, v_ref, z_ref, b_ref, lw_ref, y_ref, s_ref):
    d = pl.program_id(1)
    c = pl.program_id(2)

    @pl.when(c == 0)
    def _():
        s_ref[...] = jnp.zeros_like(s_ref)

    incl, strict = _order_masks(d)
    lw = lw_ref[...]
    cs = _dot_hi(incl.astype(F32), lw)
    tot = jnp.sum(lw, axis=0, keepdims=True)
    p_inc = jnp.exp(cs)
    p_exc = jnp.exp(cs - lw)
    p_inv = jnp.exp(-cs)
    p_end = jnp.exp(tot - cs)
    p_tot = jnp.exp(tot)
    k = k_ref[...]
    b = b_ref[...]
    zt = (z_ref[...] * p_exc).astype(BF16)
    rt = (r_ref[...] * p_inc).astype(BF16)
    bh = (b * p_inv).astype(BF16)
    kh = (k * p_inv).astype(BF16)
    be = (b * p_end).astype(BF16)
    ke = (k * p_end).astype(BF16)
    vb = v_ref[...].astype(BF16)
    for h in range(RW_HEADS):
        sl = slice(h * RW_HEAD, (h + 1) * RW_HEAD)
        s = s_ref[h]
        sb = s.astype(BF16)
        a_b = jnp.where(strict, _dot_nt(zt[:, sl], bh[:, sl]), 0.0)
        a_k = jnp.where(strict, _dot_nt(zt[:, sl], kh[:, sl]), 0.0)
        b_b = jnp.where(incl, _dot_nt(rt[:, sl], bh[:, sl]), 0.0)
        b_k = jnp.where(incl, _dot_nt(rt[:, sl], kh[:, sl]), 0.0)
        vh = vb[:, sl]
        u = _dot_nt(zt[:, sl], sb) + _dot(a_k.astype(BF16), vh)
        apow = a_b
        for it in range(6):
            u = u + _dot(apow.astype(BF16), u.astype(BF16))
            if it < 5:
                apow = _dot(apow.astype(BF16), apow.astype(BF16))
        ub = u.astype(BF16)
        y = _dot_nt(rt[:, sl], sb) + _dot(b_b.astype(BF16), ub) + _dot(b_k.astype(BF16), vh)
        y_ref[:, sl] = y
        s_ref[h] = s * p_tot[:, sl] + _dot_tn(ub, be[:, sl]) + _dot_tn(vh, ke[:, sl])


def _rwkv_scan(r, k, v, z, b, lw, batch, ctx_chunks, n_chunks):
    n = r.shape[0]

    def rows(bi, d, c):
        return (bi * n_chunks + _chunk_index(d, c, ctx_chunks, n_chunks), 0)

    spec = pl.BlockSpec((CHUNK, RW_W), rows)
    return pl.pallas_call(
        _rwkv_scan_kernel,
        grid=(batch, 2, n_chunks),
        in_specs=[spec] * 5 + [pl.BlockSpec((None, CHUNK, RW_W), lambda bi, d, c: (d,) + rows(bi, d, c))],
        out_specs=pl.BlockSpec((None, CHUNK, RW_W), lambda bi, d, c: (d,) + rows(bi, d, c)),
        out_shape=jax.ShapeDtypeStruct((2, n, RW_W), F32),
        scratch_shapes=[pltpu.VMEM((RW_HEADS, RW_HEAD, RW_HEAD), F32)],
        compiler_params=_cparams(("parallel", "parallel", "arbitrary")),
    )(r, k, v, z, b, lw)


def _hgrn_scan_kernel(q_ref, f_ref, i_ref, lb_ref, o_ref, s_ref):
    d = pl.program_id(1)
    c = pl.program_id(2)

    @pl.when(c == 0)
    def _():
        s_ref[...] = jnp.zeros_like(s_ref)

    incl, _ = _order_masks(d)
    lb = lb_ref[...]
    q = _silu(q_ref[...])
    fg = lb + (1.0 - lb) * jax.nn.sigmoid(f_ref[...])
    logf = jnp.log(fg)
    k = 1.0 - fg
    cs = _dot_hi(incl.astype(F32), logf)
    tot = jnp.sum(logf, axis=0, keepdims=True)
    qb = (q * jnp.exp(cs)).astype(BF16)
    kb = (k * jnp.exp(-cs)).astype(BF16)
    kd = (k * jnp.exp(tot - cs)).astype(BF16)
    gl = jnp.exp(tot)
    vb = i_ref[...].astype(BF16)
    for h in range(HG_HEADS):
        sl = slice(h * HG_DK, (h + 1) * HG_DK)
        s = s_ref[h]
        scores = jnp.where(incl, _dot_nt(qb[:, sl], kb[:, sl]), 0.0)
        o_ref[:, sl] = _dot(scores.astype(BF16), vb[:, sl]) + _dot_nt(qb[:, sl], s.astype(BF16))
        s_ref[h] = s * gl[:, sl] + _dot_tn(vb[:, sl], kd[:, sl])


def _hgrn_scan(ub, lb, batch, ctx_chunks, n_chunks):
    n = ub.shape[0]

    def rows(bi, d, c):
        return bi * n_chunks + _chunk_index(d, c, ctx_chunks, n_chunks)

    return pl.pallas_call(
        _hgrn_scan_kernel,
        grid=(batch, 2, n_chunks),
        in_specs=[
            pl.BlockSpec((CHUNK, HG_KW), lambda bi, d, c: (rows(bi, d, c), 0)),
            pl.BlockSpec((CHUNK, HG_KW), lambda bi, d, c: (rows(bi, d, c), 1 + d)),
            pl.BlockSpec((CHUNK, HG_KW), lambda bi, d, c: (rows(bi, d, c), 3)),
            _const_spec((1, HG_KW)),
        ],
        out_specs=pl.BlockSpec((None, CHUNK, HG_KW), lambda bi, d, c: (d, rows(bi, d, c), 0)),
        out_shape=jax.ShapeDtypeStruct((2, n, HG_KW), F32),
        scratch_shapes=[pltpu.VMEM((HG_HEADS, HG_DK, HG_DK), F32)],
        compiler_params=_cparams(("parallel", "parallel", "arbitrary")),
    )(ub, ub, ub, lb.reshape(1, HG_KW))


def _even_out_kernel(h_ref, mod_ref, y_ref, bonus_ref, grw_ref, o_ref, ghg_ref, gng_ref, gnb_ref, hgn_ref,
                     gm64_ref, gm128_ref, w_ref, g_ref, b_ref, out_ref, acc_ref, *, nblk, alpha):
    y = y_ref[0] + y_ref[1]
    mu = _dot_hi(y, gm64_ref[...])
    yc = y - mu
    var = _dot_hi(yc * yc, gm64_ref[...])
    ra = (yc * lax.rsqrt(var + RW_GN_EPS) * gng_ref[...] + gnb_ref[...] + bonus_ref[...]) * grw_ref[...]
    o = o_ref[0] + o_ref[1]
    ms = _dot_hi(o * o, gm128_ref[...])
    hb = o * lax.rsqrt(ms + RMS_EPS) * hgn_ref[...] * _silu(ghg_ref[...])
    acc_ref[...] = _dot(ra.astype(BF16), w_ref[:RW_W, :]) + _dot(hb.astype(BF16), w_ref[RW_W:, :])
    _post_norm_rows(h_ref, out_ref, acc_ref, mod_ref, g_ref, b_ref, nblk, alpha, 1.0)


def _even_out(h, mod3, y, bonus, g_rw, o, ub, gn_g, gn_b, hg_norm, gm64, gm128, w_out, ln_g, ln_b, alpha, tm=256):
    n, d = h.shape
    nblk = tm // ROW_BLK
    kern = functools.partial(_even_out_kernel, nblk=nblk, alpha=alpha)
    row = lambda i: (i, 0)
    vec = _const_spec((1, RW_W))
    return pl.pallas_call(
        kern,
        grid=(n // tm,),
        in_specs=[
            pl.BlockSpec((tm, d), row),
            pl.BlockSpec((nblk, 3, d), lambda i: (i, 0, 0)),
            pl.BlockSpec((2, tm, RW_W), lambda i: (0, i, 0)),
            pl.BlockSpec((tm, RW_W), row),
            pl.BlockSpec((tm, RW_W), row),
            pl.BlockSpec((2, tm, HG_KW), lambda i: (0, i, 0)),
            pl.BlockSpec((tm, HG_KW), lambda i: (i, 4)),
            vec, vec, vec,
            _const_spec((RW_W, RW_W)),
            _const_spec((HG_KW, HG_KW)),
            _const_spec((d, d)),
            _const_spec((1, d)),
            _const_spec((1, d)),
        ],
        out_specs=pl.BlockSpec((tm, d), row),
        out_shape=jax.ShapeDtypeStruct((n, d), F32),
        scratch_shapes=[pltpu.VMEM((tm, d), F32)],
        compiler_params=_cparams(("parallel",)),
    )(h, mod3, y, bonus, g_rw, o, ub, gn_g.reshape(1, RW_W), gn_b.reshape(1, RW_W), hg_norm, gm64, gm128,
      w_out.astype(BF16), ln_g.reshape(1, d), ln_b.reshape(1, d))


def _rope(t, cos, sin_up, sin_dn):
    return t * cos + pltpu.roll(t, LANES - 16, 1) * sin_up + pltpu.roll(t, 16, 1) * sin_dn


def _odd_proj_kernel(h_ref, mod_ref, w_ref, cos_ref, sup_ref, sdn_ref, q_ref, k_ref, v_ref, xm_ref, *, nblk):
    _modulate_rows(h_ref, mod_ref, xm_ref, nblk)
    xm = xm_ref[...]
    cos, sup, sdn = cos_ref[...], sup_ref[...], sdn_ref[...]
    q = _dot(xm, w_ref[:, :DA_W])
    k = _dot(xm, w_ref[:, DA_W:2 * DA_W])
    for h in range(DA_HEADS):
        sl = slice(h * LANES, (h + 1) * LANES)
        q_ref[:, sl] = (_rope(q[:, sl], cos, sup, sdn) * (DA_HEAD ** -0.5)).astype(q_ref.dtype)
        k_ref[:, sl] = _rope(k[:, sl], cos, sup, sdn).astype(k_ref.dtype)
    v_ref[...] = _dot(xm, w_ref[:, 2 * DA_W:]).astype(v_ref.dtype)


def _odd_proj(h, mod3, w, cos, sup, sdn, tm=512):
    n, d = h.shape
    nblk = tm // ROW_BLK
    kern = functools.partial(_odd_proj_kernel, nblk=nblk)
    row = lambda i: (i, 0)
    out = jax.ShapeDtypeStruct((n, DA_W), BF16)
    return pl.pallas_call(
        kern,
        grid=(n // tm,),
        in_specs=[
            pl.BlockSpec((tm, d), row),
            pl.BlockSpec((nblk, 3, d), lambda i: (i, 0, 0)),
            _const_spec((d, 3 * DA_W)),
            pl.BlockSpec((tm, LANES), row),
            pl.BlockSpec((tm, LANES), row),
            pl.BlockSpec((tm, LANES), row),
        ],
        out_specs=[pl.BlockSpec((tm, DA_W), row)] * 3,
        out_shape=[out, out, out],
        scratch_shapes=[pltpu.VMEM((tm, d), BF16)],
        compiler_params=_cparams(("parallel",)),
    )(h, mod3, w, cos, sup, sdn)


def _diff_attn_kernel(q_ref, k_ref, v_ref, lam_ref, ng_ref, o_ref, *, lam_init):
    lv = lam_ref[...]
    lam = (jnp.exp(jnp.sum(lv[0:1] * lv[1:2], axis=-1, keepdims=True))
           - jnp.exp(jnp.sum(lv[2:3] * lv[3:4], axis=-1, keepdims=True)) + lam_init)
    q = q_ref[...]
    k = k_ref[...]
    probs = []
    for m in range(2):
        sl = slice(m * DA_HEAD, (m + 1) * DA_HEAD)
        s = _dot_nt(q[:, sl], k[:, sl])
        e = jnp.exp(s - jnp.max(s, axis=-1, keepdims=True))
        probs.append(e / jnp.sum(e, axis=-1, keepdims=True))
    a = probs[0] - lam * probs[1]
    o = _dot(a.astype(BF16), v_ref[...])
    ms = jnp.mean(o * o, axis=-1, keepdims=True)
    o_ref[...] = (o * lax.rsqrt(ms + RMS_EPS) * ng_ref[...] * (1.0 - lam_init)).astype(o_ref.dtype)


def _diff_attn(q, k, v, lam_vecs, norm_g, lam_init, batch, seq_blocks, ctx_blocks, bq=ROW_BLK):
    s_len = seq_blocks * ROW_BLK
    lat_blocks = seq_blocks - ctx_blocks
    k3 = k.reshape(batch, s_len, DA_W)
    v3 = v.reshape(batch, s_len, DA_W)
    kern = functools.partial(_diff_attn_kernel, lam_init=lam_init)
    kv_spec = pl.BlockSpec((None, s_len, LANES), lambda b, h, i: (b, 0, h))
    return pl.pallas_call(
        kern,
        grid=(batch, DA_HEADS, lat_blocks),
        in_specs=[
            pl.BlockSpec((bq, LANES), lambda b, h, i: (b * seq_blocks + ctx_blocks + i, h)),
            kv_spec, kv_spec,
            _const_spec((4, DA_HEAD)),
            _const_spec((1, LANES)),
        ],
        out_specs=pl.BlockSpec((bq, LANES), lambda b, h, i: (b * lat_blocks + i, h)),
        out_shape=jax.ShapeDtypeStruct((batch * lat_blocks * ROW_BLK, DA_W), BF16),
        compiler_params=_cparams(("parallel", "parallel", "arbitrary")),
    )(q, k3, v3, lam_vecs, norm_g.reshape(1, LANES))


def _odd_out_kernel(h_ref, mod_ref, x_ref, w_ref, g_ref, b_ref, out_ref, acc_ref, *, nblk, alpha):
    acc_ref[...] = _dot(x_ref[...], w_ref[...])
    _post_norm_rows(h_ref, out_ref, acc_ref, mod_ref, g_ref, b_ref, nblk, alpha, 1.0)


def _odd_out(h, mod3, x, w_out, ln_g, ln_b, alpha, tm=512):
    n, d = h.shape
    nblk = tm // ROW_BLK
    kern = functools.partial(_odd_out_kernel, nblk=nblk, alpha=alpha)
    row = lambda i: (i, 0)
    return pl.pallas_call(
        kern,
        grid=(n // tm,),
        in_specs=[
            pl.BlockSpec((tm, d), row),
            pl.BlockSpec((nblk, 3, d), lambda i: (i, 0, 0)),
            pl.BlockSpec((tm, DA_W), row),
            _const_spec((DA_W, d)),
            _const_spec((1, d)),
            _const_spec((1, d)),
        ],
        out_specs=pl.BlockSpec((tm, d), row),
        out_shape=jax.ShapeDtypeStruct((n, d), F32),
        scratch_shapes=[pltpu.VMEM((tm, d), F32)],
        compiler_params=_cparams(("parallel",)),
    )(h, mod3, x, w_out.astype(BF16), ln_g.reshape(1, d), ln_b.reshape(1, d))


def _rope_tables(batch, ctx_len, t):
    n_rows = t // GRID_W
    rowp = np.repeat(np.arange(n_rows), GRID_W).astype(np.float32)
    colp = np.tile(np.arange(GRID_W), n_rows).astype(np.float32)
    n_freq = DA_HEAD // 4
    inv = jnp.asarray(ROPE_BASE, F32) ** (-jnp.arange(n_freq, dtype=F32) / n_freq)
    ar = jnp.asarray(rowp)[:, None] * inv
    ac = jnp.asarray(colp)[:, None] * inv
    ang = jnp.concatenate([ar, ar, ac, ac], axis=-1)
    cos = jnp.concatenate([jnp.ones((ctx_len, DA_HEAD), F32), jnp.cos(ang)], axis=0)
    sin = jnp.concatenate([jnp.zeros((ctx_len, DA_HEAD), F32), jnp.sin(ang)], axis=0)
    first = (np.arange(DA_HEAD) % 32) < 16
    sup = jnp.where(first, -sin, 0.0)
    sdn = jnp.where(first, 0.0, sin)
    rep = lambda a: jnp.tile(a, (batch, 2))
    return rep(cos), rep(sup), rep(sdn)


def _group_mean_matrix(width, group):
    idx = np.arange(width) // group
    return jnp.asarray((idx[:, None] == idx[None, :]).astype(np.float32) / group)


def kernel(x, c, ctx, c_ctx, w_mod, b_mod, ln_g, ln_b, ffn_w_in, ffn_w_out, ev_w_in, ev_w_out,
           rw_mu, rw_w0, rw_w2, rw_a0, rw_a2, rw_g2, rw_k_k, rw_k_a, rw_r_k, rw_gn_g, rw_gn_b,
           hg_lb, hg_norm_g, od_w_in, od_w_out, da_lambda, da_norm_g):
    batch, t, d = x.shape
    ctx_len = ctx.shape[1]
    depth = w_mod.shape[0]
    assert t % ROW_BLK == 0 and ctx_len % ROW_BLK == 0 and t % GRID_W == 0
    alpha = (2 * depth) ** 0.25
    s_len = ctx_len + t
    seq_blocks = s_len // ROW_BLK
    ctx_blocks = ctx_len // ROW_BLK
    n_chunks = s_len // CHUNK
    ctx_chunks = ctx_len // CHUNK
    n = batch * s_len

    h = jnp.concatenate([ctx, x], axis=1).reshape(n, d)

    rows_pad = -(-(batch + 1) // SUBLANES) * SUBLANES
    cvec = jnp.zeros((rows_pad, d), F32).at[:batch].set(c).at[batch].set(c_ctx)
    m_all = _modulation(cvec, w_mod, b_mod).reshape(depth, rows_pad, N_SUB, 3, d)
    blk_row = np.array([batch if j < ctx_blocks else bi for bi in range(batch) for j in range(seq_blocks)])
    m_blk = m_all[:, blk_row]

    lower_bounds = jnp.cumsum(jax.nn.softmax(hg_lb.astype(F32), axis=0), axis=0)
    gsum64 = _group_mean_matrix(RW_W, RW_HEAD) * RW_HEAD
    gm64 = _group_mean_matrix(RW_W, RW_HEAD)
    gm128 = _group_mean_matrix(HG_KW, HG_DK)

    for i in range(depth):
        j = i // 2
        with_ctx_out = i < depth - 1
        ffn = functools.partial(_ffn_sublayer, alpha=alpha)
        h = ffn(h, m_blk[i, :, 0], ffn_w_in[i, 0], ffn_w_out[i, 0], ln_g[i, 0], ln_b[i, 0])
        if i % 2 == 0:
            pad_a = jnp.zeros((d, A_PAD - A_TOTAL), F32)
            w_pad = jnp.concatenate([ev_w_in[j][:, :A_TOTAL], pad_a, ev_w_in[j][:, A_TOTAL:]], axis=1).astype(BF16)
            ua, ub = _even_proj(h, m_blk[i, :, 1], w_pad)
            mu_pad = jnp.concatenate([rw_mu[j], jnp.zeros((A_PAD - A_TOTAL,), F32)]).reshape(1, A_PAD)
            w_lora = jnp.zeros((RW_LORA_PAD, 4 * RW_W), F32)
            w_lora = w_lora.at[0:64, 0:RW_W].set(rw_w2[j, 0]).at[64:128, RW_W:2 * RW_W].set(rw_w2[j, 1])
            w_lora = w_lora.at[128:192, 2 * RW_W:3 * RW_W].set(rw_a2[j])
            w_lora = w_lora.at[192:192 + RW_GATE_LORA, 3 * RW_W:].set(rw_g2[j]).astype(BF16)
            r, k, v, z, b, lw, g_rw, bonus = _rwkv_prep(
                ua, mu_pad, w_lora, rw_w0[j], rw_a0[j], rw_k_k[j], rw_k_a[j], rw_r_k[j].reshape(RW_W), gsum64,
                seq_blocks, ctx_blocks)
            y = _rwkv_scan(r, k, v, z, b, lw, batch, ctx_chunks, n_chunks)
            o = _hgrn_scan(ub, lower_bounds[i], batch, ctx_chunks, n_chunks)
            hg_norm = jnp.tile(hg_norm_g[j], HG_HEADS).reshape(1, HG_KW)
            h = _even_out(h, m_blk[i, :, 1], y, bonus, g_rw, o, ub, rw_gn_g[j], rw_gn_b[j], hg_norm, gm64, gm128,
                          ev_w_out[j], ln_g[i, 1], ln_b[i, 1], alpha)
        else:
            lam_init = 0.8 - 0.6 * math.exp(-0.3 * i)
            cos, sup, sdn = _rope_tables(batch, ctx_len, t)
            q, k, v = _odd_proj(h, m_blk[i, :, 1], od_w_in[j].astype(BF16), cos, sup, sdn)
            att = _diff_attn(q, k, v, da_lambda[j], da_norm_g[j], lam_init, batch, seq_blocks, ctx_blocks)
            assert not with_ctx_out
            h = h.reshape(batch, s_len, d)[:, ctx_len:].reshape(batch * t, d)
            m_blk = m_blk[:, np.array([bi * seq_blocks + jb for bi in range(batch)
                                       for jb in range(ctx_blocks, seq_blocks)])]
            h = _odd_out(h, m_blk[i, :, 1], att, od_w_out[j], ln_g[i, 1], ln_b[i, 1], alpha)
        h = ffn(h, m_blk[i, :, 2], ffn_w_in[i, 1], ffn_w_out[i, 1], ln_g[i, 2], ln_b[i, 2])
    if h.shape[0] == n:
        h = h.reshape(batch, s_len, d)[:, ctx_len:]
    return h.reshape(batch, t, d)
```

```python
import functools
import math

import numpy as np
import jax
import jax.numpy as jnp
from jax import lax
from jax.experimental import pallas as pl
from jax.experimental.pallas import tpu as pltpu

F32 = jnp.float32
BF16 = jnp.bfloat16
HIGHEST = lax.Precision.HIGHEST

N_SUB = 3
LN_EPS = 1e-5
RMS_EPS = 1e-5
RW_HEADS = 8
RW_HEAD = 64
RW_W = RW_HEADS * RW_HEAD
RW_DECAY_LORA = 64
RW_A_LORA = 64
RW_GATE_LORA = 160
RW_GN_EPS = 64e-5
RW_MAIN = 3 * RW_W
RW_LORA = 2 * RW_DECAY_LORA + RW_A_LORA + RW_GATE_LORA
RW_LORA_PAD = 384
A_TOTAL = RW_MAIN + RW_LORA
A_PAD = RW_MAIN + RW_LORA_PAD
HG_HEADS = 4
HG_DK = 128
HG_KW = HG_HEADS * HG_DK
B_TOTAL = 5 * HG_KW
DA_HEADS = 8
DA_HEAD = 64
DA_W = DA_HEADS * 2 * DA_HEAD
GRID_W = 64
ROPE_BASE = 10000.0
Q_SCALE = DA_HEAD ** -0.5 * math.log2(math.e)

LANES = 128
SUBLANES = 8
ROW_BLK = 256
CHUNK = 64
VMEM_LIMIT = 56 * 1024 * 1024


def _cparams(sem):
    return pltpu.CompilerParams(dimension_semantics=sem, vmem_limit_bytes=VMEM_LIMIT)


def _const_spec(shape):
    nd = len(shape)
    return pl.BlockSpec(shape, lambda *_: (0,) * nd, pipeline_mode=pl.Buffered(1))


def _dot(a, b):
    return jnp.dot(a, b, preferred_element_type=F32)


def _dot_nt(a, b):
    return lax.dot_general(a, b, (((1,), (1,)), ((), ())), preferred_element_type=F32)


def _dot_tn(a, b):
    return lax.dot_general(a, b, (((0,), (0,)), ((), ())), preferred_element_type=F32)


def _dot_hi(a, b):
    return jnp.dot(a, b, preferred_element_type=F32, precision=HIGHEST)


def _silu(x):
    return x * jax.nn.sigmoid(x)


def _layernorm(y, g, b):
    mu = jnp.mean(y, axis=-1, keepdims=True)
    yc = y - mu
    var = jnp.mean(yc * yc, axis=-1, keepdims=True)
    return yc * lax.rsqrt(var + LN_EPS) * g + b


def _mod_kernel(c_ref, w_ref, b_ref, o_ref):
    o_ref[...] = _dot_hi(_silu(c_ref[...]), w_ref[...]) + b_ref[...]


def _modulation(cvec, w_mod, b_mod):
    depth, d, width = w_mod.shape
    rows = cvec.shape[0]
    tn = 1024
    return pl.pallas_call(
        _mod_kernel,
        grid=(depth, width // tn),
        in_specs=[
            pl.BlockSpec((rows, d), lambda l, j: (0, 0)),
            pl.BlockSpec((None, d, tn), lambda l, j: (l, 0, j)),
            pl.BlockSpec((None, 1, tn), lambda l, j: (l, 0, j)),
        ],
        out_specs=pl.BlockSpec((None, rows, tn), lambda l, j: (l, 0, j)),
        out_shape=jax.ShapeDtypeStruct((depth, rows, width), F32),
        compiler_params=_cparams(("parallel", "parallel")),
    )(cvec, w_mod, b_mod.reshape(depth, 1, width))


def _modulate_rows(h_ref, mod_ref, xm_ref, nblk):
    for s in range(nblk):
        rows = pl.ds(s * ROW_BLK, ROW_BLK)
        x = h_ref[rows, :]
        xm_ref[rows, :] = (x * (1.0 + mod_ref[s, 1:2, :]) + mod_ref[s, 0:1, :]).astype(xm_ref.dtype)


def _post_norm_rows(h_ref, out_ref, acc_ref, mod_ref, g_ref, b_ref, nblk, alpha, weight):
    for s in range(nblk):
        rows = pl.ds(s * ROW_BLK, ROW_BLK)
        y = alpha * h_ref[rows, :] + (weight * mod_ref[s, 2:3, :]) * acc_ref[rows, :]
        out_ref[rows, :] = _layernorm(y, g_ref[...], b_ref[...])


def _ffn_kernel(h_ref, mod_ref, wg_ref, wu_ref, wo_ref, g_ref, b_ref, out_ref, xm_ref, acc_ref,
                *, nblk, nf, alpha):
    _modulate_rows(h_ref, mod_ref, xm_ref, nblk)
    acc_ref[...] = jnp.zeros_like(acc_ref)

    def body(fc, carry):
        xm = xm_ref[...]
        gt = _dot(xm, wg_ref[fc])
        up = _dot(xm, wu_ref[fc])
        act = (_silu(gt) * up).astype(BF16)
        acc_ref[...] += _dot(act, wo_ref[fc])
        return carry

    lax.fori_loop(0, nf, body, 0)
    _post_norm_rows(h_ref, out_ref, acc_ref, mod_ref, g_ref, b_ref, nblk, alpha, 0.5)


def _ffn_sublayer(h, mod3, w_in, w_out, ln_g, ln_b, alpha, tm=512, tf=256):
    n, d = h.shape
    f = w_out.shape[0]
    nf = f // tf
    nblk = tm // ROW_BLK
    wg = w_in[:, :f].reshape(d, nf, tf).transpose(1, 0, 2).astype(BF16)
    wu = w_in[:, f:].reshape(d, nf, tf).transpose(1, 0, 2).astype(BF16)
    wo = w_out.reshape(nf, tf, d).astype(BF16)
    kern = functools.partial(_ffn_kernel, nblk=nblk, nf=nf, alpha=alpha)
    return pl.pallas_call(
        kern,
        grid=(n // tm,),
        in_specs=[
            pl.BlockSpec((tm, d), lambda i: (i, 0)),
            pl.BlockSpec((nblk, 3, d), lambda i: (i, 0, 0)),
            _const_spec((nf, d, tf)),
            _const_spec((nf, d, tf)),
            _const_spec((nf, tf, d)),
            _const_spec((1, d)),
            _const_spec((1, d)),
        ],
        out_specs=pl.BlockSpec((tm, d), lambda i: (i, 0)),
        out_shape=jax.ShapeDtypeStruct((n, d), F32),
        scratch_shapes=[pltpu.VMEM((tm, d), BF16), pltpu.VMEM((tm, d), F32)],
        compiler_params=_cparams(("parallel",)),
    )(h, mod3, wg, wu, wo, ln_g.reshape(1, d), ln_b.reshape(1, d))


def _even_proj_kernel(h_ref, mod_ref, w_ref, ua_ref, ub_ref, xm_ref, *, nblk):
    _modulate_rows(h_ref, mod_ref, xm_ref, nblk)
    xm = xm_ref[...]
    ua_ref[...] = _dot(xm, w_ref[:, :A_PAD])
    ub_ref[...] = _dot(xm, w_ref[:, A_PAD:])


def _even_proj(h, mod3, w_pad, tm=512):
    n, d = h.shape
    nblk = tm // ROW_BLK
    kern = functools.partial(_even_proj_kernel, nblk=nblk)
    return pl.pallas_call(
        kern,
        grid=(n // tm,),
        in_specs=[
            pl.BlockSpec((tm, d), lambda i: (i, 0)),
            pl.BlockSpec((nblk, 3, d), lambda i: (i, 0, 0)),
            _const_spec((d, A_PAD + B_TOTAL)),
        ],
        out_specs=[pl.BlockSpec((tm, A_PAD), lambda i: (i, 0)),
                   pl.BlockSpec((tm, B_TOTAL), lambda i: (i, 0))],
        out_shape=[jax.ShapeDtypeStruct((n, A_PAD), F32), jax.ShapeDtypeStruct((n, B_TOTAL), F32)],
        scratch_shapes=[pltpu.VMEM((tm, d), BF16)],
        compiler_params=_cparams(("parallel",)),
    )(h, mod3, w_pad)


def _rwkv_prep_kernel(u_ref, prev_ref, next_ref, mu_ref, wl_ref, w0_ref, a0_ref, kk_ref, ka_ref, rk_ref,
                      gsum_ref, r_out, k_out, v_out, z_out, b_out, lw_out, g_out, bonus_out,
                      *, blocks_per_seq, ctx_blocks):
    i = pl.program_id(0)
    j = i % blocks_per_seq
    at_start = jnp.logical_or(j == 0, j == ctx_blocks)
    at_end = jnp.logical_or(j == ctx_blocks - 1, j == blocks_per_seq - 1)
    u = u_ref[...]
    prev_row = jnp.where(at_start, 0.0, prev_ref[SUBLANES - 1:SUBLANES, :])
    next_row = jnp.where(at_end, 0.0, next_ref[0:1, :])
    rid = lax.broadcasted_iota(jnp.int32, u.shape, 0)
    up = jnp.where(rid == 0, prev_row, pltpu.roll(u, 1, 0))
    dn = jnp.where(rid == ROW_BLK - 1, next_row, pltpu.roll(u, ROW_BLK - 1, 0))
    u = u + mu_ref[...] * (0.5 * (up + dn) - u)

    r = u[:, 0:RW_W]
    k = u[:, RW_W:2 * RW_W]
    v = u[:, 2 * RW_W:3 * RW_W]
    lo = u[:, RW_MAIN:A_PAD]
    lane = lax.broadcasted_iota(jnp.int32, lo.shape, 1)
    lo = jnp.where(lane < 2 * RW_DECAY_LORA, jnp.tanh(lo),
                   jnp.where(lane < 2 * RW_DECAY_LORA + RW_A_LORA, lo, jax.nn.sigmoid(lo)))
    lora = _dot(lo.astype(BF16), wl_ref[...])
    a = jax.nn.sigmoid(a0_ref[...] + lora[:, 2 * RW_W:3 * RW_W])
    g_out[...] = lora[:, 3 * RW_W:4 * RW_W]
    for dd in range(2):
        x = -(w0_ref[dd:dd + 1, :] + lora[:, dd * RW_W:(dd + 1) * RW_W])
        softplus = jnp.maximum(x, 0.0) + jnp.log(1.0 + jnp.exp(-jnp.abs(x)))
        lw_out[dd] = -jnp.exp(-softplus - 0.5)
    kk = k * kk_ref[...]
    ss = _dot_hi(kk * kk, gsum_ref[...])
    kk = kk / jnp.maximum(jnp.sqrt(ss), 1e-12)
    k = k * (1.0 + (a - 1.0) * ka_ref[...])
    r_out[...] = r
    k_out[...] = k
    v_out[...] = v
    z_out[...] = -kk
    b_out[...] = kk * a
    bonus_out[...] = _dot_hi(r * k * rk_ref[...], gsum_ref[...]) * v


def _rwkv_prep(ua, mu_pad, w_lora, w0, a0, k_k, k_a, r_k, gsum, blocks_per_seq, ctx_blocks):
    n = ua.shape[0]
    nb = n // ROW_BLK
    halo = ROW_BLK // SUBLANES
    last8 = n // SUBLANES - 1
    kern = functools.partial(_rwkv_prep_kernel, blocks_per_seq=blocks_per_seq, ctx_blocks=ctx_blocks)
    row = lambda i: (i, 0)
    vec = _const_spec((1, RW_W))
    out512 = jax.ShapeDtypeStruct((n, RW_W), F32)
    return pl.pallas_call(
        kern,
        grid=(nb,),
        in_specs=[
            pl.BlockSpec((ROW_BLK, A_PAD), row),
            pl.BlockSpec((SUBLANES, A_PAD), lambda i: (jnp.maximum(i * halo - 1, 0), 0)),
            pl.BlockSpec((SUBLANES, A_PAD), lambda i: (jnp.minimum((i + 1) * halo, last8), 0)),
            _const_spec((1, A_PAD)),
            _const_spec((RW_LORA_PAD, 4 * RW_W)),
            _const_spec((2, RW_W)),
            vec, vec, vec, vec,
            _const_spec((RW_W, RW_W)),
        ],
        out_specs=[pl.BlockSpec((ROW_BLK, RW_W), row)] * 5
        + [pl.BlockSpec((2, ROW_BLK, RW_W), lambda i: (0, i, 0))]
        + [pl.BlockSpec((ROW_BLK, RW_W), row)] * 2,
        out_shape=[out512] * 5 + [jax.ShapeDtypeStruct((2, n, RW_W), F32)] + [out512] * 2,
        compiler_params=_cparams(("parallel",)),
    )(ua, ua, ua, mu_pad, w_lora, w0, a0.reshape(1, RW_W), k_k.reshape(1, RW_W), k_a.reshape(1, RW_W),
      r_k.reshape(1, RW_W), gsum)


CPB = ROW_BLK // CHUNK


def _block_index(d, g, ctx_blocks, seq_blocks):
    if d == 0:
        return g
    return jnp.where(g < ctx_blocks, ctx_blocks - 1 - g, seq_blocks + ctx_blocks - 1 - g)


def _chunk_order(d):
    return range(CPB) if d == 0 else range(CPB - 1, -1, -1)


def _incl_mask(d):
    row = lax.broadcasted_iota(jnp.int32, (CHUNK, CHUNK), 0)
    col = lax.broadcasted_iota(jnp.int32, (CHUNK, CHUNK), 1)
    return (row - col) * (1 - 2 * d) >= 0


def _rwkv_chunk_kernel(r_ref, k_ref, v_ref, z_ref, b_ref, lw_ref, rp_ref, y0_ref, mx_ref, s0_ref, pt_ref):
    r = r_ref[...]
    k = k_ref[...]
    z = z_ref[...]
    b = b_ref[...]
    vb = v_ref[...].astype(BF16)
    zeros = jnp.zeros((CHUNK, RW_HEAD), BF16)
    row = lax.broadcasted_iota(jnp.int32, (2 * CHUNK, 2 * CHUNK), 0)
    col = lax.broadcasted_iota(jnp.int32, (2 * CHUNK, 2 * CHUNK), 1)
    chains = []
    for d in range(2):
        order = ((row & (CHUNK - 1)) - (col & (CHUNK - 1))) * (1 - 2 * d)
        mask = order + (row >> 6) > 0
        lw = lw_ref[d]
        cs = _dot_hi(_incl_mask(d).astype(F32), lw)
        tot = jnp.sum(lw, axis=0, keepdims=True)
        p_inc = jnp.exp(cs)
        p_inv = jnp.exp(-cs)
        p_end = jnp.exp(tot - cs)
        zt = z * jnp.exp(cs - lw)
        rt = r * p_inc
        zr = jnp.concatenate([zt, rt], axis=0).astype(BF16)
        bk = jnp.concatenate([b * p_inv, k * p_inv], axis=0).astype(BF16)
        be = (b * p_end).astype(BF16)
        ke = (k * p_end).astype(BF16)
        pt_ref[d] = jnp.exp(tot)
        for h in range(RW_HEADS):
            sl = slice(h * RW_HEAD, (h + 1) * RW_HEAD)
            g = jnp.where(mask, _dot_nt(zr[:, sl], bk[:, sl]), 0.0)
            chains.append(dict(d=d, sl=sl, vh=vb[:, sl], zt=zt[:, sl], rt=rt[:, sl], be=be[:, sl], ke=ke[:, sl],
                               gtop=g[:CHUNK].astype(BF16), gbot=g[CHUNK:].astype(BF16), p=g[:CHUNK, :CHUNK]))
    for c in chains:
        akv = _dot(c["gtop"], jnp.concatenate([zeros, c["vh"]], axis=0))
        c["x"] = jnp.concatenate([c["zt"], akv], axis=1)
    for it in range(6):
        for c in chains:
            c["pb"] = c["p"].astype(BF16)
            c["x"] = c["x"] + _dot(c["pb"], c["x"].astype(BF16))
        if it < 5:
            for c in chains:
                c["p"] = _dot(c["pb"], c["pb"])
    for c in chains:
        c["xb"] = c["x"].astype(BF16)
        rhs = jnp.concatenate([c["xb"], jnp.concatenate([zeros, c["vh"]], axis=1)], axis=0)
        c["ry"] = _dot(c["gbot"], rhs)
    for c in chains:
        c["ms"] = _dot_tn(c["xb"], c["be"])
        c["vk"] = _dot_tn(c["vh"], c["ke"])
    for c in chains:
        d, sl = c["d"], c["sl"]
        rp_ref[d, :, sl] = (c["rt"] + c["ry"][:, :RW_HEAD]).astype(rp_ref.dtype)
        y0_ref[d, :, sl] = c["ry"][:, RW_HEAD:]
        mx_ref[d, :, sl] = c["ms"][:RW_HEAD]
        s0_ref[d, :, sl] = c["ms"][RW_HEAD:] + c["vk"]


def _rwkv_chunks(r, k, v, z, b, lw):
    n = r.shape[0]
    nc = n // CHUNK
    row = lambda i: (i, 0)
    drow = lambda i: (0, i, 0)
    spec = pl.BlockSpec((CHUNK, RW_W), row)
    dspec = pl.BlockSpec((2, CHUNK, RW_W), drow)
    big = jax.ShapeDtypeStruct((2, n, RW_W), F32)
    return pl.pallas_call(
        _rwkv_chunk_kernel,
        grid=(nc,),
        in_specs=[spec] * 5 + [dspec],
        out_specs=[dspec] * 4 + [pl.BlockSpec((2, None, 1, RW_W), lambda i: (0, i, 0, 0))],
        out_shape=[jax.ShapeDtypeStruct((2, n, RW_W), BF16), big, big, big,
                   jax.ShapeDtypeStruct((2, nc, 1, RW_W), F32)],
        compiler_params=_cparams(("parallel",)),
    )(r, k, v, z, b, lw)


def _rwkv_state_kernel(rp0, y00, mx0, s00, pt0, rp1, y01, mx1, s01, pt1, yf_ref, yb_ref, s_ref):
    @pl.when(pl.program_id(1) == 0)
    def _():
        s_ref[...] = jnp.zeros_like(s_ref)

    dirs = ((rp0, y00, mx0, s00, pt0, yf_ref), (rp1, y01, mx1, s01, pt1, yb_ref))
    heads = [(d, h) for d in range(2) for h in range(RW_HEADS)]
    state = {dh: s_ref[dh[0], dh[1]] for dh in heads}
    for step in range(CPB):
        for d, h in heads:
            rp, y0, mx, s0, pt, y_ref = dirs[d]
            cc = _chunk_order(d)[step]
            rows = slice(cc * CHUNK, (cc + 1) * CHUNK)
            sl = slice(h * RW_HEAD, (h + 1) * RW_HEAD)
            s = state[d, h]
            y_ref[rows, sl] = _dot_nt(rp[rows, sl], s.astype(BF16)) + y0[rows, sl]
            state[d, h] = s * pt[cc, :, sl] + _dot_hi(s, mx[rows, sl]) + s0[rows, sl]
    for d, h in heads:
        s_ref[d, h] = state[d, h]


def _rwkv_state(rp, y0, mx, s0, pt, batch, ctx_blocks, seq_blocks):
    n = y0.shape[1]
    in_specs = []
    for d in range(2):
        blk = lambda bi, g, d=d: bi * seq_blocks + _block_index(d, g, ctx_blocks, seq_blocks)
        big = pl.BlockSpec((None, ROW_BLK, RW_W), lambda bi, g, d=d, blk=blk: (d, blk(bi, g), 0))
        in_specs += [big] * 4 + [pl.BlockSpec((None, CPB, 1, RW_W), lambda bi, g, d=d, blk=blk: (d, blk(bi, g), 0, 0))]
    out_specs = [pl.BlockSpec((ROW_BLK, RW_W),
                              lambda bi, g, d=d: (bi * seq_blocks + _block_index(d, g, ctx_blocks, seq_blocks), 0))
                 for d in range(2)]
    out = jax.ShapeDtypeStruct((n, RW_W), F32)
    return pl.pallas_call(
        _rwkv_state_kernel,
        grid=(batch, seq_blocks),
        in_specs=in_specs,
        out_specs=out_specs,
        out_shape=[out, out],
        scratch_shapes=[pltpu.VMEM((2, RW_HEADS, RW_HEAD, RW_HEAD), F32)],
        compiler_params=_cparams(("parallel", "arbitrary")),
    )(rp, y0, mx, s0, pt, rp, y0, mx, s0, pt)


def _hgrn_scan_kernel(q0_ref, f0_ref, i0_ref, q1_ref, f1_ref, i1_ref, lb_ref, of_ref, ob_ref, s_ref):
    @pl.when(pl.program_id(1) == 0)
    def _():
        s_ref[...] = jnp.zeros_like(s_ref)

    lb = lb_ref[...]
    dirs = ((q0_ref, f0_ref, i0_ref, of_ref), (q1_ref, f1_ref, i1_ref, ob_ref))
    for d, (q_ref, f_ref, i_ref, o_ref) in enumerate(dirs):
        incl = _incl_mask(d)
        tri = incl.astype(F32)
        for cc in _chunk_order(d):
            rows = slice(cc * CHUNK, (cc + 1) * CHUNK)
            q = _silu(q_ref[rows, :])
            fg = lb + (1.0 - lb) * jax.nn.sigmoid(f_ref[rows, :])
            logf = jnp.log(fg)
            k = 1.0 - fg
            cs = _dot_hi(tri, logf)
            tot = jnp.sum(logf, axis=0, keepdims=True)
            qb = (q * jnp.exp(cs)).astype(BF16)
            kb = (k * jnp.exp(-cs)).astype(BF16)
            kd = (k * jnp.exp(tot - cs)).astype(BF16)
            gl = jnp.exp(tot)
            vb = i_ref[rows, :].astype(BF16)
            for h in range(HG_HEADS):
                sl = slice(h * HG_DK, (h + 1) * HG_DK)
                s = s_ref[d, h]
                scores = jnp.where(incl, _dot_nt(qb[:, sl], kb[:, sl]), 0.0)
                o_ref[rows, sl] = _dot(scores.astype(BF16), vb[:, sl]) + _dot_nt(qb[:, sl], s.astype(BF16))
                s_ref[d, h] = s * gl[:, sl] + _dot_tn(vb[:, sl], kd[:, sl])


def _hgrn_scan(ub, lb, batch, ctx_blocks, seq_blocks):
    n = ub.shape[0]
    in_specs = []
    out_specs = []
    for d in range(2):
        blk = lambda bi, g, d=d: bi * seq_blocks + _block_index(d, g, ctx_blocks, seq_blocks)
        in_specs += [
            pl.BlockSpec((ROW_BLK, HG_KW), lambda bi, g, blk=blk: (blk(bi, g), 0)),
            pl.BlockSpec((ROW_BLK, HG_KW), lambda bi, g, blk=blk, d=d: (blk(bi, g), 1 + d)),
            pl.BlockSpec((ROW_BLK, HG_KW), lambda bi, g, blk=blk: (blk(bi, g), 3)),
        ]
        out_specs.append(pl.BlockSpec((ROW_BLK, HG_KW), lambda bi, g, blk=blk: (blk(bi, g), 0)))
    out = jax.ShapeDtypeStruct((n, HG_KW), F32)
    return pl.pallas_call(
        _hgrn_scan_kernel,
        grid=(batch, seq_blocks),
        in_specs=in_specs + [_const_spec((1, HG_KW))],
        out_specs=out_specs,
        out_shape=[out, out],
        scratch_shapes=[pltpu.VMEM((2, HG_HEADS, HG_DK, HG_DK), F32)],
        compiler_params=_cparams(("parallel", "arbitrary")),
    )(ub, ub, ub, ub, ub, ub, lb.reshape(1, HG_KW))


def _even_out_kernel(h_ref, mod_ref, yf_ref, yb_ref, bonus_ref, grw_ref, of_ref, ob_ref, ghg_ref, gng_ref, gnb_ref,
                     hgn_ref, gm64_ref, gm128_ref, w_ref, g_ref, b_ref, out_ref, acc_ref, *, nblk, alpha):
    y = yf_ref[...] + yb_ref[...]
    mu = _dot_hi(y, gm64_ref[...])
    yc = y - mu
    var = _dot_hi(yc * yc, gm64_ref[...])
    ra = (yc * lax.rsqrt(var + RW_GN_EPS) * gng_ref[...] + gnb_ref[...] + bonus_ref[...]) * grw_ref[...]
    o = of_ref[...] + ob_ref[...]
    ms = _dot_hi(o * o, gm128_ref[...])
    hb = o * lax.rsqrt(ms + RMS_EPS) * hgn_ref[...] * _silu(ghg_ref[...])
    acc_ref[...] = _dot(ra.astype(BF16), w_ref[:RW_W, :]) + _dot(hb.astype(BF16), w_ref[RW_W:, :])
    _post_norm_rows(h_ref, out_ref, acc_ref, mod_ref, g_ref, b_ref, nblk, alpha, 1.0)


def _even_out(h, mod3, yf, yb, bonus, g_rw, of, ob, ub, gn_g, gn_b, hg_norm, gm64, gm128, w_out, ln_g, ln_b, alpha,
              tm=256):
    n, d = h.shape
    nblk = tm // ROW_BLK
    kern = functools.partial(_even_out_kernel, nblk=nblk, alpha=alpha)
    row = lambda i: (i, 0)
    vec = _const_spec((1, RW_W))
    return pl.pallas_call(
        kern,
        grid=(n // tm,),
        in_specs=[
            pl.BlockSpec((tm, d), row),
            pl.BlockSpec((nblk, 3, d), lambda i: (i, 0, 0)),
            pl.BlockSpec((tm, RW_W), row),
            pl.BlockSpec((tm, RW_W), row),
            pl.BlockSpec((tm, RW_W), row),
            pl.BlockSpec((tm, RW_W), row),
            pl.BlockSpec((tm, HG_KW), row),
            pl.BlockSpec((tm, HG_KW), row),
            pl.BlockSpec((tm, HG_KW), lambda i: (i, 4)),
            vec, vec, vec,
            _const_spec((RW_W, RW_W)),
            _const_spec((HG_KW, HG_KW)),
            _const_spec((d, d)),
            _const_spec((1, d)),
            _const_spec((1, d)),
        ],
        out_specs=pl.BlockSpec((tm, d), row),
        out_shape=jax.ShapeDtypeStruct((n, d), F32),
        scratch_shapes=[pltpu.VMEM((tm, d), F32)],
        compiler_params=_cparams(("parallel",)),
    )(h, mod3, yf, yb, bonus, g_rw, of, ob, ub, gn_g.reshape(1, RW_W), gn_b.reshape(1, RW_W), hg_norm, gm64, gm128,
      w_out.astype(BF16), ln_g.reshape(1, d), ln_b.reshape(1, d))


def _rope(t, cos, sin_up, sin_dn):
    return t * cos + pltpu.roll(t, LANES - 16, 1) * sin_up + pltpu.roll(t, 16, 1) * sin_dn


def _odd_proj_kernel(h_ref, mod_ref, w_ref, cos_ref, sup_ref, sdn_ref, q_ref, k_ref, v_ref, xm_ref, *, nblk):
    _modulate_rows(h_ref, mod_ref, xm_ref, nblk)
    xm = xm_ref[...]
    cos, sup, sdn = cos_ref[...], sup_ref[...], sdn_ref[...]
    q = _dot(xm, w_ref[:, :DA_W])
    k = _dot(xm, w_ref[:, DA_W:2 * DA_W])
    for h in range(DA_HEADS):
        sl = slice(h * LANES, (h + 1) * LANES)
        q_ref[:, sl] = (_rope(q[:, sl], cos, sup, sdn) * Q_SCALE).astype(q_ref.dtype)
        k_ref[:, sl] = _rope(k[:, sl], cos, sup, sdn).astype(k_ref.dtype)
    v_ref[...] = _dot(xm, w_ref[:, 2 * DA_W:]).astype(v_ref.dtype)


def _odd_proj(h, mod3, w, cos, sup, sdn, tm=512):
    n, d = h.shape
    nblk = tm // ROW_BLK
    kern = functools.partial(_odd_proj_kernel, nblk=nblk)
    row = lambda i: (i, 0)
    out = jax.ShapeDtypeStruct((n, DA_W), BF16)
    return pl.pallas_call(
        kern,
        grid=(n // tm,),
        in_specs=[
            pl.BlockSpec((tm, d), row),
            pl.BlockSpec((nblk, 3, d), lambda i: (i, 0, 0)),
            _const_spec((d, 3 * DA_W)),
            pl.BlockSpec((tm, LANES), row),
            pl.BlockSpec((tm, LANES), row),
            pl.BlockSpec((tm, LANES), row),
        ],
        out_specs=[pl.BlockSpec((tm, DA_W), row)] * 3,
        out_shape=[out, out, out],
        scratch_shapes=[pltpu.VMEM((tm, d), BF16)],
        compiler_params=_cparams(("parallel",)),
    )(h, mod3, w, cos, sup, sdn)


def _diff_attn_kernel(q_ref, k_ref, v_ref, lam_ref, ng_ref, o_ref, *, lam_init):
    lv = lam_ref[...]
    lam = (jnp.exp(jnp.sum(lv[0:1] * lv[1:2], axis=-1, keepdims=True))
           - jnp.exp(jnp.sum(lv[2:3] * lv[3:4], axis=-1, keepdims=True)) + lam_init)
    q = q_ref[...]
    k = k_ref[...]
    es, ls = [], []
    for m in range(2):
        sl = slice(m * DA_HEAD, (m + 1) * DA_HEAD)
        s = _dot_nt(q[:, sl], k[:, sl])
        e = jnp.exp2(s - jnp.max(s, axis=-1, keepdims=True))
        es.append(e)
        ls.append(jnp.sum(e, axis=-1, keepdims=True))
    a = es[0] - (lam * ls[0] / ls[1]) * es[1]
    o = _dot(a.astype(BF16), v_ref[...]) / ls[0]
    ms = jnp.mean(o * o, axis=-1, keepdims=True)
    o_ref[...] = (o * lax.rsqrt(ms + RMS_EPS) * ng_ref[...] * (1.0 - lam_init)).astype(o_ref.dtype)


def _diff_attn(q, k, v, lam_vecs, norm_g, lam_init, batch, seq_blocks, ctx_blocks, bq=ROW_BLK):
    s_len = seq_blocks * ROW_BLK
    lat_blocks = seq_blocks - ctx_blocks
    k3 = k.reshape(batch, s_len, DA_W)
    v3 = v.reshape(batch, s_len, DA_W)
    kern = functools.partial(_diff_attn_kernel, lam_init=lam_init)
    kv_spec = pl.BlockSpec((None, s_len, LANES), lambda b, h, i: (b, 0, h))
    return pl.pallas_call(
        kern,
        grid=(batch, DA_HEADS, lat_blocks),
        in_specs=[
            pl.BlockSpec((bq, LANES), lambda b, h, i: (b * seq_blocks + ctx_blocks + i, h)),
            kv_spec, kv_spec,
            _const_spec((4, DA_HEAD)),
            _const_spec((1, LANES)),
        ],
        out_specs=pl.BlockSpec((bq, LANES), lambda b, h, i: (b * lat_blocks + i, h)),
        out_shape=jax.ShapeDtypeStruct((batch * lat_blocks * ROW_BLK, DA_W), BF16),
        compiler_params=_cparams(("parallel", "parallel", "arbitrary")),
    )(q, k3, v3, lam_vecs, norm_g.reshape(1, LANES))


def _odd_out_kernel(h_ref, mod_ref, x_ref, w_ref, g_ref, b_ref, out_ref, acc_ref, *, nblk, alpha):
    acc_ref[...] = _dot(x_ref[...], w_ref[...])
    _post_norm_rows(h_ref, out_ref, acc_ref, mod_ref, g_ref, b_ref, nblk, alpha, 1.0)


def _odd_out(h, mod3, x, w_out, ln_g, ln_b, alpha, tm=512):
    n, d = h.shape
    nblk = tm // ROW_BLK
    kern = functools.partial(_odd_out_kernel, nblk=nblk, alpha=alpha)
    row = lambda i: (i, 0)
    return pl.pallas_call(
        kern,
        grid=(n // tm,),
        in_specs=[
            pl.BlockSpec((tm, d), row),
            pl.BlockSpec((nblk, 3, d), lambda i: (i, 0, 0)),
            pl.BlockSpec((tm, DA_W), row),
            _const_spec((DA_W, d)),
            _const_spec((1, d)),
            _const_spec((1, d)),
        ],
        out_specs=pl.BlockSpec((tm, d), row),
        out_shape=jax.ShapeDtypeStruct((n, d), F32),
        scratch_shapes=[pltpu.VMEM((tm, d), F32)],
        compiler_params=_cparams(("parallel",)),
    )(h, mod3, x, w_out.astype(BF16), ln_g.reshape(1, d), ln_b.reshape(1, d))


def _rope_tables(batch, ctx_len, t):
    n_rows = t // GRID_W
    rowp = np.repeat(np.arange(n_rows), GRID_W).astype(np.float32)
    colp = np.tile(np.arange(GRID_W), n_rows).astype(np.float32)
    n_freq = DA_HEAD // 4
    inv = jnp.asarray(ROPE_BASE, F32) ** (-jnp.arange(n_freq, dtype=F32) / n_freq)
    ar = jnp.asarray(rowp)[:, None] * inv
    ac = jnp.asarray(colp)[:, None] * inv
    ang = jnp.concatenate([ar, ar, ac, ac], axis=-1)
    cos = jnp.concatenate([jnp.ones((ctx_len, DA_HEAD), F32), jnp.cos(ang)], axis=0)
    sin = jnp.concatenate([jnp.zeros((ctx_len, DA_HEAD), F32), jnp.sin(ang)], axis=0)
    first = (np.arange(DA_HEAD) % 32) < 16
    sup = jnp.where(first, -sin, 0.0)
    sdn = jnp.where(first, 0.0, sin)
    rep = lambda a: jnp.tile(a, (batch, 2))
    return rep(cos), rep(sup), rep(sdn)


def _group_mean_matrix(width, group):
    idx = np.arange(width) // group
    return jnp.asarray((idx[:, None] == idx[None, :]).astype(np.float32) / group)


def kernel(x, c, ctx, c_ctx, w_mod, b_mod, ln_g, ln_b, ffn_w_in, ffn_w_out, ev_w_in, ev_w_out,
           rw_mu, rw_w0, rw_w2, rw_a0, rw_a2, rw_g2, rw_k_k, rw_k_a, rw_r_k, rw_gn_g, rw_gn_b,
           hg_lb, hg_norm_g, od_w_in, od_w_out, da_lambda, da_norm_g):
    batch, t, d = x.shape
    ctx_len = ctx.shape[1]
    depth = w_mod.shape[0]
    assert t % ROW_BLK == 0 and ctx_len % ROW_BLK == 0 and t % GRID_W == 0
    alpha = (2 * depth) ** 0.25
    s_len = ctx_len + t
    seq_blocks = s_len // ROW_BLK
    ctx_blocks = ctx_len // ROW_BLK
    n = batch * s_len

    h = jnp.concatenate([ctx, x], axis=1).reshape(n, d)

    rows_pad = -(-(batch + 1) // SUBLANES) * SUBLANES
    cvec = jnp.zeros((rows_pad, d), F32).at[:batch].set(c).at[batch].set(c_ctx)
    m_all = _modulation(cvec, w_mod, b_mod).reshape(depth, rows_pad, N_SUB, 3, d)
    blk_row = np.array([batch if j < ctx_blocks else bi for bi in range(batch) for j in range(seq_blocks)])
    m_blk = m_all[:, blk_row]

    lower_bounds = jnp.cumsum(jax.nn.softmax(hg_lb.astype(F32), axis=0), axis=0)
    gsum64 = _group_mean_matrix(RW_W, RW_HEAD) * RW_HEAD
    gm64 = _group_mean_matrix(RW_W, RW_HEAD)
    gm128 = _group_mean_matrix(HG_KW, HG_DK)

    for i in range(depth):
        j = i // 2
        with_ctx_out = i < depth - 1
        ffn = functools.partial(_ffn_sublayer, alpha=alpha)
        h = ffn(h, m_blk[i, :, 0], ffn_w_in[i, 0], ffn_w_out[i, 0], ln_g[i, 0], ln_b[i, 0])
        if i % 2 == 0:
            pad_a = jnp.zeros((d, A_PAD - A_TOTAL), F32)
            w_pad = jnp.concatenate([ev_w_in[j][:, :A_TOTAL], pad_a, ev_w_in[j][:, A_TOTAL:]], axis=1).astype(BF16)
            ua, ub = _even_proj(h, m_blk[i, :, 1], w_pad)
            mu_pad = jnp.concatenate([rw_mu[j], jnp.zeros((A_PAD - A_TOTAL,), F32)]).reshape(1, A_PAD)
            w_lora = jnp.zeros((RW_LORA_PAD, 4 * RW_W), F32)
            w_lora = w_lora.at[0:64, 0:RW_W].set(rw_w2[j, 0]).at[64:128, RW_W:2 * RW_W].set(rw_w2[j, 1])
            w_lora = w_lora.at[128:192, 2 * RW_W:3 * RW_W].set(rw_a2[j])
            w_lora = w_lora.at[192:192 + RW_GATE_LORA, 3 * RW_W:].set(rw_g2[j]).astype(BF16)
            r, k, v, z, b, lw, g_rw, bonus = _rwkv_prep(
                ua, mu_pad, w_lora, rw_w0[j], rw_a0[j], rw_k_k[j], rw_k_a[j], rw_r_k[j].reshape(RW_W), gsum64,
                seq_blocks, ctx_blocks)
            yf, yb = _rwkv_state(*_rwkv_chunks(r, k, v, z, b, lw), batch, ctx_blocks, seq_blocks)
            of, ob = _hgrn_scan(ub, lower_bounds[i], batch, ctx_blocks, seq_blocks)
            hg_norm = jnp.tile(hg_norm_g[j], HG_HEADS).reshape(1, HG_KW)
            h = _even_out(h, m_blk[i, :, 1], yf, yb, bonus, g_rw, of, ob, ub, rw_gn_g[j], rw_gn_b[j], hg_norm, gm64, gm128,
                          ev_w_out[j], ln_g[i, 1], ln_b[i, 1], alpha)
        else:
            lam_init = 0.8 - 0.6 * math.exp(-0.3 * i)
            cos, sup, sdn = _rope_tables(batch, ctx_len, t)
            q, k, v = _odd_proj(h, m_blk[i, :, 1], od_w_in[j].astype(BF16), cos, sup, sdn)
            att = _diff_attn(q, k, v, da_lambda[j], da_norm_g[j], lam_init, batch, seq_blocks, ctx_blocks)
            assert not with_ctx_out
            h = h.reshape(batch, s_len, d)[:, ctx_len:].reshape(batch * t, d)
            m_blk = m_blk[:, np.array([bi * seq_blocks + jb for bi in range(batch)
                                       for jb in range(ctx_blocks, seq_blocks)])]
            h = _odd_out(h, m_blk[i, :, 1], att, od_w_out[j], ln_g[i, 1], ln_b[i, 1], alpha)
        h = ffn(h, m_blk[i, :, 2], ffn_w_in[i, 1], ffn_w_out[i, 1], ln_g[i, 2], ln_b[i, 2])
    if h.shape[0] == n:
        h = h.reshape(batch, s_len, d)[:, ctx_len:]
    return h.reshape(batch, t, d)
```

```python
import functools
import math

import numpy as np
import jax
import jax.numpy as jnp
from jax import lax
from jax.experimental import pallas as pl
from jax.experimental.pallas import tpu as pltpu

F32 = jnp.float32
BF16 = jnp.bfloat16
HIGHEST = lax.Precision.HIGHEST

N_SUB = 3
LN_EPS = 1e-5
RMS_EPS = 1e-5
RW_HEADS = 8
RW_HEAD = 64
RW_W = RW_HEADS * RW_HEAD
RW_DECAY_LORA = 64
RW_A_LORA = 64
RW_GATE_LORA = 160
RW_GN_EPS = 64e-5
RW_MAIN = 3 * RW_W
RW_LORA = 2 * RW_DECAY_LORA + RW_A_LORA + RW_GATE_LORA
RW_LORA_PAD = 384
A_TOTAL = RW_MAIN + RW_LORA
A_PAD = RW_MAIN + RW_LORA_PAD
HG_HEADS = 4
HG_DK = 128
HG_KW = HG_HEADS * HG_DK
B_TOTAL = 5 * HG_KW
DA_HEADS = 8
DA_HEAD = 64
DA_W = DA_HEADS * 2 * DA_HEAD
GRID_W = 64
ROPE_BASE = 10000.0
Q_SCALE = DA_HEAD ** -0.5 * math.log2(math.e)

LANES = 128
SUBLANES = 8
ROW_BLK = 256
CHUNK = 64
VMEM_LIMIT = 56 * 1024 * 1024


def _cparams(sem):
    return pltpu.CompilerParams(dimension_semantics=sem, vmem_limit_bytes=VMEM_LIMIT)


def _const_spec(shape):
    nd = len(shape)
    return pl.BlockSpec(shape, lambda *_: (0,) * nd, pipeline_mode=pl.Buffered(1))


def _dot(a, b):
    return jnp.dot(a, b, preferred_element_type=F32)


def _dot_nt(a, b):
    return lax.dot_general(a, b, (((1,), (1,)), ((), ())), preferred_element_type=F32)


def _dot_tn(a, b):
    return lax.dot_general(a, b, (((0,), (0,)), ((), ())), preferred_element_type=F32)


def _dot_hi(a, b):
    return jnp.dot(a, b, preferred_element_type=F32, precision=HIGHEST)


def _split(x, pieces):
    out = []
    for _ in range(pieces):
        p = x.astype(BF16)
        out.append(p)
        x = x - p.astype(F32)
    return out


def _group_sum(x, gmat):
    return sum(_dot(p, gmat) for p in _split(x, 2))


def _cumsum_rows(tri, x):
    return sum(_dot(tri, p) for p in _split(x, 3))


def _silu(x):
    return x * jax.nn.sigmoid(x)


def _layernorm(y, g, b):
    mu = jnp.mean(y, axis=-1, keepdims=True)
    yc = y - mu
    var = jnp.mean(yc * yc, axis=-1, keepdims=True)
    return yc * lax.rsqrt(var + LN_EPS) * g + b


def _mod_kernel(c_ref, w_ref, b_ref, o_ref):
    o_ref[...] = _dot_hi(_silu(c_ref[...]), w_ref[...]) + b_ref[...]


def _modulation(cvec, w_mod, b_mod):
    depth, d, width = w_mod.shape
    rows = cvec.shape[0]
    tn = 1024
    return pl.pallas_call(
        _mod_kernel,
        grid=(depth, width // tn),
        in_specs=[
            pl.BlockSpec((rows, d), lambda l, j: (0, 0)),
            pl.BlockSpec((None, d, tn), lambda l, j: (l, 0, j)),
            pl.BlockSpec((None, 1, tn), lambda l, j: (l, 0, j)),
        ],
        out_specs=pl.BlockSpec((None, rows, tn), lambda l, j: (l, 0, j)),
        out_shape=jax.ShapeDtypeStruct((depth, rows, width), F32),
        compiler_params=_cparams(("parallel", "parallel")),
    )(cvec, w_mod, b_mod.reshape(depth, 1, width))


def _modulate_rows(h_ref, mod_ref, xm_ref, nblk):
    for s in range(nblk):
        rows = pl.ds(s * ROW_BLK, ROW_BLK)
        x = h_ref[rows, :]
        xm_ref[rows, :] = (x * (1.0 + mod_ref[s, 1:2, :]) + mod_ref[s, 0:1, :]).astype(xm_ref.dtype)


def _post_norm_rows(h_ref, out_ref, acc_ref, mod_ref, g_ref, b_ref, nblk, alpha, weight):
    for s in range(nblk):
        rows = pl.ds(s * ROW_BLK, ROW_BLK)
        y = alpha * h_ref[rows, :] + (weight * mod_ref[s, 2:3, :]) * acc_ref[rows, :]
        out_ref[rows, :] = _layernorm(y, g_ref[...], b_ref[...])


def _ffn_kernel(h_ref, mod_ref, wg_ref, wu_ref, wo_ref, g_ref, b_ref, out_ref, xm_ref, acc_ref,
                *, nblk, nf, alpha):
    _modulate_rows(h_ref, mod_ref, xm_ref, nblk)
    acc_ref[...] = jnp.zeros_like(acc_ref)

    def body(fc, carry):
        xm = xm_ref[...]
        gt = _dot(xm, wg_ref[fc])
        up = _dot(xm, wu_ref[fc])
        act = (_silu(gt) * up).astype(BF16)
        acc_ref[...] += _dot(act, wo_ref[fc])
        return carry

    lax.fori_loop(0, nf, body, 0, unroll=True)
    _post_norm_rows(h_ref, out_ref, acc_ref, mod_ref, g_ref, b_ref, nblk, alpha, 0.5)


def _ffn_sublayer(h, mod3, w_in, w_out, ln_g, ln_b, alpha, tm=512, tf=256):
    n, d = h.shape
    f = w_out.shape[0]
    nf = f // tf
    nblk = tm // ROW_BLK
    wg = w_in[:, :f].reshape(d, nf, tf).transpose(1, 0, 2).astype(BF16)
    wu = w_in[:, f:].reshape(d, nf, tf).transpose(1, 0, 2).astype(BF16)
    wo = w_out.reshape(nf, tf, d).astype(BF16)
    kern = functools.partial(_ffn_kernel, nblk=nblk, nf=nf, alpha=alpha)
    return pl.pallas_call(
        kern,
        grid=(n // tm,),
        in_specs=[
            pl.BlockSpec((tm, d), lambda i: (i, 0)),
            pl.BlockSpec((nblk, 3, d), lambda i: (i, 0, 0)),
            _const_spec((nf, d, tf)),
            _const_spec((nf, d, tf)),
            _const_spec((nf, tf, d)),
            _const_spec((1, d)),
            _const_spec((1, d)),
        ],
        out_specs=pl.BlockSpec((tm, d), lambda i: (i, 0)),
        out_shape=jax.ShapeDtypeStruct((n, d), F32),
        scratch_shapes=[pltpu.VMEM((tm, d), BF16), pltpu.VMEM((tm, d), F32)],
        compiler_params=_cparams(("parallel",)),
    )(h, mod3, wg, wu, wo, ln_g.reshape(1, d), ln_b.reshape(1, d))


def _even_proj_kernel(h_ref, mod_ref, w_ref, ua_ref, ub_ref, xm_ref, *, nblk):
    _modulate_rows(h_ref, mod_ref, xm_ref, nblk)
    xm = xm_ref[...]
    ua_ref[...] = _dot(xm, w_ref[:, :A_PAD])
    ub_ref[...] = _dot(xm, w_ref[:, A_PAD:])


def _even_proj(h, mod3, w_pad, tm=512):
    n, d = h.shape
    nblk = tm // ROW_BLK
    kern = functools.partial(_even_proj_kernel, nblk=nblk)
    return pl.pallas_call(
        kern,
        grid=(n // tm,),
        in_specs=[
            pl.BlockSpec((tm, d), lambda i: (i, 0)),
            pl.BlockSpec((nblk, 3, d), lambda i: (i, 0, 0)),
            _const_spec((d, A_PAD + B_TOTAL)),
        ],
        out_specs=[pl.BlockSpec((tm, A_PAD), lambda i: (i, 0)),
                   pl.BlockSpec((tm, B_TOTAL), lambda i: (i, 0))],
        out_shape=[jax.ShapeDtypeStruct((n, A_PAD), F32), jax.ShapeDtypeStruct((n, B_TOTAL), F32)],
        scratch_shapes=[pltpu.VMEM((tm, d), BF16)],
        compiler_params=_cparams(("parallel",)),
    )(h, mod3, w_pad)


def _rwkv_prep_kernel(u_ref, prev_ref, next_ref, mu_ref, wl_ref, w0_ref, a0_ref, kk_ref, ka_ref, rk_ref,
                      gsum_ref, r_out, k_out, v_out, z_out, b_out, lw_out, g_out, bonus_out,
                      *, blocks_per_seq, ctx_blocks):
    i = pl.program_id(0)
    j = i % blocks_per_seq
    at_start = jnp.logical_or(j == 0, j == ctx_blocks)
    at_end = jnp.logical_or(j == ctx_blocks - 1, j == blocks_per_seq - 1)
    u = u_ref[...]
    prev_row = jnp.where(at_start, 0.0, prev_ref[SUBLANES - 1:SUBLANES, :])
    next_row = jnp.where(at_end, 0.0, next_ref[0:1, :])
    rid = lax.broadcasted_iota(jnp.int32, u.shape, 0)
    up = jnp.where(rid == 0, prev_row, pltpu.roll(u, 1, 0))
    dn = jnp.where(rid == ROW_BLK - 1, next_row, pltpu.roll(u, ROW_BLK - 1, 0))
    u = u + mu_ref[...] * (0.5 * (up + dn) - u)

    r = u[:, 0:RW_W]
    k = u[:, RW_W:2 * RW_W]
    v = u[:, 2 * RW_W:3 * RW_W]
    lo = u[:, RW_MAIN:A_PAD]
    lane = lax.broadcasted_iota(jnp.int32, lo.shape, 1)
    lo = jnp.where(lane < 2 * RW_DECAY_LORA, jnp.tanh(lo),
                   jnp.where(lane < 2 * RW_DECAY_LORA + RW_A_LORA, lo, jax.nn.sigmoid(lo)))
    lora = _dot(lo.astype(BF16), wl_ref[...])
    a = jax.nn.sigmoid(a0_ref[...] + lora[:, 2 * RW_W:3 * RW_W])
    g_out[...] = lora[:, 3 * RW_W:4 * RW_W]
    for dd in range(2):
        x = -(w0_ref[dd:dd + 1, :] + lora[:, dd * RW_W:(dd + 1) * RW_W])
        softplus = jnp.maximum(x, 0.0) + jnp.log(1.0 + jnp.exp(-jnp.abs(x)))
        lw_out[dd] = -jnp.exp(-softplus - 0.5)
    kk = k * kk_ref[...]
    ss = _group_sum(kk * kk, gsum_ref[...])
    kk = kk / jnp.maximum(jnp.sqrt(ss), 1e-12)
    k = k * (1.0 + (a - 1.0) * ka_ref[...])
    r_out[...] = r
    k_out[...] = k
    v_out[...] = v
    z_out[...] = -kk
    b_out[...] = kk * a
    bonus_out[...] = _group_sum(r * k * rk_ref[...], gsum_ref[...]) * v


def _rwkv_prep(ua, mu_pad, w_lora, w0, a0, k_k, k_a, r_k, gsum, blocks_per_seq, ctx_blocks):
    n = ua.shape[0]
    nb = n // ROW_BLK
    halo = ROW_BLK // SUBLANES
    last8 = n // SUBLANES - 1
    kern = functools.partial(_rwkv_prep_kernel, blocks_per_seq=blocks_per_seq, ctx_blocks=ctx_blocks)
    row = lambda i: (i, 0)
    vec = _const_spec((1, RW_W))
    out512 = jax.ShapeDtypeStruct((n, RW_W), F32)
    return pl.pallas_call(
        kern,
        grid=(nb,),
        in_specs=[
            pl.BlockSpec((ROW_BLK, A_PAD), row),
            pl.BlockSpec((SUBLANES, A_PAD), lambda i: (jnp.maximum(i * halo - 1, 0), 0)),
            pl.BlockSpec((SUBLANES, A_PAD), lambda i: (jnp.minimum((i + 1) * halo, last8), 0)),
            _const_spec((1, A_PAD)),
            _const_spec((RW_LORA_PAD, 4 * RW_W)),
            _const_spec((2, RW_W)),
            vec, vec, vec, vec,
            _const_spec((RW_W, RW_W)),
        ],
        out_specs=[pl.BlockSpec((ROW_BLK, RW_W), row)] * 5
        + [pl.BlockSpec((2, ROW_BLK, RW_W), lambda i: (0, i, 0))]
        + [pl.BlockSpec((ROW_BLK, RW_W), row)] * 2,
        out_shape=[out512] * 5 + [jax.ShapeDtypeStruct((2, n, RW_W), F32)] + [out512] * 2,
        compiler_params=_cparams(("parallel",)),
    )(ua, ua, ua, mu_pad, w_lora, w0, a0.reshape(1, RW_W), k_k.reshape(1, RW_W), k_a.reshape(1, RW_W),
      r_k.reshape(1, RW_W), gsum)


CPB = ROW_BLK // CHUNK


def _block_index(d, g, ctx_blocks, seq_blocks):
    if d == 0:
        return g
    return jnp.where(g < ctx_blocks, ctx_blocks - 1 - g, seq_blocks + ctx_blocks - 1 - g)


def _chunk_order(d):
    return range(CPB) if d == 0 else range(CPB - 1, -1, -1)


def _incl_mask(d):
    row = lax.broadcasted_iota(jnp.int32, (CHUNK, CHUNK), 0)
    col = lax.broadcasted_iota(jnp.int32, (CHUNK, CHUNK), 1)
    return (row - col) * (1 - 2 * d) >= 0


def _rwkv_chunk_kernel(r_ref, k_ref, v_ref, z_ref, b_ref, lw_ref, rp_ref, y0_ref, mx_ref, s0_ref, pt_ref):
    r = r_ref[...]
    k = k_ref[...]
    z = z_ref[...]
    b = b_ref[...]
    vb = v_ref[...].astype(BF16)
    zeros = jnp.zeros((CHUNK, RW_HEAD), BF16)
    row = lax.broadcasted_iota(jnp.int32, (2 * CHUNK, 2 * CHUNK), 0)
    col = lax.broadcasted_iota(jnp.int32, (2 * CHUNK, 2 * CHUNK), 1)
    chains = []
    for d in range(2):
        order = ((row & (CHUNK - 1)) - (col & (CHUNK - 1))) * (1 - 2 * d)
        mask = order + (row >> 6) > 0
        lw = lw_ref[d]
        cs = _cumsum_rows(_incl_mask(d).astype(BF16), lw)
        tot = jnp.sum(lw, axis=0, keepdims=True)
        p_inc = jnp.exp(cs)
        p_inv = jnp.exp(-cs)
        p_end = jnp.exp(tot - cs)
        zt = z * jnp.exp(cs - lw)
        rt = r * p_inc
        zr = jnp.concatenate([zt, rt], axis=0).astype(BF16)
        bk = jnp.concatenate([b * p_inv, k * p_inv], axis=0).astype(BF16)
        be = (b * p_end).astype(BF16)
        ke = (k * p_end).astype(BF16)
        pt_ref[d] = jnp.exp(tot)
        for h in range(RW_HEADS):
            sl = slice(h * RW_HEAD, (h + 1) * RW_HEAD)
            g = jnp.where(mask, _dot_nt(zr[:, sl], bk[:, sl]), 0.0)
            chains.append(dict(d=d, sl=sl, vh=vb[:, sl], zt=zt[:, sl], rt=rt[:, sl], be=be[:, sl], ke=ke[:, sl],
                               gtop=g[:CHUNK].astype(BF16), gbot=g[CHUNK:].astype(BF16), p=g[:CHUNK, :CHUNK]))
    for c in chains:
        akv = _dot(c["gtop"], jnp.concatenate([zeros, c["vh"]], axis=0))
        c["x"] = jnp.concatenate([c["zt"], akv], axis=1)
    for it in range(6):
        for c in chains:
            c["pb"] = c["p"].astype(BF16)
            c["x"] = c["x"] + _dot(c["pb"], c["x"].astype(BF16))
        if it < 5:
            for c in chains:
                c["p"] = _dot(c["pb"], c["pb"])
    for c in chains:
        c["xb"] = c["x"].astype(BF16)
        rhs = jnp.concatenate([c["xb"], jnp.concatenate([zeros, c["vh"]], axis=1)], axis=0)
        c["ry"] = _dot(c["gbot"], rhs)
    for c in chains:
        c["ms"] = _dot_tn(c["xb"], c["be"])
        c["vk"] = _dot_tn(c["vh"], c["ke"])
    for c in chains:
        d, sl = c["d"], c["sl"]
        rp_ref[d, :, sl] = (c["rt"] + c["ry"][:, :RW_HEAD]).astype(rp_ref.dtype)
        y0_ref[d, :, sl] = c["ry"][:, RW_HEAD:]
        mx_ref[d, :, sl] = c["ms"][:RW_HEAD]
        s0_ref[d, :, sl] = c["ms"][RW_HEAD:] + c["vk"]


def _rwkv_chunks(r, k, v, z, b, lw):
    n = r.shape[0]
    nc = n // CHUNK
    row = lambda i: (i, 0)
    drow = lambda i: (0, i, 0)
    spec = pl.BlockSpec((CHUNK, RW_W), row)
    dspec = pl.BlockSpec((2, CHUNK, RW_W), drow)
    big = jax.ShapeDtypeStruct((2, n, RW_W), F32)
    return pl.pallas_call(
        _rwkv_chunk_kernel,
        grid=(nc,),
        in_specs=[spec] * 5 + [dspec],
        out_specs=[dspec] * 4 + [pl.BlockSpec((2, None, 1, RW_W), lambda i: (0, i, 0, 0))],
        out_shape=[jax.ShapeDtypeStruct((2, n, RW_W), BF16), big, big, big,
                   jax.ShapeDtypeStruct((2, nc, 1, RW_W), F32)],
        compiler_params=_cparams(("parallel",)),
    )(r, k, v, z, b, lw)


def _rwkv_state_kernel(rp0, y00, mx0, s00, pt0, rp1, y01, mx1, s01, pt1, yf_ref, yb_ref, s_ref):
    @pl.when(pl.program_id(1) == 0)
    def _():
        s_ref[...] = jnp.zeros_like(s_ref)

    dirs = ((rp0, y00, mx0, s00, pt0, yf_ref), (rp1, y01, mx1, s01, pt1, yb_ref))
    heads = [(d, h) for d in range(2) for h in range(RW_HEADS)]
    state = {dh: s_ref[dh[0], dh[1]] for dh in heads}
    for step in range(CPB):
        for d, h in heads:
            rp, y0, mx, s0, pt, y_ref = dirs[d]
            cc = _chunk_order(d)[step]
            rows = slice(cc * CHUNK, (cc + 1) * CHUNK)
            sl = slice(h * RW_HEAD, (h + 1) * RW_HEAD)
            s = state[d, h]
            y_ref[rows, sl] = _dot_nt(rp[rows, sl], s.astype(BF16)) + y0[rows, sl]
            state[d, h] = s * pt[cc, :, sl] + _dot_hi(s, mx[rows, sl]) + s0[rows, sl]
    for d, h in heads:
        s_ref[d, h] = state[d, h]


def _rwkv_state(rp, y0, mx, s0, pt, batch, ctx_blocks, seq_blocks):
    n = y0.shape[1]
    in_specs = []
    for d in range(2):
        blk = lambda bi, g, d=d: bi * seq_blocks + _block_index(d, g, ctx_blocks, seq_blocks)
        big = pl.BlockSpec((None, ROW_BLK, RW_W), lambda bi, g, d=d, blk=blk: (d, blk(bi, g), 0))
        in_specs += [big] * 4 + [pl.BlockSpec((None, CPB, 1, RW_W), lambda bi, g, d=d, blk=blk: (d, blk(bi, g), 0, 0))]
    out_specs = [pl.BlockSpec((ROW_BLK, RW_W),
                              lambda bi, g, d=d: (bi * seq_blocks + _block_index(d, g, ctx_blocks, seq_blocks), 0))
                 for d in range(2)]
    out = jax.ShapeDtypeStruct((n, RW_W), F32)
    return pl.pallas_call(
        _rwkv_state_kernel,
        grid=(batch, seq_blocks),
        in_specs=in_specs,
        out_specs=out_specs,
        out_shape=[out, out],
        scratch_shapes=[pltpu.VMEM((2, RW_HEADS, RW_HEAD, RW_HEAD), F32)],
        compiler_params=_cparams(("parallel", "arbitrary")),
    )(rp, y0, mx, s0, pt, rp, y0, mx, s0, pt)


def _hgrn_scan_kernel(q0_ref, f0_ref, i0_ref, q1_ref, f1_ref, i1_ref, lb_ref, of_ref, ob_ref, s_ref):
    @pl.when(pl.program_id(1) == 0)
    def _():
        s_ref[...] = jnp.zeros_like(s_ref)

    lb = lb_ref[...]
    dirs = ((q0_ref, f0_ref, i0_ref, of_ref), (q1_ref, f1_ref, i1_ref, ob_ref))
    for d, (q_ref, f_ref, i_ref, o_ref) in enumerate(dirs):
        incl = _incl_mask(d)
        tri = incl.astype(BF16)
        for cc in _chunk_order(d):
            rows = slice(cc * CHUNK, (cc + 1) * CHUNK)
            q = _silu(q_ref[rows, :])
            fg = lb + (1.0 - lb) * jax.nn.sigmoid(f_ref[rows, :])
            logf = jnp.log(fg)
            k = 1.0 - fg
            cs = _cumsum_rows(tri, logf)
            tot = jnp.sum(logf, axis=0, keepdims=True)
            qb = (q * jnp.exp(cs)).astype(BF16)
            kb = (k * jnp.exp(-cs)).astype(BF16)
            kd = (k * jnp.exp(tot - cs)).astype(BF16)
            gl = jnp.exp(tot)
            vb = i_ref[rows, :].astype(BF16)
            for h in range(HG_HEADS):
                sl = slice(h * HG_DK, (h + 1) * HG_DK)
                s = s_ref[d, h]
                scores = jnp.where(incl, _dot_nt(qb[:, sl], kb[:, sl]), 0.0)
                o_ref[rows, sl] = _dot(scores.astype(BF16), vb[:, sl]) + _dot_nt(qb[:, sl], s.astype(BF16))
                s_ref[d, h] = s * gl[:, sl] + _dot_tn(vb[:, sl], kd[:, sl])


def _hgrn_scan(ub, lb, batch, ctx_blocks, seq_blocks):
    n = ub.shape[0]
    in_specs = []
    out_specs = []
    for d in range(2):
        blk = lambda bi, g, d=d: bi * seq_blocks + _block_index(d, g, ctx_blocks, seq_blocks)
        in_specs += [
            pl.BlockSpec((ROW_BLK, HG_KW), lambda bi, g, blk=blk: (blk(bi, g), 0)),
            pl.BlockSpec((ROW_BLK, HG_KW), lambda bi, g, blk=blk, d=d: (blk(bi, g), 1 + d)),
            pl.BlockSpec((ROW_BLK, HG_KW), lambda bi, g, blk=blk: (blk(bi, g), 3)),
        ]
        out_specs.append(pl.BlockSpec((ROW_BLK, HG_KW), lambda bi, g, blk=blk: (blk(bi, g), 0)))
    out = jax.ShapeDtypeStruct((n, HG_KW), F32)
    return pl.pallas_call(
        _hgrn_scan_kernel,
        grid=(batch, seq_blocks),
        in_specs=in_specs + [_const_spec((1, HG_KW))],
        out_specs=out_specs,
        out_shape=[out, out],
        scratch_shapes=[pltpu.VMEM((2, HG_HEADS, HG_DK, HG_DK), F32)],
        compiler_params=_cparams(("parallel", "arbitrary")),
    )(ub, ub, ub, ub, ub, ub, lb.reshape(1, HG_KW))


def _even_out_kernel(h_ref, mod_ref, yf_ref, yb_ref, bonus_ref, grw_ref, of_ref, ob_ref, ghg_ref, gng_ref, gnb_ref,
                     hgn_ref, gm64_ref, gm128_ref, w_ref, g_ref, b_ref, out_ref, acc_ref, *, nblk, alpha):
    y = yf_ref[...] + yb_ref[...]
    mu = _group_sum(y, gm64_ref[...])
    yc = y - mu
    var = _group_sum(yc * yc, gm64_ref[...])
    ra = (yc * lax.rsqrt(var + RW_GN_EPS) * gng_ref[...] + gnb_ref[...] + bonus_ref[...]) * grw_ref[...]
    o = of_ref[...] + ob_ref[...]
    ms = _group_sum(o * o, gm128_ref[...])
    hb = o * lax.rsqrt(ms + RMS_EPS) * hgn_ref[...] * _silu(ghg_ref[...])
    acc_ref[...] = _dot(ra.astype(BF16), w_ref[:RW_W, :]) + _dot(hb.astype(BF16), w_ref[RW_W:, :])
    _post_norm_rows(h_ref, out_ref, acc_ref, mod_ref, g_ref, b_ref, nblk, alpha, 1.0)


def _even_out(h, mod3, yf, yb, bonus, g_rw, of, ob, ub, gn_g, gn_b, hg_norm, gm64, gm128, w_out, ln_g, ln_b, alpha,
              tm=256):
    n, d = h.shape
    nblk = tm // ROW_BLK
    kern = functools.partial(_even_out_kernel, nblk=nblk, alpha=alpha)
    row = lambda i: (i, 0)
    vec = _const_spec((1, RW_W))
    return pl.pallas_call(
        kern,
        grid=(n // tm,),
        in_specs=[
            pl.BlockSpec((tm, d), row),
            pl.BlockSpec((nblk, 3, d), lambda i: (i, 0, 0)),
            pl.BlockSpec((tm, RW_W), row),
            pl.BlockSpec((tm, RW_W), row),
            pl.BlockSpec((tm, RW_W), row),
            pl.BlockSpec((tm, RW_W), row),
            pl.BlockSpec((tm, HG_KW), row),
            pl.BlockSpec((tm, HG_KW), row),
            pl.BlockSpec((tm, HG_KW), lambda i: (i, 4)),
            vec, vec, vec,
            _const_spec((RW_W, RW_W)),
            _const_spec((HG_KW, HG_KW)),
            _const_spec((d, d)),
            _const_spec((1, d)),
            _const_spec((1, d)),
        ],
        out_specs=pl.BlockSpec((tm, d), row),
        out_shape=jax.ShapeDtypeStruct((n, d), F32),
        scratch_shapes=[pltpu.VMEM((tm, d), F32)],
        compiler_params=_cparams(("parallel",)),
    )(h, mod3, yf, yb, bonus, g_rw, of, ob, ub, gn_g.reshape(1, RW_W), gn_b.reshape(1, RW_W), hg_norm, gm64, gm128,
      w_out.astype(BF16), ln_g.reshape(1, d), ln_b.reshape(1, d))


def _rope(t, cos, sin_up, sin_dn):
    return t * cos + pltpu.roll(t, LANES - 16, 1) * sin_up + pltpu.roll(t, 16, 1) * sin_dn


def _odd_proj_kernel(h_ref, mod_ref, w_ref, cos_ref, sup_ref, sdn_ref, q_ref, k_ref, v_ref, xm_ref, *, nblk):
    _modulate_rows(h_ref, mod_ref, xm_ref, nblk)
    xm = xm_ref[...]
    cos, sup, sdn = cos_ref[...], sup_ref[...], sdn_ref[...]
    q = _dot(xm, w_ref[:, :DA_W])
    k = _dot(xm, w_ref[:, DA_W:2 * DA_W])
    for h in range(DA_HEADS):
        sl = slice(h * LANES, (h + 1) * LANES)
        q_ref[:, sl] = (_rope(q[:, sl], cos, sup, sdn) * Q_SCALE).astype(q_ref.dtype)
        k_ref[:, sl] = _rope(k[:, sl], cos, sup, sdn).astype(k_ref.dtype)
    v_ref[...] = _dot(xm, w_ref[:, 2 * DA_W:]).astype(v_ref.dtype)


def _odd_proj(h, mod3, w, cos, sup, sdn, tm=512):
    n, d = h.shape
    nblk = tm // ROW_BLK
    kern = functools.partial(_odd_proj_kernel, nblk=nblk)
    row = lambda i: (i, 0)
    out = jax.ShapeDtypeStruct((n, DA_W), BF16)
    return pl.pallas_call(
        kern,
        grid=(n // tm,),
        in_specs=[
            pl.BlockSpec((tm, d), row),
            pl.BlockSpec((nblk, 3, d), lambda i: (i, 0, 0)),
            _const_spec((d, 3 * DA_W)),
            pl.BlockSpec((tm, LANES), row),
            pl.BlockSpec((tm, LANES), row),
            pl.BlockSpec((tm, LANES), row),
        ],
        out_specs=[pl.BlockSpec((tm, DA_W), row)] * 3,
        out_shape=[out, out, out],
        scratch_shapes=[pltpu.VMEM((tm, d), BF16)],
        compiler_params=_cparams(("parallel",)),
    )(h, mod3, w, cos, sup, sdn)


def _diff_attn_kernel(q_ref, k_ref, v_ref, lam_ref, ng_ref, o_ref, *, lam_init):
    lv = lam_ref[...]
    lam = (jnp.exp(jnp.sum(lv[0:1] * lv[1:2], axis=-1, keepdims=True))
           - jnp.exp(jnp.sum(lv[2:3] * lv[3:4], axis=-1, keepdims=True)) + lam_init)
    q = q_ref[...]
    k = k_ref[...]
    es, ls = [], []
    for m in range(2):
        sl = slice(m * DA_HEAD, (m + 1) * DA_HEAD)
        s = _dot_nt(q[:, sl], k[:, sl])
        e = jnp.exp2(s - jnp.max(s, axis=-1, keepdims=True))
        es.append(e)
        ls.append(jnp.sum(e, axis=-1, keepdims=True))
    a = es[0] - (lam * ls[0] / ls[1]) * es[1]
    o = _dot(a.astype(BF16), v_ref[...]) / ls[0]
    ms = jnp.mean(o * o, axis=-1, keepdims=True)
    o_ref[...] = (o * lax.rsqrt(ms + RMS_EPS) * ng_ref[...] * (1.0 - lam_init)).astype(o_ref.dtype)


def _diff_attn(q, k, v, lam_vecs, norm_g, lam_init, batch, seq_blocks, ctx_blocks, bq=ROW_BLK):
    s_len = seq_blocks * ROW_BLK
    lat_blocks = seq_blocks - ctx_blocks
    k3 = k.reshape(batch, s_len, DA_W)
    v3 = v.reshape(batch, s_len, DA_W)
    kern = functools.partial(_diff_attn_kernel, lam_init=lam_init)
    kv_spec = pl.BlockSpec((None, s_len, LANES), lambda b, h, i: (b, 0, h))
    return pl.pallas_call(
        kern,
        grid=(batch, DA_HEADS, lat_blocks),
        in_specs=[
            pl.BlockSpec((bq, LANES), lambda b, h, i: (b * seq_blocks + ctx_blocks + i, h)),
            kv_spec, kv_spec,
            _const_spec((4, DA_HEAD)),
            _const_spec((1, LANES)),
        ],
        out_specs=pl.BlockSpec((bq, LANES), lambda b, h, i: (b * lat_blocks + i, h)),
        out_shape=jax.ShapeDtypeStruct((batch * lat_blocks * ROW_BLK, DA_W), BF16),
        compiler_params=_cparams(("parallel", "parallel", "arbitrary")),
    )(q, k3, v3, lam_vecs, norm_g.reshape(1, LANES))


def _odd_out_kernel(h_ref, mod_ref, x_ref, w_ref, g_ref, b_ref, out_ref, acc_ref, *, nblk, alpha):
    acc_ref[...] = _dot(x_ref[...], w_ref[...])
    _post_norm_rows(h_ref, out_ref, acc_ref, mod_ref, g_ref, b_ref, nblk, alpha, 1.0)


def _odd_out(h, mod3, x, w_out, ln_g, ln_b, alpha, tm=512):
    n, d = h.shape
    nblk = tm // ROW_BLK
    kern = functools.partial(_odd_out_kernel, nblk=nblk, alpha=alpha)
    row = lambda i: (i, 0)
    return pl.pallas_call(
        kern,
        grid=(n // tm,),
        in_specs=[
            pl.BlockSpec((tm, d), row),
            pl.BlockSpec((nblk, 3, d), lambda i: (i, 0, 0)),
            pl.BlockSpec((tm, DA_W), row),
            _const_spec((DA_W, d)),
            _const_spec((1, d)),
            _const_spec((1, d)),
        ],
        out_specs=pl.BlockSpec((tm, d), row),
        out_shape=jax.ShapeDtypeStruct((n, d), F32),
        scratch_shapes=[pltpu.VMEM((tm, d), F32)],
        compiler_params=_cparams(("parallel",)),
    )(h, mod3, x, w_out.astype(BF16), ln_g.reshape(1, d), ln_b.reshape(1, d))


def _rope_tables(batch, ctx_len, t):
    n_rows = t // GRID_W
    rowp = np.repeat(np.arange(n_rows), GRID_W).astype(np.float32)
    colp = np.tile(np.arange(GRID_W), n_rows).astype(np.float32)
    n_freq = DA_HEAD // 4
    inv = jnp.asarray(ROPE_BASE, F32) ** (-jnp.arange(n_freq, dtype=F32) / n_freq)
    ar = jnp.asarray(rowp)[:, None] * inv
    ac = jnp.asarray(colp)[:, None] * inv
    ang = jnp.concatenate([ar, ar, ac, ac], axis=-1)
    cos = jnp.concatenate([jnp.ones((ctx_len, DA_HEAD), F32), jnp.cos(ang)], axis=0)
    sin = jnp.concatenate([jnp.zeros((ctx_len, DA_HEAD), F32), jnp.sin(ang)], axis=0)
    first = (np.arange(DA_HEAD) % 32) < 16
    sup = jnp.where(first, -sin, 0.0)
    sdn = jnp.where(first, 0.0, sin)
    rep = lambda a: jnp.tile(a, (batch, 2))
    return rep(cos), rep(sup), rep(sdn)


def _group_mean_matrix(width, group, scale=1.0):
    idx = np.arange(width) // group
    return jnp.asarray((idx[:, None] == idx[None, :]).astype(np.float32) * (scale / group)).astype(BF16)


def kernel(x, c, ctx, c_ctx, w_mod, b_mod, ln_g, ln_b, ffn_w_in, ffn_w_out, ev_w_in, ev_w_out,
           rw_mu, rw_w0, rw_w2, rw_a0, rw_a2, rw_g2, rw_k_k, rw_k_a, rw_r_k, rw_gn_g, rw_gn_b,
           hg_lb, hg_norm_g, od_w_in, od_w_out, da_lambda, da_norm_g):
    batch, t, d = x.shape
    ctx_len = ctx.shape[1]
    depth = w_mod.shape[0]
    assert t % ROW_BLK == 0 and ctx_len % ROW_BLK == 0 and t % GRID_W == 0
    alpha = (2 * depth) ** 0.25
    s_len = ctx_len + t
    seq_blocks = s_len // ROW_BLK
    ctx_blocks = ctx_len // ROW_BLK
    n = batch * s_len

    h = jnp.concatenate([ctx, x], axis=1).reshape(n, d)

    rows_pad = -(-(batch + 1) // SUBLANES) * SUBLANES
    cvec = jnp.zeros((rows_pad, d), F32).at[:batch].set(c).at[batch].set(c_ctx)
    m_all = _modulation(cvec, w_mod, b_mod).reshape(depth, rows_pad, N_SUB, 3, d)
    blk_row = np.array([batch if j < ctx_blocks else bi for bi in range(batch) for j in range(seq_blocks)])
    m_blk = m_all[:, blk_row]

    lower_bounds = jnp.cumsum(jax.nn.softmax(hg_lb.astype(F32), axis=0), axis=0)
    gsum64 = _group_mean_matrix(RW_W, RW_HEAD, scale=RW_HEAD)
    gm64 = _group_mean_matrix(RW_W, RW_HEAD)
    gm128 = _group_mean_matrix(HG_KW, HG_DK)

    for i in range(depth):
        j = i // 2
        with_ctx_out = i < depth - 1
        ffn = functools.partial(_ffn_sublayer, alpha=alpha)
        h = ffn(h, m_blk[i, :, 0], ffn_w_in[i, 0], ffn_w_out[i, 0], ln_g[i, 0], ln_b[i, 0])
        if i % 2 == 0:
            pad_a = jnp.zeros((d, A_PAD - A_TOTAL), F32)
            w_pad = jnp.concatenate([ev_w_in[j][:, :A_TOTAL], pad_a, ev_w_in[j][:, A_TOTAL:]], axis=1).astype(BF16)
            ua, ub = _even_proj(h, m_blk[i, :, 1], w_pad)
            mu_pad = jnp.concatenate([rw_mu[j], jnp.zeros((A_PAD - A_TOTAL,), F32)]).reshape(1, A_PAD)
            w_lora = jnp.zeros((RW_LORA_PAD, 4 * RW_W), F32)
            w_lora = w_lora.at[0:64, 0:RW_W].set(rw_w2[j, 0]).at[64:128, RW_W:2 * RW_W].set(rw_w2[j, 1])
            w_lora = w_lora.at[128:192, 2 * RW_W:3 * RW_W].set(rw_a2[j])
            w_lora = w_lora.at[192:192 + RW_GATE_LORA, 3 * RW_W:].set(rw_g2[j]).astype(BF16)
            r, k, v, z, b, lw, g_rw, bonus = _rwkv_prep(
                ua, mu_pad, w_lora, rw_w0[j], rw_a0[j], rw_k_k[j], rw_k_a[j], rw_r_k[j].reshape(RW_W), gsum64,
                seq_blocks, ctx_blocks)
            yf, yb = _rwkv_state(*_rwkv_chunks(r, k, v, z, b, lw), batch, ctx_blocks, seq_blocks)
            of, ob = _hgrn_scan(ub, lower_bounds[i], batch, ctx_blocks, seq_blocks)
            hg_norm = jnp.tile(hg_norm_g[j], HG_HEADS).reshape(1, HG_KW)
            h = _even_out(h, m_blk[i, :, 1], yf, yb, bonus, g_rw, of, ob, ub, rw_gn_g[j], rw_gn_b[j], hg_norm, gm64, gm128,
                          ev_w_out[j], ln_g[i, 1], ln_b[i, 1], alpha)
        else:
            lam_init = 0.8 - 0.6 * math.exp(-0.3 * i)
            cos, sup, sdn = _rope_tables(batch, ctx_len, t)
            q, k, v = _odd_proj(h, m_blk[i, :, 1], od_w_in[j].astype(BF16), cos, sup, sdn)
            att = _diff_attn(q, k, v, da_lambda[j], da_norm_g[j], lam_init, batch, seq_blocks, ctx_blocks)
            assert not with_ctx_out
            h = h.reshape(batch, s_len, d)[:, ctx_len:].reshape(batch * t, d)
            m_blk = m_blk[:, np.array([bi * seq_blocks + jb for bi in range(batch)
                                       for jb in range(ctx_blocks, seq_blocks)])]
            h = _odd_out(h, m_blk[i, :, 1], att, od_w_out[j], ln_g[i, 1], ln_b[i, 1], alpha)
        h = ffn(h, m_blk[i, :, 2], ffn_w_in[i, 1], ffn_w_out[i, 1], ln_g[i, 2], ln_b[i, 2])
    if h.shape[0] == n:
        h = h.reshape(batch, s_len, d)[:, ctx_len:]
    return h.reshape(batch, t, d)
```

```python
import functools
import math

import numpy as np
import jax
import jax.numpy as jnp
from jax import lax
from jax.experimental import pallas as pl
from jax.experimental.pallas import tpu as pltpu

F32 = jnp.float32
BF16 = jnp.bfloat16
HIGHEST = lax.Precision.HIGHEST

N_SUB = 3
LN_EPS = 1e-5
RMS_EPS = 1e-5
RW_HEADS = 8
RW_HEAD = 64
RW_W = RW_HEADS * RW_HEAD
RW_DECAY_LORA = 64
RW_A_LORA = 64
RW_GATE_LORA = 160
RW_GN_EPS = 64e-5
RW_MAIN = 3 * RW_W
RW_LORA = 2 * RW_DECAY_LORA + RW_A_LORA + RW_GATE_LORA
RW_LORA_PAD = 384
A_TOTAL = RW_MAIN + RW_LORA
A_PAD = RW_MAIN + RW_LORA_PAD
HG_HEADS = 4
HG_DK = 128
HG_KW = HG_HEADS * HG_DK
B_TOTAL = 5 * HG_KW
DA_HEADS = 8
DA_HEAD = 64
DA_W = DA_HEADS * 2 * DA_HEAD
GRID_W = 64
ROPE_BASE = 10000.0
Q_SCALE = DA_HEAD ** -0.5 * math.log2(math.e)

LANES = 128
SUBLANES = 8
ROW_BLK = 256
CHUNK = 64
ATT_SUB = 256
VMEM_LIMIT = 56 * 1024 * 1024


def _cparams(sem):
    return pltpu.CompilerParams(dimension_semantics=sem, vmem_limit_bytes=VMEM_LIMIT)


def _const_spec(shape):
    nd = len(shape)
    return pl.BlockSpec(shape, lambda *_: (0,) * nd, pipeline_mode=pl.Buffered(1))


def _dot(a, b):
    return jnp.dot(a, b, preferred_element_type=F32)


def _dot_nt(a, b):
    return lax.dot_general(a, b, (((1,), (1,)), ((), ())), preferred_element_type=F32)


def _dot_tn(a, b):
    return lax.dot_general(a, b, (((0,), (0,)), ((), ())), preferred_element_type=F32)


def _dot_hi(a, b):
    return jnp.dot(a, b, preferred_element_type=F32, precision=HIGHEST)


def _split(x, pieces):
    out = []
    for _ in range(pieces):
        p = x.astype(BF16)
        out.append(p)
        x = x - p.astype(F32)
    return out


def _group_sum(x, gmat):
    return sum(_dot(p, gmat) for p in _split(x, 2))


def _dot_split(a, b):
    a_hi, a_lo = _split(a, 2)
    b_hi, b_lo = _split(b, 2)
    return _dot(a_hi, b_hi) + _dot(a_hi, b_lo) + _dot(a_lo, b_hi)


def _cumsum_rows(tri, x):
    return sum(_dot(tri, p) for p in _split(x, 3))


def _silu(x):
    return x * jax.nn.sigmoid(x)


def _layernorm(y, g, b):
    mu = jnp.mean(y, axis=-1, keepdims=True)
    yc = y - mu
    var = jnp.mean(yc * yc, axis=-1, keepdims=True)
    return yc * lax.rsqrt(var + LN_EPS) * g + b


def _mod_kernel(c_ref, w_ref, b_ref, o_ref):
    o_ref[...] = _dot_hi(_silu(c_ref[...]), w_ref[...]) + b_ref[...]


def _modulation(cvec, w_mod, b_mod):
    depth, d, width = w_mod.shape
    rows = cvec.shape[0]
    tn = 1024
    return pl.pallas_call(
        _mod_kernel,
        grid=(depth, width // tn),
        in_specs=[
            pl.BlockSpec((rows, d), lambda l, j: (0, 0)),
            pl.BlockSpec((None, d, tn), lambda l, j: (l, 0, j)),
            pl.BlockSpec((None, 1, tn), lambda l, j: (l, 0, j)),
        ],
        out_specs=pl.BlockSpec((None, rows, tn), lambda l, j: (l, 0, j)),
        out_shape=jax.ShapeDtypeStruct((depth, rows, width), F32),
        compiler_params=_cparams(("parallel", "parallel")),
    )(cvec, w_mod, b_mod.reshape(depth, 1, width))


def _modulate_rows(h_ref, mod_ref, xm_ref, nblk):
    for s in range(nblk):
        rows = pl.ds(s * ROW_BLK, ROW_BLK)
        x = h_ref[rows, :]
        xm_ref[rows, :] = (x * (1.0 + mod_ref[s, 1:2, :]) + mod_ref[s, 0:1, :]).astype(xm_ref.dtype)


def _post_norm_rows(h_ref, out_ref, acc_ref, mod_ref, g_ref, b_ref, nblk, alpha, weight):
    for s in range(nblk):
        rows = pl.ds(s * ROW_BLK, ROW_BLK)
        y = alpha * h_ref[rows, :] + (weight * mod_ref[s, 2:3, :]) * acc_ref[rows, :]
        out_ref[rows, :] = _layernorm(y, g_ref[...], b_ref[...])


def _ffn_kernel(h_ref, mod_ref, wg_ref, wu_ref, wo_ref, g_ref, b_ref, out_ref, xm_ref, acc_ref,
                *, nblk, nf, alpha):
    _modulate_rows(h_ref, mod_ref, xm_ref, nblk)
    acc_ref[...] = jnp.zeros_like(acc_ref)

    def body(fc, carry):
        xm = xm_ref[...]
        gt = _dot(xm, wg_ref[fc])
        up = _dot(xm, wu_ref[fc])
        act = (_silu(gt) * up).astype(BF16)
        acc_ref[...] += _dot(act, wo_ref[fc])
        return carry

    lax.fori_loop(0, nf, body, 0, unroll=True)
    _post_norm_rows(h_ref, out_ref, acc_ref, mod_ref, g_ref, b_ref, nblk, alpha, 0.5)


def _ffn_sublayer(h, mod3, w_in, w_out, ln_g, ln_b, alpha, tm=512, tf=256):
    n, d = h.shape
    f = w_out.shape[0]
    nf = f // tf
    nblk = tm // ROW_BLK
    wg = w_in[:, :f].reshape(d, nf, tf).transpose(1, 0, 2).astype(BF16)
    wu = w_in[:, f:].reshape(d, nf, tf).transpose(1, 0, 2).astype(BF16)
    wo = w_out.reshape(nf, tf, d).astype(BF16)
    kern = functools.partial(_ffn_kernel, nblk=nblk, nf=nf, alpha=alpha)
    return pl.pallas_call(
        kern,
        grid=(n // tm,),
        in_specs=[
            pl.BlockSpec((tm, d), lambda i: (i, 0)),
            pl.BlockSpec((nblk, 3, d), lambda i: (i, 0, 0)),
            _const_spec((nf, d, tf)),
            _const_spec((nf, d, tf)),
            _const_spec((nf, tf, d)),
            _const_spec((1, d)),
            _const_spec((1, d)),
        ],
        out_specs=pl.BlockSpec((tm, d), lambda i: (i, 0)),
        out_shape=jax.ShapeDtypeStruct((n, d), F32),
        scratch_shapes=[pltpu.VMEM((tm, d), BF16), pltpu.VMEM((tm, d), F32)],
        compiler_params=_cparams(("parallel",)),
    )(h, mod3, wg, wu, wo, ln_g.reshape(1, d), ln_b.reshape(1, d))


def _even_proj_kernel(h_ref, mod_ref, w_ref, ua_ref, ub_ref, xm_ref, *, nblk):
    _modulate_rows(h_ref, mod_ref, xm_ref, nblk)
    xm = xm_ref[...]
    ua_ref[...] = _dot(xm, w_ref[:, :A_PAD])
    ub_ref[...] = _dot(xm, w_ref[:, A_PAD:])


def _even_proj(h, mod3, w_pad, tm=512):
    n, d = h.shape
    nblk = tm // ROW_BLK
    kern = functools.partial(_even_proj_kernel, nblk=nblk)
    return pl.pallas_call(
        kern,
        grid=(n // tm,),
        in_specs=[
            pl.BlockSpec((tm, d), lambda i: (i, 0)),
            pl.BlockSpec((nblk, 3, d), lambda i: (i, 0, 0)),
            _const_spec((d, A_PAD + B_TOTAL)),
        ],
        out_specs=[pl.BlockSpec((tm, A_PAD), lambda i: (i, 0)),
                   pl.BlockSpec((tm, B_TOTAL), lambda i: (i, 0))],
        out_shape=[jax.ShapeDtypeStruct((n, A_PAD), F32), jax.ShapeDtypeStruct((n, B_TOTAL), F32)],
        scratch_shapes=[pltpu.VMEM((tm, d), BF16)],
        compiler_params=_cparams(("parallel",)),
    )(h, mod3, w_pad)


def _rwkv_prep_kernel(u_ref, prev_ref, next_ref, mu_ref, wl_ref, w0_ref, a0_ref, kk_ref, ka_ref, rk_ref,
                      gsum_ref, r_out, k_out, v_out, z_out, b_out, lw_out, g_out, bonus_out,
                      *, blocks_per_seq, ctx_blocks):
    i = pl.program_id(0)
    j = i % blocks_per_seq
    at_start = jnp.logical_or(j == 0, j == ctx_blocks)
    at_end = jnp.logical_or(j == ctx_blocks - 1, j == blocks_per_seq - 1)
    u = u_ref[...]
    prev_row = jnp.where(at_start, 0.0, prev_ref[SUBLANES - 1:SUBLANES, :])
    next_row = jnp.where(at_end, 0.0, next_ref[0:1, :])
    rid = lax.broadcasted_iota(jnp.int32, u.shape, 0)
    up = jnp.where(rid == 0, prev_row, pltpu.roll(u, 1, 0))
    dn = jnp.where(rid == ROW_BLK - 1, next_row, pltpu.roll(u, ROW_BLK - 1, 0))
    u = u + mu_ref[...] * (0.5 * (up + dn) - u)

    r = u[:, 0:RW_W]
    k = u[:, RW_W:2 * RW_W]
    v = u[:, 2 * RW_W:3 * RW_W]
    lo = u[:, RW_MAIN:A_PAD]
    lane = lax.broadcasted_iota(jnp.int32, lo.shape, 1)
    lo = jnp.where(lane < 2 * RW_DECAY_LORA, jnp.tanh(lo),
                   jnp.where(lane < 2 * RW_DECAY_LORA + RW_A_LORA, lo, jax.nn.sigmoid(lo)))
    lora = _dot(lo.astype(BF16), wl_ref[...])
    a = jax.nn.sigmoid(a0_ref[...] + lora[:, 2 * RW_W:3 * RW_W])
    g_out[...] = lora[:, 3 * RW_W:4 * RW_W]
    for dd in range(2):
        x = -(w0_ref[dd:dd + 1, :] + lora[:, dd * RW_W:(dd + 1) * RW_W])
        softplus = jnp.maximum(x, 0.0) + jnp.log(1.0 + jnp.exp(-jnp.abs(x)))
        lw_out[dd] = -jnp.exp(-softplus - 0.5)
    kk = k * kk_ref[...]
    ss = _group_sum(kk * kk, gsum_ref[...])
    kk = kk / jnp.maximum(jnp.sqrt(ss), 1e-12)
    k = k * (1.0 + (a - 1.0) * ka_ref[...])
    r_out[...] = r
    k_out[...] = k
    v_out[...] = v
    z_out[...] = -kk
    b_out[...] = kk * a
    bonus_out[...] = _group_sum(r * k * rk_ref[...], gsum_ref[...]) * v


def _rwkv_prep(ua, mu_pad, w_lora, w0, a0, k_k, k_a, r_k, gsum, blocks_per_seq, ctx_blocks):
    n = ua.shape[0]
    nb = n // ROW_BLK
    halo = ROW_BLK // SUBLANES
    last8 = n // SUBLANES - 1
    kern = functools.partial(_rwkv_prep_kernel, blocks_per_seq=blocks_per_seq, ctx_blocks=ctx_blocks)
    row = lambda i: (i, 0)
    vec = _const_spec((1, RW_W))
    out512 = jax.ShapeDtypeStruct((n, RW_W), F32)
    return pl.pallas_call(
        kern,
        grid=(nb,),
        in_specs=[
            pl.BlockSpec((ROW_BLK, A_PAD), row),
            pl.BlockSpec((SUBLANES, A_PAD), lambda i: (jnp.maximum(i * halo - 1, 0), 0)),
            pl.BlockSpec((SUBLANES, A_PAD), lambda i: (jnp.minimum((i + 1) * halo, last8), 0)),
            _const_spec((1, A_PAD)),
            _const_spec((RW_LORA_PAD, 4 * RW_W)),
            _const_spec((2, RW_W)),
            vec, vec, vec, vec,
            _const_spec((RW_W, RW_W)),
        ],
        out_specs=[pl.BlockSpec((ROW_BLK, RW_W), row)] * 5
        + [pl.BlockSpec((2, ROW_BLK, RW_W), lambda i: (0, i, 0))]
        + [pl.BlockSpec((ROW_BLK, RW_W), row)] * 2,
        out_shape=[out512] * 5 + [jax.ShapeDtypeStruct((2, n, RW_W), F32)] + [out512] * 2,
        compiler_params=_cparams(("parallel",)),
    )(ua, ua, ua, mu_pad, w_lora, w0, a0.reshape(1, RW_W), k_k.reshape(1, RW_W), k_a.reshape(1, RW_W),
      r_k.reshape(1, RW_W), gsum)


CPB = ROW_BLK // CHUNK


def _block_index(d, g, ctx_blocks, seq_blocks):
    if d == 0:
        return g
    return jnp.where(g < ctx_blocks, ctx_blocks - 1 - g, seq_blocks + ctx_blocks - 1 - g)


def _chunk_order(d):
    return range(CPB) if d == 0 else range(CPB - 1, -1, -1)


def _incl_mask(d):
    row = lax.broadcasted_iota(jnp.int32, (CHUNK, CHUNK), 0)
    col = lax.broadcasted_iota(jnp.int32, (CHUNK, CHUNK), 1)
    return (row - col) * (1 - 2 * d) >= 0


def _rwkv_chunk_kernel(r_ref, k_ref, v_ref, z_ref, b_ref, lw_ref, rp_ref, y0_ref, mx_ref, s0_ref, pt_ref):
    zeros = jnp.zeros((CHUNK, RW_HEAD), BF16)
    row = lax.broadcasted_iota(jnp.int32, (2 * CHUNK, 2 * CHUNK), 0)
    col = lax.broadcasted_iota(jnp.int32, (2 * CHUNK, 2 * CHUNK), 1)
    chains = []
    for cc in range(r_ref.shape[0] // CHUNK):
        rows = slice(cc * CHUNK, (cc + 1) * CHUNK)
        r = r_ref[rows, :]
        k = k_ref[rows, :]
        z = z_ref[rows, :]
        b = b_ref[rows, :]
        vb = v_ref[rows, :].astype(BF16)
        for d in range(2):
            order = ((row & (CHUNK - 1)) - (col & (CHUNK - 1))) * (1 - 2 * d)
            mask = order + (row >> 6) > 0
            lw = lw_ref[d, rows, :]
            cs = _cumsum_rows(_incl_mask(d).astype(BF16), lw)
            tot = jnp.sum(lw, axis=0, keepdims=True)
            p_inc = jnp.exp(cs)
            p_inv = jnp.exp(-cs)
            p_end = jnp.exp(tot - cs)
            zt = z * jnp.exp(cs - lw)
            rt = r * p_inc
            zr = jnp.concatenate([zt, rt], axis=0).astype(BF16)
            bk = jnp.concatenate([b * p_inv, k * p_inv], axis=0).astype(BF16)
            be = (b * p_end).astype(BF16)
            ke = (k * p_end).astype(BF16)
            pt_ref[d, cc] = jnp.exp(tot)
            for h in range(RW_HEADS):
                sl = slice(h * RW_HEAD, (h + 1) * RW_HEAD)
                g = jnp.where(mask, _dot_nt(zr[:, sl], bk[:, sl]), 0.0)
                chains.append(dict(d=d, rows=rows, sl=sl, vh=vb[:, sl], zt=zt[:, sl], rt=rt[:, sl],
                                   be=be[:, sl], ke=ke[:, sl], gtop=g[:CHUNK].astype(BF16),
                                   gbot=g[CHUNK:].astype(BF16), p=g[:CHUNK, :CHUNK]))
    for c in chains:
        akv = _dot(c["gtop"], jnp.concatenate([zeros, c["vh"]], axis=0))
        c["x"] = jnp.concatenate([c["zt"], akv], axis=1)
    for it in range(6):
        for c in chains:
            c["pb"] = c["p"].astype(BF16)
            c["x"] = c["x"] + _dot(c["pb"], c["x"].astype(BF16))
        if it < 5:
            for c in chains:
                c["p"] = _dot(c["pb"], c["pb"])
    for c in chains:
        c["xb"] = c["x"].astype(BF16)
        rhs = jnp.concatenate([c["xb"], jnp.concatenate([zeros, c["vh"]], axis=1)], axis=0)
        c["ry"] = _dot(c["gbot"], rhs)
    for c in chains:
        c["ms"] = _dot_tn(c["xb"], c["be"])
        c["vk"] = _dot_tn(c["vh"], c["ke"])
    for c in chains:
        d, rows, sl = c["d"], c["rows"], c["sl"]
        rp_ref[d, rows, sl] = (c["rt"] + c["ry"][:, :RW_HEAD]).astype(rp_ref.dtype)
        y0_ref[d, rows, sl] = c["ry"][:, RW_HEAD:]
        mx_ref[d, rows, sl] = c["ms"][:RW_HEAD]
        s0_ref[d, rows, sl] = c["ms"][RW_HEAD:] + c["vk"]


def _rwkv_chunks(r, k, v, z, b, lw, chunks_per_step=4):
    n = r.shape[0]
    nc = n // CHUNK
    tm = chunks_per_step * CHUNK
    row = lambda i: (i, 0)
    drow = lambda i: (0, i, 0)
    spec = pl.BlockSpec((tm, RW_W), row)
    dspec = pl.BlockSpec((2, tm, RW_W), drow)
    big = jax.ShapeDtypeStruct((2, n, RW_W), F32)
    return pl.pallas_call(
        _rwkv_chunk_kernel,
        grid=(n // tm,),
        in_specs=[spec] * 5 + [dspec],
        out_specs=[dspec] * 4 + [pl.BlockSpec((2, chunks_per_step, 1, RW_W), lambda i: (0, i, 0, 0))],
        out_shape=[jax.ShapeDtypeStruct((2, n, RW_W), BF16), big, big, big,
                   jax.ShapeDtypeStruct((2, nc, 1, RW_W), F32)],
        compiler_params=_cparams(("parallel",)),
    )(r, k, v, z, b, lw)


def _rwkv_state_kernel(rp0, y00, mx0, s00, pt0, rp1, y01, mx1, s01, pt1, yf_ref, yb_ref, s_ref):
    @pl.when(pl.program_id(1) == 0)
    def _():
        s_ref[...] = jnp.zeros_like(s_ref)

    dirs = ((rp0, y00, mx0, s00, pt0, yf_ref), (rp1, y01, mx1, s01, pt1, yb_ref))
    heads = [(d, h) for d in range(2) for h in range(RW_HEADS)]
    state = {dh: s_ref[dh[0], dh[1]] for dh in heads}
    for step in range(CPB):
        for d, h in heads:
            rp, y0, mx, s0, pt, y_ref = dirs[d]
            cc = _chunk_order(d)[step]
            rows = slice(cc * CHUNK, (cc + 1) * CHUNK)
            sl = slice(h * RW_HEAD, (h + 1) * RW_HEAD)
            s = state[d, h]
            y_ref[rows, sl] = _dot_nt(rp[rows, sl], s.astype(BF16)) + y0[rows, sl]
            state[d, h] = s * pt[cc, :, sl] + _dot_split(s, mx[rows, sl]) + s0[rows, sl]
    for d, h in heads:
        s_ref[d, h] = state[d, h]


def _rwkv_state(rp, y0, mx, s0, pt, batch, ctx_blocks, seq_blocks):
    n = y0.shape[1]
    in_specs = []
    for d in range(2):
        blk = lambda bi, g, d=d: bi * seq_blocks + _block_index(d, g, ctx_blocks, seq_blocks)
        big = pl.BlockSpec((None, ROW_BLK, RW_W), lambda bi, g, d=d, blk=blk: (d, blk(bi, g), 0))
        in_specs += [big] * 4 + [pl.BlockSpec((None, CPB, 1, RW_W), lambda bi, g, d=d, blk=blk: (d, blk(bi, g), 0, 0))]
    out_specs = [pl.BlockSpec((ROW_BLK, RW_W),
                              lambda bi, g, d=d: (bi * seq_blocks + _block_index(d, g, ctx_blocks, seq_blocks), 0))
                 for d in range(2)]
    out = jax.ShapeDtypeStruct((n, RW_W), F32)
    return pl.pallas_call(
        _rwkv_state_kernel,
        grid=(batch, seq_blocks),
        in_specs=in_specs,
        out_specs=out_specs,
        out_shape=[out, out],
        scratch_shapes=[pltpu.VMEM((2, RW_HEADS, RW_HEAD, RW_HEAD), F32)],
        compiler_params=_cparams(("parallel", "arbitrary")),
    )(rp, y0, mx, s0, pt, rp, y0, mx, s0, pt)


def _hgrn_scan_kernel(q0_ref, f0_ref, i0_ref, q1_ref, f1_ref, i1_ref, lb_ref, of_ref, ob_ref, s_ref):
    @pl.when(pl.program_id(1) == 0)
    def _():
        s_ref[...] = jnp.zeros_like(s_ref)

    lb = lb_ref[...]
    dirs = ((q0_ref, f0_ref, i0_ref, of_ref), (q1_ref, f1_ref, i1_ref, ob_ref))
    groups = []
    for d, (q_ref, f_ref, i_ref, o_ref) in enumerate(dirs):
        for cc in _chunk_order(d):
            rows = slice(cc * CHUNK, (cc + 1) * CHUNK)
            fg = lb + (1.0 - lb) * jax.nn.sigmoid(f_ref[rows, :])
            groups.append(dict(d=d, rows=rows, o_ref=o_ref, q=_silu(q_ref[rows, :]), logf=jnp.log(fg),
                               k=1.0 - fg, vb=i_ref[rows, :].astype(BF16)))
    for g in groups:
        g["cs"] = _cumsum_rows(_incl_mask(g["d"]).astype(BF16), g["logf"])
    for g in groups:
        cs = g["cs"]
        tot = jnp.sum(g["logf"], axis=0, keepdims=True)
        g["qb"] = (g["q"] * jnp.exp(cs)).astype(BF16)
        g["kb"] = (g["k"] * jnp.exp(-cs)).astype(BF16)
        g["kd"] = (g["k"] * jnp.exp(tot - cs)).astype(BF16)
        g["gl"] = jnp.exp(tot)
    heads = [slice(h * HG_DK, (h + 1) * HG_DK) for h in range(HG_HEADS)]
    for g in groups:
        incl = _incl_mask(g["d"])
        g["sc"] = [jnp.where(incl, _dot_nt(g["qb"][:, sl], g["kb"][:, sl]), 0.0).astype(BF16) for sl in heads]
    for g in groups:
        g["intra"] = [_dot(sc, g["vb"][:, sl]) for sc, sl in zip(g["sc"], heads)]
        g["kv"] = [_dot_tn(g["vb"][:, sl], g["kd"][:, sl]) for sl in heads]
    state = {(d, h): s_ref[d, h] for d in range(2) for h in range(HG_HEADS)}
    for g in groups:
        d = g["d"]
        for h, sl in enumerate(heads):
            s = state[d, h]
            g["o_ref"][g["rows"], sl] = g["intra"][h] + _dot_nt(g["qb"][:, sl], s.astype(BF16))
            state[d, h] = s * g["gl"][:, sl] + g["kv"][h]
    for (d, h), s in state.items():
        s_ref[d, h] = s


def _hgrn_scan(ub, lb, batch, ctx_blocks, seq_blocks):
    n = ub.shape[0]
    in_specs = []
    out_specs = []
    for d in range(2):
        blk = lambda bi, g, d=d: bi * seq_blocks + _block_index(d, g, ctx_blocks, seq_blocks)
        in_specs += [
            pl.BlockSpec((ROW_BLK, HG_KW), lambda bi, g, blk=blk: (blk(bi, g), 0)),
            pl.BlockSpec((ROW_BLK, HG_KW), lambda bi, g, blk=blk, d=d: (blk(bi, g), 1 + d)),
            pl.BlockSpec((ROW_BLK, HG_KW), lambda bi, g, blk=blk: (blk(bi, g), 3)),
        ]
        out_specs.append(pl.BlockSpec((ROW_BLK, HG_KW), lambda bi, g, blk=blk: (blk(bi, g), 0)))
    out = jax.ShapeDtypeStruct((n, HG_KW), F32)
    return pl.pallas_call(
        _hgrn_scan_kernel,
        grid=(batch, seq_blocks),
        in_specs=in_specs + [_const_spec((1, HG_KW))],
        out_specs=out_specs,
        out_shape=[out, out],
        scratch_shapes=[pltpu.VMEM((2, HG_HEADS, HG_DK, HG_DK), F32)],
        compiler_params=_cparams(("parallel", "arbitrary")),
    )(ub, ub, ub, ub, ub, ub, lb.reshape(1, HG_KW))


def _even_out_kernel(h_ref, mod_ref, yf_ref, yb_ref, bonus_ref, grw_ref, of_ref, ob_ref, ghg_ref, gng_ref, gnb_ref,
                     hgn_ref, gm64_ref, gm128_ref, w_ref, g_ref, b_ref, out_ref, acc_ref, *, nblk, alpha):
    y = yf_ref[...] + yb_ref[...]
    mu = _group_sum(y, gm64_ref[...])
    yc = y - mu
    var = _group_sum(yc * yc, gm64_ref[...])
    ra = (yc * lax.rsqrt(var + RW_GN_EPS) * gng_ref[...] + gnb_ref[...] + bonus_ref[...]) * grw_ref[...]
    o = of_ref[...] + ob_ref[...]
    ms = _group_sum(o * o, gm128_ref[...])
    hb = o * lax.rsqrt(ms + RMS_EPS) * hgn_ref[...] * _silu(ghg_ref[...])
    acc_ref[...] = _dot(ra.astype(BF16), w_ref[:RW_W, :]) + _dot(hb.astype(BF16), w_ref[RW_W:, :])
    _post_norm_rows(h_ref, out_ref, acc_ref, mod_ref, g_ref, b_ref, nblk, alpha, 1.0)


def _even_out(h, mod3, yf, yb, bonus, g_rw, of, ob, ub, gn_g, gn_b, hg_norm, gm64, gm128, w_out, ln_g, ln_b, alpha,
              tm=256):
    n, d = h.shape
    nblk = tm // ROW_BLK
    kern = functools.partial(_even_out_kernel, nblk=nblk, alpha=alpha)
    row = lambda i: (i, 0)
    vec = _const_spec((1, RW_W))
    return pl.pallas_call(
        kern,
        grid=(n // tm,),
        in_specs=[
            pl.BlockSpec((tm, d), row),
            pl.BlockSpec((nblk, 3, d), lambda i: (i, 0, 0)),
            pl.BlockSpec((tm, RW_W), row),
            pl.BlockSpec((tm, RW_W), row),
            pl.BlockSpec((tm, RW_W), row),
            pl.BlockSpec((tm, RW_W), row),
            pl.BlockSpec((tm, HG_KW), row),
            pl.BlockSpec((tm, HG_KW), row),
            pl.BlockSpec((tm, HG_KW), lambda i: (i, 4)),
            vec, vec, vec,
            _const_spec((RW_W, RW_W)),
            _const_spec((HG_KW, HG_KW)),
            _const_spec((d, d)),
            _const_spec((1, d)),
            _const_spec((1, d)),
        ],
        out_specs=pl.BlockSpec((tm, d), row),
        out_shape=jax.ShapeDtypeStruct((n, d), F32),
        scratch_shapes=[pltpu.VMEM((tm, d), F32)],
        compiler_params=_cparams(("parallel",)),
    )(h, mod3, yf, yb, bonus, g_rw, of, ob, ub, gn_g.reshape(1, RW_W), gn_b.reshape(1, RW_W), hg_norm, gm64, gm128,
      w_out.astype(BF16), ln_g.reshape(1, d), ln_b.reshape(1, d))


def _rope(t, cos, sin_up, sin_dn):
    return t * cos + pltpu.roll(t, LANES - 16, 1) * sin_up + pltpu.roll(t, 16, 1) * sin_dn


def _odd_proj_kernel(h_ref, mod_ref, w_ref, cos_ref, sup_ref, sdn_ref, q_ref, k_ref, v_ref, xm_ref, *, nblk):
    _modulate_rows(h_ref, mod_ref, xm_ref, nblk)
    xm = xm_ref[...]
    cos, sup, sdn = cos_ref[...], sup_ref[...], sdn_ref[...]
    q = _dot(xm, w_ref[:, :DA_W])
    k = _dot(xm, w_ref[:, DA_W:2 * DA_W])
    for h in range(DA_HEADS):
        sl = slice(h * LANES, (h + 1) * LANES)
        q_ref[:, sl] = (_rope(q[:, sl], cos, sup, sdn) * Q_SCALE).astype(q_ref.dtype)
        k_ref[:, sl] = _rope(k[:, sl], cos, sup, sdn).astype(k_ref.dtype)
    v_ref[...] = _dot(xm, w_ref[:, 2 * DA_W:]).astype(v_ref.dtype)


def _odd_proj(h, mod3, w, cos, sup, sdn, tm=512):
    n, d = h.shape
    nblk = tm // ROW_BLK
    kern = functools.partial(_odd_proj_kernel, nblk=nblk)
    row = lambda i: (i, 0)
    out = jax.ShapeDtypeStruct((n, DA_W), BF16)
    return pl.pallas_call(
        kern,
        grid=(n // tm,),
        in_specs=[
            pl.BlockSpec((tm, d), row),
            pl.BlockSpec((nblk, 3, d), lambda i: (i, 0, 0)),
            _const_spec((d, 3 * DA_W)),
            pl.BlockSpec((tm, LANES), row),
            pl.BlockSpec((tm, LANES), row),
            pl.BlockSpec((tm, LANES), row),
        ],
        out_specs=[pl.BlockSpec((tm, DA_W), row)] * 3,
        out_shape=[out, out, out],
        scratch_shapes=[pltpu.VMEM((tm, d), BF16)],
        compiler_params=_cparams(("parallel",)),
    )(h, mod3, w, cos, sup, sdn)


def _diff_attn_kernel(q_ref, k_ref, v_ref, lam_ref, ng_ref, o_ref, *, lam_init, q0, bq):
    base = q0 + pl.program_id(2) * bq
    lv = lam_ref[...]
    lam = (jnp.exp(jnp.sum(lv[0:1] * lv[1:2], axis=-1, keepdims=True))
           - jnp.exp(jnp.sum(lv[2:3] * lv[3:4], axis=-1, keepdims=True)) + lam_init)
    k = k_ref[...]
    halves = [slice(i * ATT_SUB, (i + 1) * ATT_SUB) for i in range(bq // ATT_SUB)]

    def scores(rows):
        q = q_ref[pl.ds(pl.multiple_of(base + rows.start, ATT_SUB), ATT_SUB), :]
        return [_dot_nt(q[:, m * DA_HEAD:(m + 1) * DA_HEAD], k[:, m * DA_HEAD:(m + 1) * DA_HEAD])
                for m in range(2)]

    def readout(rows, s):
        es, ls = [], []
        for m in range(2):
            e = jnp.exp2(s[m] - jnp.max(s[m], axis=-1, keepdims=True))
            es.append(e)
            ls.append(jnp.sum(e, axis=-1, keepdims=True))
        a = es[0] - (lam * ls[0] / ls[1]) * es[1]
        o = _dot(a.astype(BF16), v_ref[...]) / ls[0]
        ms = jnp.mean(o * o, axis=-1, keepdims=True)
        o_ref[rows, :] = (o * lax.rsqrt(ms + RMS_EPS) * ng_ref[...] * (1.0 - lam_init)).astype(o_ref.dtype)

    s_next = scores(halves[0])
    for i, rows in enumerate(halves):
        s_cur = s_next
        if i + 1 < len(halves):
            s_next = scores(halves[i + 1])
        readout(rows, s_cur)


def _diff_attn(q, k, v, lam_vecs, norm_g, lam_init, batch, seq_blocks, ctx_blocks, bq=2 * ATT_SUB):
    s_len = seq_blocks * ROW_BLK
    t_len = (seq_blocks - ctx_blocks) * ROW_BLK
    assert t_len % bq == 0
    nt = t_len // bq
    kern = functools.partial(_diff_attn_kernel, lam_init=lam_init, q0=ctx_blocks * ROW_BLK, bq=bq)
    seq_spec = pl.BlockSpec((None, s_len, LANES), lambda b, h, i: (b, 0, h))
    return pl.pallas_call(
        kern,
        grid=(batch, DA_HEADS, nt),
        in_specs=[seq_spec, seq_spec, seq_spec, _const_spec((4, DA_HEAD)), _const_spec((1, LANES))],
        out_specs=pl.BlockSpec((bq, LANES), lambda b, h, i: (b * nt + i, h)),
        out_shape=jax.ShapeDtypeStruct((batch * t_len, DA_W), BF16),
        compiler_params=_cparams(("parallel", "parallel", "arbitrary")),
    )(q.reshape(batch, s_len, DA_W), k.reshape(batch, s_len, DA_W), v.reshape(batch, s_len, DA_W),
      lam_vecs, norm_g.reshape(1, LANES))


def _odd_out_kernel(h_ref, mod_ref, x_ref, w_ref, g_ref, b_ref, out_ref, acc_ref, *, nblk, alpha):
    acc_ref[...] = _dot(x_ref[...], w_ref[...])
    _post_norm_rows(h_ref, out_ref, acc_ref, mod_ref, g_ref, b_ref, nblk, alpha, 1.0)


def _odd_out(h, mod3, x, w_out, ln_g, ln_b, alpha, tm=512):
    n, d = h.shape
    nblk = tm // ROW_BLK
    kern = functools.partial(_odd_out_kernel, nblk=nblk, alpha=alpha)
    row = lambda i: (i, 0)
    return pl.pallas_call(
        kern,
        grid=(n // tm,),
        in_specs=[
            pl.BlockSpec((tm, d), row),
            pl.BlockSpec((nblk, 3, d), lambda i: (i, 0, 0)),
            pl.BlockSpec((tm, DA_W), row),
            _const_spec((DA_W, d)),
            _const_spec((1, d)),
            _const_spec((1, d)),
        ],
        out_specs=pl.BlockSpec((tm, d), row),
        out_shape=jax.ShapeDtypeStruct((n, d), F32),
        scratch_shapes=[pltpu.VMEM((tm, d), F32)],
        compiler_params=_cparams(("parallel",)),
    )(h, mod3, x, w_out.astype(BF16), ln_g.reshape(1, d), ln_b.reshape(1, d))


def _rope_tables(batch, ctx_len, t):
    n_rows = t // GRID_W
    rowp = np.repeat(np.arange(n_rows), GRID_W).astype(np.float32)
    colp = np.tile(np.arange(GRID_W), n_rows).astype(np.float32)
    n_freq = DA_HEAD // 4
    inv = jnp.asarray(ROPE_BASE, F32) ** (-jnp.arange(n_freq, dtype=F32) / n_freq)
    ar = jnp.asarray(rowp)[:, None] * inv
    ac = jnp.asarray(colp)[:, None] * inv
    ang = jnp.concatenate([ar, ar, ac, ac], axis=-1)
    cos = jnp.concatenate([jnp.ones((ctx_len, DA_HEAD), F32), jnp.cos(ang)], axis=0)
    sin = jnp.concatenate([jnp.zeros((ctx_len, DA_HEAD), F32), jnp.sin(ang)], axis=0)
    first = (np.arange(DA_HEAD) % 32) < 16
    sup = jnp.where(first, -sin, 0.0)
    sdn = jnp.where(first, 0.0, sin)
    rep = lambda a: jnp.tile(a, (batch, 2))
    return rep(cos), rep(sup), rep(sdn)


def _group_mean_matrix(width, group, scale=1.0):
    idx = np.arange(width) // group
    return jnp.asarray((idx[:, None] == idx[None, :]).astype(np.float32) * (scale / group)).astype(BF16)


def kernel(x, c, ctx, c_ctx, w_mod, b_mod, ln_g, ln_b, ffn_w_in, ffn_w_out, ev_w_in, ev_w_out,
           rw_mu, rw_w0, rw_w2, rw_a0, rw_a2, rw_g2, rw_k_k, rw_k_a, rw_r_k, rw_gn_g, rw_gn_b,
           hg_lb, hg_norm_g, od_w_in, od_w_out, da_lambda, da_norm_g):
    batch, t, d = x.shape
    ctx_len = ctx.shape[1]
    depth = w_mod.shape[0]
    assert t % ROW_BLK == 0 and ctx_len % ROW_BLK == 0 and t % GRID_W == 0
    alpha = (2 * depth) ** 0.25
    s_len = ctx_len + t
    seq_blocks = s_len // ROW_BLK
    ctx_blocks = ctx_len // ROW_BLK
    n = batch * s_len

    h = jnp.concatenate([ctx, x], axis=1).reshape(n, d)

    rows_pad = -(-(batch + 1) // SUBLANES) * SUBLANES
    cvec = jnp.zeros((rows_pad, d), F32).at[:batch].set(c).at[batch].set(c_ctx)
    m_all = _modulation(cvec, w_mod, b_mod).reshape(depth, rows_pad, N_SUB, 3, d)
    blk_row = np.array([batch if j < ctx_blocks else bi for bi in range(batch) for j in range(seq_blocks)])
    m_blk = m_all[:, blk_row]

    lower_bounds = jnp.cumsum(jax.nn.softmax(hg_lb.astype(F32), axis=0), axis=0)
    gsum64 = _group_mean_matrix(RW_W, RW_HEAD, scale=RW_HEAD)
    gm64 = _group_mean_matrix(RW_W, RW_HEAD)
    gm128 = _group_mean_matrix(HG_KW, HG_DK)

    for i in range(depth):
        j = i // 2
        with_ctx_out = i < depth - 1
        ffn = functools.partial(_ffn_sublayer, alpha=alpha)
        h = ffn(h, m_blk[i, :, 0], ffn_w_in[i, 0], ffn_w_out[i, 0], ln_g[i, 0], ln_b[i, 0])
        if i % 2 == 0:
            pad_a = jnp.zeros((d, A_PAD - A_TOTAL), F32)
            w_pad = jnp.concatenate([ev_w_in[j][:, :A_TOTAL], pad_a, ev_w_in[j][:, A_TOTAL:]], axis=1).astype(BF16)
            ua, ub = _even_proj(h, m_blk[i, :, 1], w_pad)
            mu_pad = jnp.concatenate([rw_mu[j], jnp.zeros((A_PAD - A_TOTAL,), F32)]).reshape(1, A_PAD)
            w_lora = jnp.zeros((RW_LORA_PAD, 4 * RW_W), F32)
            w_lora = w_lora.at[0:64, 0:RW_W].set(rw_w2[j, 0]).at[64:128, RW_W:2 * RW_W].set(rw_w2[j, 1])
            w_lora = w_lora.at[128:192, 2 * RW_W:3 * RW_W].set(rw_a2[j])
            w_lora = w_lora.at[192:192 + RW_GATE_LORA, 3 * RW_W:].set(rw_g2[j]).astype(BF16)
            r, k, v, z, b, lw, g_rw, bonus = _rwkv_prep(
                ua, mu_pad, w_lora, rw_w0[j], rw_a0[j], rw_k_k[j], rw_k_a[j], rw_r_k[j].reshape(RW_W), gsum64,
                seq_blocks, ctx_blocks)
            yf, yb = _rwkv_state(*_rwkv_chunks(r, k, v, z, b, lw), batch, ctx_blocks, seq_blocks)
            of, ob = _hgrn_scan(ub, lower_bounds[i], batch, ctx_blocks, seq_blocks)
            hg_norm = jnp.tile(hg_norm_g[j], HG_HEADS).reshape(1, HG_KW)
            h = _even_out(h, m_blk[i, :, 1], yf, yb, bonus, g_rw, of, ob, ub, rw_gn_g[j], rw_gn_b[j], hg_norm, gm64, gm128,
                          ev_w_out[j], ln_g[i, 1], ln_b[i, 1], alpha)
        else:
            lam_init = 0.8 - 0.6 * math.exp(-0.3 * i)
            cos, sup, sdn = _rope_tables(batch, ctx_len, t)
            q, k, v = _odd_proj(h, m_blk[i, :, 1], od_w_in[j].astype(BF16), cos, sup, sdn)
            att = _diff_attn(q, k, v, da_lambda[j], da_norm_g[j], lam_init, batch, seq_blocks, ctx_blocks)
            assert not with_ctx_out
            h = h.reshape(batch, s_len, d)[:, ctx_len:].reshape(batch * t, d)
            m_blk = m_blk[:, np.array([bi * seq_blocks + jb for bi in range(batch)
                                       for jb in range(ctx_blocks, seq_blocks)])]
            h = _odd_out(h, m_blk[i, :, 1], att, od_w_out[j], ln_g[i, 1], ln_b[i, 1], alpha)
        h = ffn(h, m_blk[i, :, 2], ffn_w_in[i, 1], ffn_w_out[i, 1], ln_g[i, 2], ln_b[i, 2])
    if h.shape[0] == n:
        h = h.reshape(batch, s_len, d)[:, ctx_len:]
    return h.reshape(batch, t, d)
```

```python
import functools
import math

import numpy as np
import jax
import jax.numpy as jnp
from jax import lax
from jax.experimental import pallas as pl
from jax.experimental.pallas import tpu as pltpu

F32 = jnp.float32
BF16 = jnp.bfloat16
HIGHEST = lax.Precision.HIGHEST

N_SUB = 3
LN_EPS = 1e-5
RMS_EPS = 1e-5
RW_HEADS = 8
RW_HEAD = 64
RW_W = RW_HEADS * RW_HEAD
RW_DECAY_LORA = 64
RW_A_LORA = 64
RW_GATE_LORA = 160
RW_GN_EPS = 64e-5
RW_MAIN = 3 * RW_W
RW_LORA = 2 * RW_DECAY_LORA + RW_A_LORA + RW_GATE_LORA
RW_LORA_PAD = 384
A_TOTAL = RW_MAIN + RW_LORA
A_PAD = RW_MAIN + RW_LORA_PAD
HG_HEADS = 4
HG_DK = 128
HG_KW = HG_HEADS * HG_DK
B_TOTAL = 5 * HG_KW
DA_HEADS = 8
DA_HEAD = 64
DA_W = DA_HEADS * 2 * DA_HEAD
GRID_W = 64
ROPE_BASE = 10000.0
Q_SCALE = DA_HEAD ** -0.5 * math.log2(math.e)

LANES = 128
SUBLANES = 8
ROW_BLK = 256
CHUNK = 64
ATT_SUB = 256
VMEM_LIMIT = 56 * 1024 * 1024


def _cparams(sem):
    return pltpu.CompilerParams(dimension_semantics=sem, vmem_limit_bytes=VMEM_LIMIT)


def _const_spec(shape):
    nd = len(shape)
    return pl.BlockSpec(shape, lambda *_: (0,) * nd, pipeline_mode=pl.Buffered(1))


def _dot(a, b):
    return jnp.dot(a, b, preferred_element_type=F32)


def _dot_nt(a, b):
    return lax.dot_general(a, b, (((1,), (1,)), ((), ())), preferred_element_type=F32)


def _dot_tn(a, b):
    return lax.dot_general(a, b, (((0,), (0,)), ((), ())), preferred_element_type=F32)


def _dot_hi(a, b):
    return jnp.dot(a, b, preferred_element_type=F32, precision=HIGHEST)


def _split(x, pieces):
    out = []
    for _ in range(pieces):
        p = x.astype(BF16)
        out.append(p)
        x = x - p.astype(F32)
    return out


def _group_sum(x, gmat):
    return sum(_dot(p, gmat) for p in _split(x, 2))


def _dot_split(a, b):
    a_hi, a_lo = _split(a, 2)
    b_hi, b_lo = _split(b, 2)
    return _dot(a_hi, b_hi) + _dot(a_hi, b_lo) + _dot(a_lo, b_hi)


def _cumsum_rows(tri, x):
    return sum(_dot(tri, p) for p in _split(x, 3))


def _silu(x):
    return x * jax.nn.sigmoid(x)


def _layernorm(y, g, b):
    mu = jnp.mean(y, axis=-1, keepdims=True)
    yc = y - mu
    var = jnp.mean(yc * yc, axis=-1, keepdims=True)
    return yc * lax.rsqrt(var + LN_EPS) * g + b


def _mod_kernel(c_ref, w_ref, b_ref, o_ref):
    o_ref[...] = _dot_hi(_silu(c_ref[...]), w_ref[...]) + b_ref[...]


def _modulation(cvec, w_mod, b_mod):
    depth, d, width = w_mod.shape
    rows = cvec.shape[0]
    tn = 1024
    return pl.pallas_call(
        _mod_kernel,
        grid=(depth, width // tn),
        in_specs=[
            pl.BlockSpec((rows, d), lambda l, j: (0, 0)),
            pl.BlockSpec((None, d, tn), lambda l, j: (l, 0, j)),
            pl.BlockSpec((None, 1, tn), lambda l, j: (l, 0, j)),
        ],
        out_specs=pl.BlockSpec((None, rows, tn), lambda l, j: (l, 0, j)),
        out_shape=jax.ShapeDtypeStruct((depth, rows, width), F32),
        compiler_params=_cparams(("parallel", "parallel")),
    )(cvec, w_mod, b_mod.reshape(depth, 1, width))


def _modulate_rows(h_ref, mod_ref, xm_ref, nblk):
    for s in range(nblk):
        rows = pl.ds(s * ROW_BLK, ROW_BLK)
        x = h_ref[rows, :]
        xm_ref[rows, :] = (x * (1.0 + mod_ref[s, 1:2, :]) + mod_ref[s, 0:1, :]).astype(xm_ref.dtype)


def _post_norm_rows(h_ref, out_ref, acc_ref, mod_ref, g_ref, b_ref, nblk, alpha, weight):
    for s in range(nblk):
        rows = pl.ds(s * ROW_BLK, ROW_BLK)
        y = alpha * h_ref[rows, :] + (weight * mod_ref[s, 2:3, :]) * acc_ref[rows, :]
        out_ref[rows, :] = _layernorm(y, g_ref[...], b_ref[...])


def _ffn_kernel(h_ref, mod_ref, wi_ref, wo_ref, g_ref, b_ref, out_ref, xm_ref, acc_ref, *, nblk, tf, alpha):
    _modulate_rows(h_ref, mod_ref, xm_ref, nblk)
    f = wo_ref.shape[0]
    for fc in range(f // tf):
        cols = slice(fc * tf, (fc + 1) * tf)
        xm = xm_ref[...]
        gt = _dot(xm, wi_ref[:, cols])
        up = _dot(xm, wi_ref[:, f + fc * tf:f + (fc + 1) * tf])
        act = (_silu(gt) * up).astype(BF16)
        part = _dot(act, wo_ref[cols, :])
        if fc == 0:
            acc_ref[...] = part
        else:
            acc_ref[...] += part
    _post_norm_rows(h_ref, out_ref, acc_ref, mod_ref, g_ref, b_ref, nblk, alpha, 0.5)


def _ffn_sublayer(h, mod3, w_in, w_out, ln_g, ln_b, alpha, tm=512, tf=256):
    n, d = h.shape
    f = w_out.shape[0]
    assert f % tf == 0
    nblk = tm // ROW_BLK
    kern = functools.partial(_ffn_kernel, nblk=nblk, tf=tf, alpha=alpha)
    return pl.pallas_call(
        kern,
        grid=(n // tm,),
        in_specs=[
            pl.BlockSpec((tm, d), lambda i: (i, 0)),
            pl.BlockSpec((nblk, 3, d), lambda i: (i, 0, 0)),
            _const_spec((d, 2 * f)),
            _const_spec((f, d)),
            _const_spec((1, d)),
            _const_spec((1, d)),
        ],
        out_specs=pl.BlockSpec((tm, d), lambda i: (i, 0)),
        out_shape=jax.ShapeDtypeStruct((n, d), F32),
        scratch_shapes=[pltpu.VMEM((tm, d), BF16), pltpu.VMEM((tm, d), F32)],
        compiler_params=_cparams(("parallel",)),
    )(h, mod3, w_in.astype(BF16), w_out.astype(BF16), ln_g.reshape(1, d), ln_b.reshape(1, d))


def _even_proj_kernel(h_ref, mod_ref, w_ref, ua_ref, ub_ref, xm_ref, *, nblk):
    _modulate_rows(h_ref, mod_ref, xm_ref, nblk)
    xm = xm_ref[...]
    ua_ref[...] = _dot(xm, w_ref[:, :A_PAD])
    ub_ref[...] = _dot(xm, w_ref[:, A_PAD:])


def _even_proj(h, mod3, w_pad, tm=512):
    n, d = h.shape
    nblk = tm // ROW_BLK
    kern = functools.partial(_even_proj_kernel, nblk=nblk)
    return pl.pallas_call(
        kern,
        grid=(n // tm,),
        in_specs=[
            pl.BlockSpec((tm, d), lambda i: (i, 0)),
            pl.BlockSpec((nblk, 3, d), lambda i: (i, 0, 0)),
            _const_spec((d, A_PAD + B_TOTAL)),
        ],
        out_specs=[pl.BlockSpec((tm, A_PAD), lambda i: (i, 0)),
                   pl.BlockSpec((tm, B_TOTAL), lambda i: (i, 0))],
        out_shape=[jax.ShapeDtypeStruct((n, A_PAD), F32), jax.ShapeDtypeStruct((n, B_TOTAL), F32)],
        scratch_shapes=[pltpu.VMEM((tm, d), BF16)],
        compiler_params=_cparams(("parallel",)),
    )(h, mod3, w_pad)


def _rwkv_prep_kernel(u_ref, prev_ref, next_ref, mu_ref, wl_ref, w0_ref, a0_ref, kk_ref, ka_ref, rk_ref,
                      gsum_ref, r_out, k_out, v_out, z_out, b_out, lw_out, g_out, bonus_out,
                      *, blocks_per_seq, ctx_blocks):
    i = pl.program_id(0)
    j = i % blocks_per_seq
    at_start = jnp.logical_or(j == 0, j == ctx_blocks)
    at_end = jnp.logical_or(j == ctx_blocks - 1, j == blocks_per_seq - 1)
    u = u_ref[...]
    prev_row = jnp.where(at_start, 0.0, prev_ref[SUBLANES - 1:SUBLANES, :])
    next_row = jnp.where(at_end, 0.0, next_ref[0:1, :])
    rid = lax.broadcasted_iota(jnp.int32, u.shape, 0)
    up = jnp.where(rid == 0, prev_row, pltpu.roll(u, 1, 0))
    dn = jnp.where(rid == ROW_BLK - 1, next_row, pltpu.roll(u, ROW_BLK - 1, 0))
    u = u + mu_ref[...] * (0.5 * (up + dn) - u)

    r = u[:, 0:RW_W]
    k = u[:, RW_W:2 * RW_W]
    v = u[:, 2 * RW_W:3 * RW_W]
    lo = u[:, RW_MAIN:A_PAD]
    lane = lax.broadcasted_iota(jnp.int32, lo.shape, 1)
    lo = jnp.where(lane < 2 * RW_DECAY_LORA, jnp.tanh(lo),
                   jnp.where(lane < 2 * RW_DECAY_LORA + RW_A_LORA, lo, jax.nn.sigmoid(lo)))
    lora = _dot(lo.astype(BF16), wl_ref[...])
    a = jax.nn.sigmoid(a0_ref[...] + lora[:, 2 * RW_W:3 * RW_W])
    g_out[...] = lora[:, 3 * RW_W:4 * RW_W]
    for dd in range(2):
        x = -(w0_ref[dd:dd + 1, :] + lora[:, dd * RW_W:(dd + 1) * RW_W])
        softplus = jnp.maximum(x, 0.0) + jnp.log(1.0 + jnp.exp(-jnp.abs(x)))
        lw_out[dd] = -jnp.exp(-softplus - 0.5)
    kk = k * kk_ref[...]
    ss = _group_sum(kk * kk, gsum_ref[...])
    kk = kk / jnp.maximum(jnp.sqrt(ss), 1e-12)
    k = k * (1.0 + (a - 1.0) * ka_ref[...])
    r_out[...] = r
    k_out[...] = k
    v_out[...] = v
    z_out[...] = -kk
    b_out[...] = kk * a
    bonus_out[...] = _group_sum(r * k * rk_ref[...], gsum_ref[...]) * v


def _rwkv_prep_chunk_kernel(u_ref, prev_ref, next_ref, mu_ref, wl_ref, w0_ref, a0_ref, kk_ref, ka_ref, rk_ref,
                            gsum_ref, rp_ref, y0_ref, mx_ref, s0_ref, pt_ref, g_out, bonus_out,
                            r_s, k_s, v_s, z_s, b_s, lw_s, *, blocks_per_seq, ctx_blocks):
    _rwkv_prep_kernel(u_ref, prev_ref, next_ref, mu_ref, wl_ref, w0_ref, a0_ref, kk_ref, ka_ref, rk_ref,
                      gsum_ref, r_s, k_s, v_s, z_s, b_s, lw_s, g_out, bonus_out,
                      blocks_per_seq=blocks_per_seq, ctx_blocks=ctx_blocks)
    _rwkv_chunk_kernel(r_s, k_s, v_s, z_s, b_s, lw_s, rp_ref, y0_ref, mx_ref, s0_ref, pt_ref)


def _rwkv_prep_chunks(ua, mu_pad, w_lora, w0, a0, k_k, k_a, r_k, gsum, blocks_per_seq, ctx_blocks):
    n = ua.shape[0]
    nb = n // ROW_BLK
    halo = ROW_BLK // SUBLANES
    last8 = n // SUBLANES - 1
    kern = functools.partial(_rwkv_prep_chunk_kernel, blocks_per_seq=blocks_per_seq, ctx_blocks=ctx_blocks)
    row = lambda i: (i, 0)
    drow = lambda i: (0, i, 0)
    vec = _const_spec((1, RW_W))
    dspec = pl.BlockSpec((2, ROW_BLK, RW_W), drow)
    out512 = jax.ShapeDtypeStruct((n, RW_W), F32)
    big = jax.ShapeDtypeStruct((2, n, RW_W), F32)
    tile = pltpu.VMEM((ROW_BLK, RW_W), F32)
    return pl.pallas_call(
        kern,
        grid=(nb,),
        in_specs=[
            pl.BlockSpec((ROW_BLK, A_PAD), row),
            pl.BlockSpec((SUBLANES, A_PAD), lambda i: (jnp.maximum(i * halo - 1, 0), 0)),
            pl.BlockSpec((SUBLANES, A_PAD), lambda i: (jnp.minimum((i + 1) * halo, last8), 0)),
            _const_spec((1, A_PAD)),
            _const_spec((RW_LORA_PAD, 4 * RW_W)),
            _const_spec((2, RW_W)),
            vec, vec, vec, vec,
            _const_spec((RW_W, RW_W)),
        ],
        out_specs=[dspec] * 4 + [pl.BlockSpec((2, CPB, 1, RW_W), lambda i: (0, i, 0, 0))]
        + [pl.BlockSpec((ROW_BLK, RW_W), row)] * 2,
        out_shape=[jax.ShapeDtypeStruct((2, n, RW_W), BF16), big, big, big,
                   jax.ShapeDtypeStruct((2, n // CHUNK, 1, RW_W), F32), out512, out512],
        scratch_shapes=[tile] * 5 + [pltpu.VMEM((2, ROW_BLK, RW_W), F32)],
        compiler_params=_cparams(("parallel",)),
    )(ua, ua, ua, mu_pad, w_lora, w0, a0.reshape(1, RW_W), k_k.reshape(1, RW_W), k_a.reshape(1, RW_W),
      r_k.reshape(1, RW_W), gsum)


CPB = ROW_BLK // CHUNK


def _block_index(d, g, ctx_blocks, seq_blocks):
    if d == 0:
        return g
    return jnp.where(g < ctx_blocks, ctx_blocks - 1 - g, seq_blocks + ctx_blocks - 1 - g)


def _chunk_order(d):
    return range(CPB) if d == 0 else range(CPB - 1, -1, -1)


def _incl_mask(d):
    row = lax.broadcasted_iota(jnp.int32, (CHUNK, CHUNK), 0)
    col = lax.broadcasted_iota(jnp.int32, (CHUNK, CHUNK), 1)
    return (row - col) * (1 - 2 * d) >= 0


def _rwkv_chunk_kernel(r_ref, k_ref, v_ref, z_ref, b_ref, lw_ref, rp_ref, y0_ref, mx_ref, s0_ref, pt_ref):
    zeros = jnp.zeros((CHUNK, RW_HEAD), BF16)
    row = lax.broadcasted_iota(jnp.int32, (2 * CHUNK, 2 * CHUNK), 0)
    col = lax.broadcasted_iota(jnp.int32, (2 * CHUNK, 2 * CHUNK), 1)
    chains = []
    for cc in range(r_ref.shape[0] // CHUNK):
        rows = slice(cc * CHUNK, (cc + 1) * CHUNK)
        r = r_ref[rows, :]
        k = k_ref[rows, :]
        z = z_ref[rows, :]
        b = b_ref[rows, :]
        vb = v_ref[rows, :].astype(BF16)
        for d in range(2):
            order = ((row & (CHUNK - 1)) - (col & (CHUNK - 1))) * (1 - 2 * d)
            mask = order + (row >> 6) > 0
            lw = lw_ref[d, rows, :]
            cs = _cumsum_rows(_incl_mask(d).astype(BF16), lw)
            tot = jnp.sum(lw, axis=0, keepdims=True)
            p_inc = jnp.exp(cs)
            p_inv = jnp.exp(-cs)
            p_end = jnp.exp(tot - cs)
            zt = z * jnp.exp(cs - lw)
            rt = r * p_inc
            zr = jnp.concatenate([zt, rt], axis=0).astype(BF16)
            bk = jnp.concatenate([b * p_inv, k * p_inv], axis=0).astype(BF16)
            be = (b * p_end).astype(BF16)
            ke = (k * p_end).astype(BF16)
            pt_ref[d, cc] = jnp.exp(tot)
            for h in range(RW_HEADS):
                sl = slice(h * RW_HEAD, (h + 1) * RW_HEAD)
                g = jnp.where(mask, _dot_nt(zr[:, sl], bk[:, sl]), 0.0)
                chains.append(dict(d=d, rows=rows, sl=sl, vh=vb[:, sl], zt=zt[:, sl], rt=rt[:, sl],
                                   be=be[:, sl], ke=ke[:, sl], gtop=g[:CHUNK].astype(BF16),
                                   gbot=g[CHUNK:].astype(BF16), p=g[:CHUNK, :CHUNK]))
    for c in chains:
        akv = _dot(c["gtop"], jnp.concatenate([zeros, c["vh"]], axis=0))
        c["x"] = jnp.concatenate([c["zt"], akv], axis=1)
    for it in range(6):
        for c in chains:
            pb = c["p"].astype(BF16)
            if it < 5:
                both = _dot(pb, jnp.concatenate([c["x"].astype(BF16), pb], axis=1))
                c["x"] = c["x"] + both[:, :2 * RW_HEAD]
                c["p"] = both[:, 2 * RW_HEAD:]
            else:
                c["x"] = c["x"] + _dot(pb, c["x"].astype(BF16))
    for c in chains:
        c["xb"] = c["x"].astype(BF16)
        rhs = jnp.concatenate([c["xb"], jnp.concatenate([zeros, c["vh"]], axis=1)], axis=0)
        c["ry"] = _dot(c["gbot"], rhs)
    for c in chains:
        c["ms"] = _dot_tn(c["xb"], c["be"])
        c["vk"] = _dot_tn(c["vh"], c["ke"])
    for c in chains:
        d, rows, sl = c["d"], c["rows"], c["sl"]
        rp_ref[d, rows, sl] = (c["rt"] + c["ry"][:, :RW_HEAD]).astype(rp_ref.dtype)
        y0_ref[d, rows, sl] = c["ry"][:, RW_HEAD:]
        mx_ref[d, rows, sl] = c["ms"][:RW_HEAD]
        s0_ref[d, rows, sl] = c["ms"][RW_HEAD:] + c["vk"]


def _rwkv_state_kernel(rp0, y00, mx0, s00, pt0, rp1, y01, mx1, s01, pt1, yf_ref, yb_ref, s_ref):
    @pl.when(pl.program_id(1) == 0)
    def _():
        s_ref[...] = jnp.zeros_like(s_ref)

    dirs = ((rp0, y00, mx0, s00, pt0, yf_ref), (rp1, y01, mx1, s01, pt1, yb_ref))
    heads = [(d, h) for d in range(2) for h in range(RW_HEADS)]
    state = {dh: s_ref[dh[0], dh[1]] for dh in heads}
    for step in range(CPB):
        for d, h in heads:
            rp, y0, mx, s0, pt, y_ref = dirs[d]
            cc = _chunk_order(d)[step]
            rows = slice(cc * CHUNK, (cc + 1) * CHUNK)
            sl = slice(h * RW_HEAD, (h + 1) * RW_HEAD)
            s = state[d, h]
            y_ref[rows, sl] = _dot_nt(rp[rows, sl], s.astype(BF16)) + y0[rows, sl]
            state[d, h] = s * pt[cc, :, sl] + _dot_split(s, mx[rows, sl]) + s0[rows, sl]
    for d, h in heads:
        s_ref[d, h] = state[d, h]


def _rwkv_state(rp, y0, mx, s0, pt, batch, ctx_blocks, seq_blocks):
    n = y0.shape[1]
    in_specs = []
    for d in range(2):
        blk = lambda bi, g, d=d: bi * seq_blocks + _block_index(d, g, ctx_blocks, seq_blocks)
        big = pl.BlockSpec((None, ROW_BLK, RW_W), lambda bi, g, d=d, blk=blk: (d, blk(bi, g), 0))
        in_specs += [big] * 4 + [pl.BlockSpec((None, CPB, 1, RW_W), lambda bi, g, d=d, blk=blk: (d, blk(bi, g), 0, 0))]
    out_specs = [pl.BlockSpec((ROW_BLK, RW_W),
                              lambda bi, g, d=d: (bi * seq_blocks + _block_index(d, g, ctx_blocks, seq_blocks), 0))
                 for d in range(2)]
    out = jax.ShapeDtypeStruct((n, RW_W), F32)
    return pl.pallas_call(
        _rwkv_state_kernel,
        grid=(batch, seq_blocks),
        in_specs=in_specs,
        out_specs=out_specs,
        out_shape=[out, out],
        scratch_shapes=[pltpu.VMEM((2, RW_HEADS, RW_HEAD, RW_HEAD), F32)],
        compiler_params=_cparams(("parallel", "arbitrary")),
    )(rp, y0, mx, s0, pt, rp, y0, mx, s0, pt)


def _hgrn_scan_kernel(q0_ref, f0_ref, i0_ref, q1_ref, f1_ref, i1_ref, lb_ref, of_ref, ob_ref, s_ref):
    @pl.when(pl.program_id(1) == 0)
    def _():
        s_ref[...] = jnp.zeros_like(s_ref)

    lb = lb_ref[...]
    dirs = ((q0_ref, f0_ref, i0_ref, of_ref), (q1_ref, f1_ref, i1_ref, ob_ref))
    groups = []
    for d, (q_ref, f_ref, i_ref, o_ref) in enumerate(dirs):
        for cc in _chunk_order(d):
            rows = slice(cc * CHUNK, (cc + 1) * CHUNK)
            fg = lb + (1.0 - lb) * jax.nn.sigmoid(f_ref[rows, :])
            groups.append(dict(d=d, rows=rows, o_ref=o_ref, q=_silu(q_ref[rows, :]), logf=jnp.log(fg),
                               k=1.0 - fg, vb=i_ref[rows, :].astype(BF16)))
    for g in groups:
        g["cs"] = _cumsum_rows(_incl_mask(g["d"]).astype(BF16), g["logf"])
    for g in groups:
        cs = g["cs"]
        tot = jnp.sum(g["logf"], axis=0, keepdims=True)
        g["qb"] = (g["q"] * jnp.exp(cs)).astype(BF16)
        g["kb"] = (g["k"] * jnp.exp(-cs)).astype(BF16)
        g["kd"] = (g["k"] * jnp.exp(tot - cs)).astype(BF16)
        g["gl"] = jnp.exp(tot)
    heads = [slice(h * HG_DK, (h + 1) * HG_DK) for h in range(HG_HEADS)]
    for g in groups:
        incl = _incl_mask(g["d"])
        g["sc"] = [jnp.where(incl, _dot_nt(g["qb"][:, sl], g["kb"][:, sl]), 0.0).astype(BF16) for sl in heads]
    for g in groups:
        g["intra"] = [_dot(sc, g["vb"][:, sl]) for sc, sl in zip(g["sc"], heads)]
        g["kv"] = [_dot_tn(g["vb"][:, sl], g["kd"][:, sl]) for sl in heads]
    state = {(d, h): s_ref[d, h] for d in range(2) for h in range(HG_HEADS)}
    for g in groups:
        d = g["d"]
        for h, sl in enumerate(heads):
            s = state[d, h]
            g["o_ref"][g["rows"], sl] = g["intra"][h] + _dot_nt(g["qb"][:, sl], s.astype(BF16))
            state[d, h] = s * g["gl"][:, sl] + g["kv"][h]
    for (d, h), s in state.items():
        s_ref[d, h] = s


def _hgrn_scan(ub, lb, batch, ctx_blocks, seq_blocks):
    n = ub.shape[0]
    in_specs = []
    out_specs = []
    for d in range(2):
        blk = lambda bi, g, d=d: bi * seq_blocks + _block_index(d, g, ctx_blocks, seq_blocks)
        in_specs += [
            pl.BlockSpec((ROW_BLK, HG_KW), lambda bi, g, blk=blk: (blk(bi, g), 0)),
            pl.BlockSpec((ROW_BLK, HG_KW), lambda bi, g, blk=blk, d=d: (blk(bi, g), 1 + d)),
            pl.BlockSpec((ROW_BLK, HG_KW), lambda bi, g, blk=blk: (blk(bi, g), 3)),
        ]
        out_specs.append(pl.BlockSpec((ROW_BLK, HG_KW), lambda bi, g, blk=blk: (blk(bi, g), 0)))
    out = jax.ShapeDtypeStruct((n, HG_KW), F32)
    return pl.pallas_call(
        _hgrn_scan_kernel,
        grid=(batch, seq_blocks),
        in_specs=in_specs + [_const_spec((1, HG_KW))],
        out_specs=out_specs,
        out_shape=[out, out],
        scratch_shapes=[pltpu.VMEM((2, HG_HEADS, HG_DK, HG_DK), F32)],
        compiler_params=_cparams(("parallel", "arbitrary")),
    )(ub, ub, ub, ub, ub, ub, lb.reshape(1, HG_KW))


def _even_out_kernel(h_ref, mod_ref, yf_ref, yb_ref, bonus_ref, grw_ref, of_ref, ob_ref, ghg_ref, gng_ref, gnb_ref,
                     hgn_ref, gm64_ref, gm128_ref, w_ref, g_ref, b_ref, out_ref, acc_ref, *, nblk, alpha):
    y = yf_ref[...] + yb_ref[...]
    mu = _group_sum(y, gm64_ref[...])
    yc = y - mu
    var = _group_sum(yc * yc, gm64_ref[...])
    ra = (yc * lax.rsqrt(var + RW_GN_EPS) * gng_ref[...] + gnb_ref[...] + bonus_ref[...]) * grw_ref[...]
    o = of_ref[...] + ob_ref[...]
    ms = _group_sum(o * o, gm128_ref[...])
    hb = o * lax.rsqrt(ms + RMS_EPS) * hgn_ref[...] * _silu(ghg_ref[...])
    acc_ref[...] = _dot(ra.astype(BF16), w_ref[:RW_W, :]) + _dot(hb.astype(BF16), w_ref[RW_W:, :])
    _post_norm_rows(h_ref, out_ref, acc_ref, mod_ref, g_ref, b_ref, nblk, alpha, 1.0)


def _even_out(h, mod3, yf, yb, bonus, g_rw, of, ob, ub, gn_g, gn_b, hg_norm, gm64, gm128, w_out, ln_g, ln_b, alpha,
              tm=256):
    n, d = h.shape
    nblk = tm // ROW_BLK
    kern = functools.partial(_even_out_kernel, nblk=nblk, alpha=alpha)
    row = lambda i: (i, 0)
    vec = _const_spec((1, RW_W))
    return pl.pallas_call(
        kern,
        grid=(n // tm,),
        in_specs=[
            pl.BlockSpec((tm, d), row),
            pl.BlockSpec((nblk, 3, d), lambda i: (i, 0, 0)),
            pl.BlockSpec((tm, RW_W), row),
            pl.BlockSpec((tm, RW_W), row),
            pl.BlockSpec((tm, RW_W), row),
            pl.BlockSpec((tm, RW_W), row),
            pl.BlockSpec((tm, HG_KW), row),
            pl.BlockSpec((tm, HG_KW), row),
            pl.BlockSpec((tm, HG_KW), lambda i: (i, 4)),
            vec, vec, vec,
            _const_spec((RW_W, RW_W)),
            _const_spec((HG_KW, HG_KW)),
            _const_spec((d, d)),
            _const_spec((1, d)),
            _const_spec((1, d)),
        ],
        out_specs=pl.BlockSpec((tm, d), row),
        out_shape=jax.ShapeDtypeStruct((n, d), F32),
        scratch_shapes=[pltpu.VMEM((tm, d), F32)],
        compiler_params=_cparams(("parallel",)),
    )(h, mod3, yf, yb, bonus, g_rw, of, ob, ub, gn_g.reshape(1, RW_W), gn_b.reshape(1, RW_W), hg_norm, gm64, gm128,
      w_out.astype(BF16), ln_g.reshape(1, d), ln_b.reshape(1, d))


def _rope(t, cos, sin_up, sin_dn):
    return t * cos + pltpu.roll(t, LANES - 16, 1) * sin_up + pltpu.roll(t, 16, 1) * sin_dn


def _odd_proj_kernel(h_ref, mod_ref, w_ref, cos_ref, sup_ref, sdn_ref, q_ref, k_ref, v_ref, xm_ref, *, nblk):
    _modulate_rows(h_ref, mod_ref, xm_ref, nblk)
    xm = xm_ref[...]
    cos, sup, sdn = cos_ref[...], sup_ref[...], sdn_ref[...]
    q = _dot(xm, w_ref[:, :DA_W])
    k = _dot(xm, w_ref[:, DA_W:2 * DA_W])
    for h in range(DA_HEADS):
        sl = slice(h * LANES, (h + 1) * LANES)
        q_ref[:, sl] = (_rope(q[:, sl], cos, sup, sdn) * Q_SCALE).astype(q_ref.dtype)
        k_ref[:, sl] = _rope(k[:, sl], cos, sup, sdn).astype(k_ref.dtype)
    v_ref[...] = _dot(xm, w_ref[:, 2 * DA_W:]).astype(v_ref.dtype)


def _odd_proj(h, mod3, w, cos, sup, sdn, tm=512):
    n, d = h.shape
    nblk = tm // ROW_BLK
    kern = functools.partial(_odd_proj_kernel, nblk=nblk)
    row = lambda i: (i, 0)
    out = jax.ShapeDtypeStruct((n, DA_W), BF16)
    return pl.pallas_call(
        kern,
        grid=(n // tm,),
        in_specs=[
            pl.BlockSpec((tm, d), row),
            pl.BlockSpec((nblk, 3, d), lambda i: (i, 0, 0)),
            _const_spec((d, 3 * DA_W)),
            pl.BlockSpec((tm, LANES), row),
            pl.BlockSpec((tm, LANES), row),
            pl.BlockSpec((tm, LANES), row),
        ],
        out_specs=[pl.BlockSpec((tm, DA_W), row)] * 3,
        out_shape=[out, out, out],
        scratch_shapes=[pltpu.VMEM((tm, d), BF16)],
        compiler_params=_cparams(("parallel",)),
    )(h, mod3, w, cos, sup, sdn)


def _diff_attn_kernel(q_ref, k_ref, v_ref, lam_ref, ng_ref, o_ref, *, lam_init, q0, bq):
    base = q0 + pl.program_id(2) * bq
    lv = lam_ref[...]
    lam = (jnp.exp(jnp.sum(lv[0:1] * lv[1:2], axis=-1, keepdims=True))
           - jnp.exp(jnp.sum(lv[2:3] * lv[3:4], axis=-1, keepdims=True)) + lam_init)
    k = k_ref[...]
    halves = [slice(i * ATT_SUB, (i + 1) * ATT_SUB) for i in range(bq // ATT_SUB)]

    def scores(rows):
        q = q_ref[pl.ds(pl.multiple_of(base + rows.start, ATT_SUB), ATT_SUB), :]
        return [_dot_nt(q[:, m * DA_HEAD:(m + 1) * DA_HEAD], k[:, m * DA_HEAD:(m + 1) * DA_HEAD])
                for m in range(2)]

    def readout(rows, s):
        es, ls = [], []
        for m in range(2):
            e = jnp.exp2(s[m] - jnp.max(s[m], axis=-1, keepdims=True))
            es.append(e)
            ls.append(jnp.sum(e, axis=-1, keepdims=True))
        a = es[0] - (lam * ls[0] / ls[1]) * es[1]
        o = _dot(a.astype(BF16), v_ref[...]) / ls[0]
        ms = jnp.mean(o * o, axis=-1, keepdims=True)
        o_ref[rows, :] = (o * lax.rsqrt(ms + RMS_EPS) * ng_ref[...] * (1.0 - lam_init)).astype(o_ref.dtype)

    s_next = scores(halves[0])
    for i, rows in enumerate(halves):
        s_cur = s_next
        if i + 1 < len(halves):
            s_next = scores(halves[i + 1])
        readout(rows, s_cur)


def _diff_attn(q, k, v, lam_vecs, norm_g, lam_init, batch, seq_blocks, ctx_blocks, bq=2 * ATT_SUB):
    s_len = seq_blocks * ROW_BLK
    t_len = (seq_blocks - ctx_blocks) * ROW_BLK
    assert t_len % bq == 0
    nt = t_len // bq
    kern = functools.partial(_diff_attn_kernel, lam_init=lam_init, q0=ctx_blocks * ROW_BLK, bq=bq)
    seq_spec = pl.BlockSpec((None, s_len, LANES), lambda b, h, i: (b, 0, h))
    return pl.pallas_call(
        kern,
        grid=(batch, DA_HEADS, nt),
        in_specs=[seq_spec, seq_spec, seq_spec, _const_spec((4, DA_HEAD)), _const_spec((1, LANES))],
        out_specs=pl.BlockSpec((bq, LANES), lambda b, h, i: (b * nt + i, h)),
        out_shape=jax.ShapeDtypeStruct((batch * t_len, DA_W), BF16),
        compiler_params=_cparams(("parallel", "parallel", "arbitrary")),
    )(q.reshape(batch, s_len, DA_W), k.reshape(batch, s_len, DA_W), v.reshape(batch, s_len, DA_W),
      lam_vecs, norm_g.reshape(1, LANES))


def _odd_out_kernel(h_ref, mod_ref, x_ref, w_ref, g_ref, b_ref, out_ref, acc_ref, *, nblk, alpha):
    acc_ref[...] = _dot(x_ref[...], w_ref[...])
    _post_norm_rows(h_ref, out_ref, acc_ref, mod_ref, g_ref, b_ref, nblk, alpha, 1.0)


def _odd_out(h, mod3, x, w_out, ln_g, ln_b, alpha, tm=512):
    n, d = h.shape
    nblk = tm // ROW_BLK
    kern = functools.partial(_odd_out_kernel, nblk=nblk, alpha=alpha)
    row = lambda i: (i, 0)
    return pl.pallas_call(
        kern,
        grid=(n // tm,),
        in_specs=[
            pl.BlockSpec((tm, d), row),
            pl.BlockSpec((nblk, 3, d), lambda i: (i, 0, 0)),
            pl.BlockSpec((tm, DA_W), row),
            _const_spec((DA_W, d)),
            _const_spec((1, d)),
            _const_spec((1, d)),
        ],
        out_specs=pl.BlockSpec((tm, d), row),
        out_shape=jax.ShapeDtypeStruct((n, d), F32),
        scratch_shapes=[pltpu.VMEM((tm, d), F32)],
        compiler_params=_cparams(("parallel",)),
    )(h, mod3, x, w_out.astype(BF16), ln_g.reshape(1, d), ln_b.reshape(1, d))


def _rope_tables(batch, ctx_len, t):
    n_rows = t // GRID_W
    rowp = np.repeat(np.arange(n_rows), GRID_W).astype(np.float32)
    colp = np.tile(np.arange(GRID_W), n_rows).astype(np.float32)
    n_freq = DA_HEAD // 4
    inv = jnp.asarray(ROPE_BASE, F32) ** (-jnp.arange(n_freq, dtype=F32) / n_freq)
    ar = jnp.asarray(rowp)[:, None] * inv
    ac = jnp.asarray(colp)[:, None] * inv
    ang = jnp.concatenate([ar, ar, ac, ac], axis=-1)
    cos = jnp.concatenate([jnp.ones((ctx_len, DA_HEAD), F32), jnp.cos(ang)], axis=0)
    sin = jnp.concatenate([jnp.zeros((ctx_len, DA_HEAD), F32), jnp.sin(ang)], axis=0)
    first = (np.arange(DA_HEAD) % 32) < 16
    sup = jnp.where(first, -sin, 0.0)
    sdn = jnp.where(first, 0.0, sin)
    rep = lambda a: jnp.tile(a, (batch, 2))
    return rep(cos), rep(sup), rep(sdn)


def _group_mean_matrix(width, group, scale=1.0):
    idx = np.arange(width) // group
    return jnp.asarray((idx[:, None] == idx[None, :]).astype(np.float32) * (scale / group)).astype(BF16)


def kernel(x, c, ctx, c_ctx, w_mod, b_mod, ln_g, ln_b, ffn_w_in, ffn_w_out, ev_w_in, ev_w_out,
           rw_mu, rw_w0, rw_w2, rw_a0, rw_a2, rw_g2, rw_k_k, rw_k_a, rw_r_k, rw_gn_g, rw_gn_b,
           hg_lb, hg_norm_g, od_w_in, od_w_out, da_lambda, da_norm_g):
    batch, t, d = x.shape
    ctx_len = ctx.shape[1]
    depth = w_mod.shape[0]
    assert t % ROW_BLK == 0 and ctx_len % ROW_BLK == 0 and t % GRID_W == 0
    alpha = (2 * depth) ** 0.25
    s_len = ctx_len + t
    seq_blocks = s_len // ROW_BLK
    ctx_blocks = ctx_len // ROW_BLK
    n = batch * s_len

    h = jnp.concatenate([ctx, x], axis=1).reshape(n, d)

    rows_pad = -(-(batch + 1) // SUBLANES) * SUBLANES
    cvec = jnp.zeros((rows_pad, d), F32).at[:batch].set(c).at[batch].set(c_ctx)
    m_all = _modulation(cvec, w_mod, b_mod).reshape(depth, rows_pad, N_SUB, 3, d)
    blk_row = np.array([batch if j < ctx_blocks else bi for bi in range(batch) for j in range(seq_blocks)])
    m_blk = m_all[:, blk_row]

    lower_bounds = jnp.cumsum(jax.nn.softmax(hg_lb.astype(F32), axis=0), axis=0)
    gsum64 = _group_mean_matrix(RW_W, RW_HEAD, scale=RW_HEAD)
    gm64 = _group_mean_matrix(RW_W, RW_HEAD)
    gm128 = _group_mean_matrix(HG_KW, HG_DK)

    for i in range(depth):
        j = i // 2
        with_ctx_out = i < depth - 1
        ffn = functools.partial(_ffn_sublayer, alpha=alpha)
        h = ffn(h, m_blk[i, :, 0], ffn_w_in[i, 0], ffn_w_out[i, 0], ln_g[i, 0], ln_b[i, 0])
        if i % 2 == 0:
            pad_a = jnp.zeros((d, A_PAD - A_TOTAL), F32)
            w_pad = jnp.concatenate([ev_w_in[j][:, :A_TOTAL], pad_a, ev_w_in[j][:, A_TOTAL:]], axis=1).astype(BF16)
            ua, ub = _even_proj(h, m_blk[i, :, 1], w_pad)
            mu_pad = jnp.concatenate([rw_mu[j], jnp.zeros((A_PAD - A_TOTAL,), F32)]).reshape(1, A_PAD)
            w_lora = jnp.zeros((RW_LORA_PAD, 4 * RW_W), F32)
            w_lora = w_lora.at[0:64, 0:RW_W].set(rw_w2[j, 0]).at[64:128, RW_W:2 * RW_W].set(rw_w2[j, 1])
            w_lora = w_lora.at[128:192, 2 * RW_W:3 * RW_W].set(rw_a2[j])
            w_lora = w_lora.at[192:192 + RW_GATE_LORA, 3 * RW_W:].set(rw_g2[j]).astype(BF16)
            rp, y0, mx, s0, pt, g_rw, bonus = _rwkv_prep_chunks(
                ua, mu_pad, w_lora, rw_w0[j], rw_a0[j], rw_k_k[j], rw_k_a[j], rw_r_k[j].reshape(RW_W), gsum64,
                seq_blocks, ctx_blocks)
            yf, yb = _rwkv_state(rp, y0, mx, s0, pt, batch, ctx_blocks, seq_blocks)
            of, ob = _hgrn_scan(ub, lower_bounds[i], batch, ctx_blocks, seq_blocks)
            hg_norm = jnp.tile(hg_norm_g[j], HG_HEADS).reshape(1, HG_KW)
            h = _even_out(h, m_blk[i, :, 1], yf, yb, bonus, g_rw, of, ob, ub, rw_gn_g[j], rw_gn_b[j], hg_norm, gm64, gm128,
                          ev_w_out[j], ln_g[i, 1], ln_b[i, 1], alpha)
        else:
            lam_init = 0.8 - 0.6 * math.exp(-0.3 * i)
            cos, sup, sdn = _rope_tables(batch, ctx_len, t)
            q, k, v = _odd_proj(h, m_blk[i, :, 1], od_w_in[j].astype(BF16), cos, sup, sdn)
            att = _diff_attn(q, k, v, da_lambda[j], da_norm_g[j], lam_init, batch, seq_blocks, ctx_blocks)
            assert not with_ctx_out
            h = h.reshape(batch, s_len, d)[:, ctx_len:].reshape(batch * t, d)
            m_blk = m_blk[:, np.array([bi * seq_blocks + jb for bi in range(batch)
                                       for jb in range(ctx_blocks, seq_blocks)])]
            h = _odd_out(h, m_blk[i, :, 1], att, od_w_out[j], ln_g[i, 1], ln_b[i, 1], alpha)
        h = ffn(h, m_blk[i, :, 2], ffn_w_in[i, 1], ffn_w_out[i, 1], ln_g[i, 2], ln_b[i, 2])
    if h.shape[0] == n:
        h = h.reshape(batch, s_len, d)[:, ctx_len:]
    return h.reshape(batch, t, d)
```

```python
import functools
import math

import numpy as np
import jax
import jax.numpy as jnp
from jax import lax
from jax.experimental import pallas as pl
from jax.experimental.pallas import tpu as pltpu

F32 = jnp.float32
BF16 = jnp.bfloat16
HIGHEST = lax.Precision.HIGHEST

N_SUB = 3
LN_EPS = 1e-5
RMS_EPS = 1e-5
RW_HEADS = 8
RW_HEAD = 64
RW_W = RW_HEADS * RW_HEAD
RW_DECAY_LORA = 64
RW_A_LORA = 64
RW_GATE_LORA = 160
RW_GN_EPS = 64e-5
RW_MAIN = 3 * RW_W
RW_LORA = 2 * RW_DECAY_LORA + RW_A_LORA + RW_GATE_LORA
RW_LORA_PAD = 384
A_TOTAL = RW_MAIN + RW_LORA
A_PAD = RW_MAIN + RW_LORA_PAD
HG_HEADS = 4
HG_DK = 128
HG_KW = HG_HEADS * HG_DK
B_TOTAL = 5 * HG_KW
DA_HEADS = 8
DA_HEAD = 64
DA_W = DA_HEADS * 2 * DA_HEAD
GRID_W = 64
ROPE_BASE = 10000.0
Q_SCALE = DA_HEAD ** -0.5 * math.log2(math.e)

LANES = 128
SUBLANES = 8
ROW_BLK = 256
CHUNK = 64
ATT_SUB = 256
VMEM_LIMIT = 56 * 1024 * 1024


def _cparams(sem):
    return pltpu.CompilerParams(dimension_semantics=sem, vmem_limit_bytes=VMEM_LIMIT)


def _const_spec(shape):
    nd = len(shape)
    return pl.BlockSpec(shape, lambda *_: (0,) * nd, pipeline_mode=pl.Buffered(1))


def _dot(a, b):
    return jnp.dot(a, b, preferred_element_type=F32)


def _dot_nt(a, b):
    return lax.dot_general(a, b, (((1,), (1,)), ((), ())), preferred_element_type=F32)


def _dot_tn(a, b):
    return lax.dot_general(a, b, (((0,), (0,)), ((), ())), preferred_element_type=F32)


def _dot_hi(a, b):
    return jnp.dot(a, b, preferred_element_type=F32, precision=HIGHEST)


def _split(x, pieces):
    out = []
    for _ in range(pieces):
        p = x.astype(BF16)
        out.append(p)
        x = x - p.astype(F32)
    return out


def _group_sum(x, gmat):
    return sum(_dot(p, gmat) for p in _split(x, 2))


def _dot_split(a, b):
    a_hi, a_lo = _split(a, 2)
    b_hi, b_lo = _split(b, 2)
    return _dot(a_hi, b_hi) + _dot(a_hi, b_lo) + _dot(a_lo, b_hi)


def _cumsum_rows(tri, x):
    return sum(_dot(tri, p) for p in _split(x, 3))


def _silu(x):
    return x * jax.nn.sigmoid(x)


def _layernorm(y, g, b):
    mu = jnp.mean(y, axis=-1, keepdims=True)
    yc = y - mu
    var = jnp.mean(yc * yc, axis=-1, keepdims=True)
    return yc * lax.rsqrt(var + LN_EPS) * g + b


def _mod_kernel(c_ref, w_ref, b_ref, o_ref):
    o_ref[...] = _dot_hi(_silu(c_ref[...]), w_ref[...]) + b_ref[...]


def _modulation(cvec, w_mod, b_mod):
    depth, d, width = w_mod.shape
    rows = cvec.shape[0]
    tn = 1024
    return pl.pallas_call(
        _mod_kernel,
        grid=(depth, width // tn),
        in_specs=[
            pl.BlockSpec((rows, d), lambda l, j: (0, 0)),
            pl.BlockSpec((None, d, tn), lambda l, j: (l, 0, j)),
            pl.BlockSpec((None, 1, tn), lambda l, j: (l, 0, j)),
        ],
        out_specs=pl.BlockSpec((None, rows, tn), lambda l, j: (l, 0, j)),
        out_shape=jax.ShapeDtypeStruct((depth, rows, width), F32),
        compiler_params=_cparams(("parallel", "parallel")),
    )(cvec, w_mod, b_mod.reshape(depth, 1, width))


def _modulate_rows(h_ref, mod_ref, xm_ref, nblk):
    for s in range(nblk):
        rows = pl.ds(s * ROW_BLK, ROW_BLK)
        x = h_ref[rows, :]
        xm_ref[rows, :] = (x * (1.0 + mod_ref[s, 1:2, :]) + mod_ref[s, 0:1, :]).astype(xm_ref.dtype)


def _post_norm_rows(h_ref, out_ref, acc_ref, mod_ref, g_ref, b_ref, nblk, alpha, weight):
    for s in range(nblk):
        rows = pl.ds(s * ROW_BLK, ROW_BLK)
        y = alpha * h_ref[rows, :] + (weight * mod_ref[s, 2:3, :]) * acc_ref[rows, :]
        out_ref[rows, :] = _layernorm(y, g_ref[...], b_ref[...])


def _ffn_two_source_kernel(ha_ref, hb_ref, *rest, n_first, **kw):
    *refs, h_sel = rest

    @pl.when(pl.program_id(0) < n_first)
    def _():
        h_sel[...] = ha_ref[...].astype(h_sel.dtype)

    @pl.when(pl.program_id(0) >= n_first)
    def _():
        h_sel[...] = hb_ref[...].astype(h_sel.dtype)

    _ffn_kernel(h_sel, *refs, **kw)


def _ffn_kernel(h_ref, mod_ref, wi_ref, wo_ref, g_ref, b_ref, out_ref, xm_ref, acc_ref, *, nblk, tf, alpha):
    _modulate_rows(h_ref, mod_ref, xm_ref, nblk)
    f = wo_ref.shape[0]
    for fc in range(f // tf):
        cols = slice(fc * tf, (fc + 1) * tf)
        xm = xm_ref[...]
        gt = _dot(xm, wi_ref[:, cols])
        up = _dot(xm, wi_ref[:, f + fc * tf:f + (fc + 1) * tf])
        act = (_silu(gt) * up).astype(BF16)
        part = _dot(act, wo_ref[cols, :])
        if fc == 0:
            acc_ref[...] = part
        else:
            acc_ref[...] += part
    _post_norm_rows(h_ref, out_ref, acc_ref, mod_ref, g_ref, b_ref, nblk, alpha, 0.5)


def _ffn_sublayer(h, mod3, w_in, w_out, ln_g, ln_b, alpha, n_rows=None, tm=512, tf=256):
    sources = h if isinstance(h, tuple) else (h,)
    d = sources[0].shape[1]
    n = sum(s.shape[0] for s in sources) if n_rows is None else n_rows
    f = w_out.shape[0]
    assert f % tf == 0 and n % tm == 0 and all(s.shape[0] % tm == 0 for s in sources)
    nblk = tm // ROW_BLK
    kw = dict(nblk=nblk, tf=tf, alpha=alpha)
    scratch = [pltpu.VMEM((tm, d), BF16), pltpu.VMEM((tm, d), F32)]
    if len(sources) == 1:
        kern = functools.partial(_ffn_kernel, **kw)
        h_specs = [pl.BlockSpec((tm, d), lambda i: (i, 0))]
    else:
        n_first = sources[0].shape[0] // tm
        last_second = sources[1].shape[0] // tm - 1
        kern = functools.partial(_ffn_two_source_kernel, n_first=n_first, **kw)
        h_specs = [pl.BlockSpec((tm, d), lambda i: (jnp.minimum(i, n_first - 1), 0)),
                   pl.BlockSpec((tm, d), lambda i: (jnp.clip(i - n_first, 0, last_second), 0))]
        scratch.append(pltpu.VMEM((tm, d), F32))
    return pl.pallas_call(
        kern,
        grid=(n // tm,),
        in_specs=h_specs + [
            pl.BlockSpec((nblk, 3, d), lambda i: (i, 0, 0)),
            _const_spec((d, 2 * f)),
            _const_spec((f, d)),
            _const_spec((1, d)),
            _const_spec((1, d)),
        ],
        out_specs=pl.BlockSpec((tm, d), lambda i: (i, 0)),
        out_shape=jax.ShapeDtypeStruct((n, d), F32),
        scratch_shapes=scratch,
        compiler_params=_cparams(("parallel",)),
    )(*sources, mod3, w_in.astype(BF16), w_out.astype(BF16), ln_g.reshape(1, d), ln_b.reshape(1, d))


def _even_proj_kernel(h_ref, mod_ref, w_ref, ua_ref, ub_ref, xm_ref, *, nblk):
    _modulate_rows(h_ref, mod_ref, xm_ref, nblk)
    xm = xm_ref[...]
    ua_ref[...] = _dot(xm, w_ref[:, :A_PAD])
    ub_ref[...] = _dot(xm, w_ref[:, A_PAD:])


def _even_proj(h, mod3, w_pad, tm=512):
    n, d = h.shape
    nblk = tm // ROW_BLK
    kern = functools.partial(_even_proj_kernel, nblk=nblk)
    return pl.pallas_call(
        kern,
        grid=(n // tm,),
        in_specs=[
            pl.BlockSpec((tm, d), lambda i: (i, 0)),
            pl.BlockSpec((nblk, 3, d), lambda i: (i, 0, 0)),
            _const_spec((d, A_PAD + B_TOTAL)),
        ],
        out_specs=[pl.BlockSpec((tm, A_PAD), lambda i: (i, 0)),
                   pl.BlockSpec((tm, B_TOTAL), lambda i: (i, 0))],
        out_shape=[jax.ShapeDtypeStruct((n, A_PAD), F32), jax.ShapeDtypeStruct((n, B_TOTAL), F32)],
        scratch_shapes=[pltpu.VMEM((tm, d), BF16)],
        compiler_params=_cparams(("parallel",)),
    )(h, mod3, w_pad)


def _rwkv_prep_kernel(u_ref, prev_ref, next_ref, mu_ref, wl_ref, w0_ref, a0_ref, kk_ref, ka_ref, rk_ref,
                      gsum_ref, r_out, k_out, v_out, z_out, b_out, lw_out, g_out, bonus_out,
                      *, lat_total, lat_blocks, ctx_blocks):
    i = pl.program_id(0)
    is_lat = i < lat_total
    j = jnp.where(is_lat, i % lat_blocks, (i - lat_total) % ctx_blocks)
    last = jnp.where(is_lat, lat_blocks - 1, ctx_blocks - 1)
    at_start = j == 0
    at_end = j == last
    u = u_ref[...]
    prev_row = jnp.where(at_start, 0.0, prev_ref[SUBLANES - 1:SUBLANES, :])
    next_row = jnp.where(at_end, 0.0, next_ref[0:1, :])
    rid = lax.broadcasted_iota(jnp.int32, u.shape, 0)
    up = jnp.where(rid == 0, prev_row, pltpu.roll(u, 1, 0))
    dn = jnp.where(rid == ROW_BLK - 1, next_row, pltpu.roll(u, ROW_BLK - 1, 0))
    u = u + mu_ref[...] * (0.5 * (up + dn) - u)

    r = u[:, 0:RW_W]
    k = u[:, RW_W:2 * RW_W]
    v = u[:, 2 * RW_W:3 * RW_W]
    lo = u[:, RW_MAIN:A_PAD]
    lane = lax.broadcasted_iota(jnp.int32, lo.shape, 1)
    lo = jnp.where(lane < 2 * RW_DECAY_LORA, jnp.tanh(lo),
                   jnp.where(lane < 2 * RW_DECAY_LORA + RW_A_LORA, lo, jax.nn.sigmoid(lo)))
    lora = _dot(lo.astype(BF16), wl_ref[...])
    a = jax.nn.sigmoid(a0_ref[...] + lora[:, 2 * RW_W:3 * RW_W])
    g_out[...] = lora[:, 3 * RW_W:4 * RW_W]
    for dd in range(2):
        x = -(w0_ref[dd:dd + 1, :] + lora[:, dd * RW_W:(dd + 1) * RW_W])
        softplus = jnp.maximum(x, 0.0) + jnp.log(1.0 + jnp.exp(-jnp.abs(x)))
        lw_out[dd] = -jnp.exp(-softplus - 0.5)
    kk = k * kk_ref[...]
    ss = _group_sum(kk * kk, gsum_ref[...])
    kk = kk / jnp.maximum(jnp.sqrt(ss), 1e-12)
    k = k * (1.0 + (a - 1.0) * ka_ref[...])
    r_out[...] = r
    k_out[...] = k
    v_out[...] = v
    z_out[...] = -kk
    b_out[...] = kk * a
    bonus_out[...] = _group_sum(r * k * rk_ref[...], gsum_ref[...]) * v


def _rwkv_prep_chunk_kernel(u_ref, prev_ref, next_ref, mu_ref, wl_ref, w0_ref, a0_ref, kk_ref, ka_ref, rk_ref,
                            gsum_ref, rp_ref, y0_ref, mx_ref, s0_ref, pt_ref, g_out, bonus_out,
                            r_s, k_s, v_s, z_s, b_s, lw_s, **layout):
    _rwkv_prep_kernel(u_ref, prev_ref, next_ref, mu_ref, wl_ref, w0_ref, a0_ref, kk_ref, ka_ref, rk_ref,
                      gsum_ref, r_s, k_s, v_s, z_s, b_s, lw_s, g_out, bonus_out, **layout)
    _rwkv_chunk_kernel(r_s, k_s, v_s, z_s, b_s, lw_s, rp_ref, y0_ref, mx_ref, s0_ref, pt_ref)


def _rwkv_prep_chunks(ua, mu_pad, w_lora, w0, a0, k_k, k_a, r_k, gsum, batch, lat_blocks, ctx_blocks):
    n = ua.shape[0]
    nb = n // ROW_BLK
    halo = ROW_BLK // SUBLANES
    last8 = n // SUBLANES - 1
    kern = functools.partial(_rwkv_prep_chunk_kernel, lat_total=batch * lat_blocks, lat_blocks=lat_blocks,
                             ctx_blocks=ctx_blocks)
    row = lambda i: (i, 0)
    drow = lambda i: (0, i, 0)
    vec = _const_spec((1, RW_W))
    dspec = pl.BlockSpec((2, ROW_BLK, RW_W), drow)
    out512 = jax.ShapeDtypeStruct((n, RW_W), F32)
    big = jax.ShapeDtypeStruct((2, n, RW_W), F32)
    tile = pltpu.VMEM((ROW_BLK, RW_W), F32)
    return pl.pallas_call(
        kern,
        grid=(nb,),
        in_specs=[
            pl.BlockSpec((ROW_BLK, A_PAD), row),
            pl.BlockSpec((SUBLANES, A_PAD), lambda i: (jnp.maximum(i * halo - 1, 0), 0)),
            pl.BlockSpec((SUBLANES, A_PAD), lambda i: (jnp.minimum((i + 1) * halo, last8), 0)),
            _const_spec((1, A_PAD)),
            _const_spec((RW_LORA_PAD, 4 * RW_W)),
            _const_spec((2, RW_W)),
            vec, vec, vec, vec,
            _const_spec((RW_W, RW_W)),
        ],
        out_specs=[dspec] * 4 + [pl.BlockSpec((2, CPB, 1, RW_W), lambda i: (0, i, 0, 0))]
        + [pl.BlockSpec((ROW_BLK, RW_W), row)] * 2,
        out_shape=[jax.ShapeDtypeStruct((2, n, RW_W), BF16), big, big, big,
                   jax.ShapeDtypeStruct((2, n // CHUNK, 1, RW_W), F32), out512, out512],
        scratch_shapes=[tile] * 5 + [pltpu.VMEM((2, ROW_BLK, RW_W), F32)],
        compiler_params=_cparams(("parallel",)),
    )(ua, ua, ua, mu_pad, w_lora, w0, a0.reshape(1, RW_W), k_k.reshape(1, RW_W), k_a.reshape(1, RW_W),
      r_k.reshape(1, RW_W), gsum)


CPB = ROW_BLK // CHUNK


def _block_index(bi, d, g, batch, ctx_blocks, seq_blocks):
    lat_blocks = seq_blocks - ctx_blocks
    in_ctx = g < ctx_blocks
    if d == 0:
        pos_ctx, pos_lat = g, g - ctx_blocks
    else:
        pos_ctx, pos_lat = ctx_blocks - 1 - g, seq_blocks - 1 - g
    return jnp.where(in_ctx, batch * lat_blocks + bi * ctx_blocks + pos_ctx, bi * lat_blocks + pos_lat)


def _chunk_order(d):
    return range(CPB) if d == 0 else range(CPB - 1, -1, -1)


def _incl_mask(d):
    row = lax.broadcasted_iota(jnp.int32, (CHUNK, CHUNK), 0)
    col = lax.broadcasted_iota(jnp.int32, (CHUNK, CHUNK), 1)
    return (row - col) * (1 - 2 * d) >= 0


def _rwkv_chunk_kernel(r_ref, k_ref, v_ref, z_ref, b_ref, lw_ref, rp_ref, y0_ref, mx_ref, s0_ref, pt_ref):
    zeros = jnp.zeros((CHUNK, RW_HEAD), BF16)
    row = lax.broadcasted_iota(jnp.int32, (2 * CHUNK, 2 * CHUNK), 0)
    col = lax.broadcasted_iota(jnp.int32, (2 * CHUNK, 2 * CHUNK), 1)
    chains = []
    for cc in range(r_ref.shape[0] // CHUNK):
        rows = slice(cc * CHUNK, (cc + 1) * CHUNK)
        r = r_ref[rows, :]
        k = k_ref[rows, :]
        z = z_ref[rows, :]
        b = b_ref[rows, :]
        vb = v_ref[rows, :].astype(BF16)
        for d in range(2):
            order = ((row & (CHUNK - 1)) - (col & (CHUNK - 1))) * (1 - 2 * d)
            mask = order + (row >> 6) > 0
            lw = lw_ref[d, rows, :]
            cs = _cumsum_rows(_incl_mask(d).astype(BF16), lw)
            tot = jnp.sum(lw, axis=0, keepdims=True)
            p_inc = jnp.exp(cs)
            p_inv = jnp.exp(-cs)
            p_end = jnp.exp(tot - cs)
            zt = z * jnp.exp(cs - lw)
            rt = r * p_inc
            zr = jnp.concatenate([zt, rt], axis=0).astype(BF16)
            bk = jnp.concatenate([b * p_inv, k * p_inv], axis=0).astype(BF16)
            be = (b * p_end).astype(BF16)
            ke = (k * p_end).astype(BF16)
            pt_ref[d, cc] = jnp.exp(tot)
            for h in range(RW_HEADS):
                sl = slice(h * RW_HEAD, (h + 1) * RW_HEAD)
                g = jnp.where(mask, _dot_nt(zr[:, sl], bk[:, sl]), 0.0)
                chains.append(dict(d=d, rows=rows, sl=sl, vh=vb[:, sl], zt=zt[:, sl], rt=rt[:, sl],
                                   be=be[:, sl], ke=ke[:, sl], gtop=g[:CHUNK].astype(BF16),
                                   gbot=g[CHUNK:].astype(BF16), p=g[:CHUNK, :CHUNK]))
    for c in chains:
        akv = _dot(c["gtop"], jnp.concatenate([zeros, c["vh"]], axis=0))
        c["x"] = jnp.concatenate([c["zt"], akv], axis=1)
    for it in range(6):
        for c in chains:
            pb = c["p"].astype(BF16)
            if it < 5:
                both = _dot(pb, jnp.concatenate([c["x"].astype(BF16), pb], axis=1))
                c["x"] = c["x"] + both[:, :2 * RW_HEAD]
                c["p"] = both[:, 2 * RW_HEAD:]
            else:
                c["x"] = c["x"] + _dot(pb, c["x"].astype(BF16))
    for c in chains:
        c["xb"] = c["x"].astype(BF16)
        rhs = jnp.concatenate([c["xb"], jnp.concatenate([zeros, c["vh"]], axis=1)], axis=0)
        c["ry"] = _dot(c["gbot"], rhs)
    for c in chains:
        c["ms"] = _dot_tn(c["xb"], c["be"])
        c["vk"] = _dot_tn(c["vh"], c["ke"])
    for c in chains:
        d, rows, sl = c["d"], c["rows"], c["sl"]
        rp_ref[d, rows, sl] = (c["rt"] + c["ry"][:, :RW_HEAD]).astype(rp_ref.dtype)
        y0_ref[d, rows, sl] = c["ry"][:, RW_HEAD:]
        mx_ref[d, rows, sl] = c["ms"][:RW_HEAD]
        s0_ref[d, rows, sl] = c["ms"][RW_HEAD:] + c["vk"]


def _rwkv_state_phases(rp0, y00, mx0, s00, pt0, rp1, y01, mx1, s01, pt1, yf_ref, yb_ref, s_ref):
    dirs = ((rp0, y00, mx0, s00, pt0, yf_ref), (rp1, y01, mx1, s01, pt1, yb_ref))
    heads = [(d, h) for d in range(2) for h in range(RW_HEADS)]
    state = {dh: s_ref[dh[0], dh[1]] for dh in heads}
    for step in range(CPB):
        for d, h in heads:
            rp, y0, mx, s0, pt, y_ref = dirs[d]
            cc = _chunk_order(d)[step]
            rows = slice(cc * CHUNK, (cc + 1) * CHUNK)
            sl = slice(h * RW_HEAD, (h + 1) * RW_HEAD)
            s = state[d, h]
            y_ref[rows, sl] = _dot_nt(rp[rows, sl], s.astype(BF16)) + y0[rows, sl]
            state[d, h] = s * pt[cc, :, sl] + _dot_split(s, mx[rows, sl]) + s0[rows, sl]
        if step == CPB - 1:
            for d, h in heads:
                s_ref[d, h] = state[d, h]
        yield


def _hgrn_phases(q0_ref, f0_ref, i0_ref, q1_ref, f1_ref, i1_ref, lb_ref, of_ref, ob_ref, s_ref):
    lb = lb_ref[...]
    dirs = ((q0_ref, f0_ref, i0_ref, of_ref), (q1_ref, f1_ref, i1_ref, ob_ref))
    groups = []
    for d, (q_ref, f_ref, i_ref, o_ref) in enumerate(dirs):
        for cc in _chunk_order(d):
            rows = slice(cc * CHUNK, (cc + 1) * CHUNK)
            fg = lb + (1.0 - lb) * jax.nn.sigmoid(f_ref[rows, :])
            groups.append(dict(d=d, rows=rows, o_ref=o_ref, q=_silu(q_ref[rows, :]), logf=jnp.log(fg),
                               k=1.0 - fg, vb=i_ref[rows, :].astype(BF16)))
    for g in groups:
        g["cs"] = _cumsum_rows(_incl_mask(g["d"]).astype(BF16), g["logf"])
    yield
    for g in groups:
        cs = g["cs"]
        tot = jnp.sum(g["logf"], axis=0, keepdims=True)
        g["qb"] = (g["q"] * jnp.exp(cs)).astype(BF16)
        g["kb"] = (g["k"] * jnp.exp(-cs)).astype(BF16)
        g["kd"] = (g["k"] * jnp.exp(tot - cs)).astype(BF16)
        g["gl"] = jnp.exp(tot)
    heads = [slice(h * HG_DK, (h + 1) * HG_DK) for h in range(HG_HEADS)]
    for g in groups:
        incl = _incl_mask(g["d"])
        g["sc"] = [jnp.where(incl, _dot_nt(g["qb"][:, sl], g["kb"][:, sl]), 0.0).astype(BF16) for sl in heads]
    yield
    for g in groups:
        g["intra"] = [_dot(sc, g["vb"][:, sl]) for sc, sl in zip(g["sc"], heads)]
        g["kv"] = [_dot_tn(g["vb"][:, sl], g["kd"][:, sl]) for sl in heads]
    yield
    state = {(d, h): s_ref[d, h] for d in range(2) for h in range(HG_HEADS)}
    for g in groups:
        d = g["d"]
        for h, sl in enumerate(heads):
            s = state[d, h]
            g["o_ref"][g["rows"], sl] = g["intra"][h] + _dot_nt(g["qb"][:, sl], s.astype(BF16))
            state[d, h] = s * g["gl"][:, sl] + g["kv"][h]
    for (d, h), s in state.items():
        s_ref[d, h] = s


N_HG_IN = 7
N_RW_IN = 10


def _even_scan_kernel(*refs):
    hg_in = refs[:N_HG_IN]
    rw_in = refs[N_HG_IN:N_HG_IN + N_RW_IN]
    of_ref, ob_ref, yf_ref, yb_ref, hg_state, rw_state = refs[N_HG_IN + N_RW_IN:]

    @pl.when(pl.program_id(1) == 0)
    def _():
        hg_state[...] = jnp.zeros_like(hg_state)
        rw_state[...] = jnp.zeros_like(rw_state)

    streams = [_rwkv_state_phases(*rw_in, yf_ref, yb_ref, rw_state),
               _hgrn_phases(*hg_in, of_ref, ob_ref, hg_state)]
    while streams:
        streams = [s for s in streams if next(s, True) is None]


def _even_scan(ub, lb, rp, y0, mx, s0, pt, batch, ctx_blocks, seq_blocks):
    n = ub.shape[0]
    hg_in, rw_in, hg_out, rw_out = [], [], [], []
    for d in range(2):
        blk = lambda bi, g, d=d: _block_index(bi, d, g, batch, ctx_blocks, seq_blocks)
        hg_in += [
            pl.BlockSpec((ROW_BLK, HG_KW), lambda bi, g, blk=blk: (blk(bi, g), 0)),
            pl.BlockSpec((ROW_BLK, HG_KW), lambda bi, g, blk=blk, d=d: (blk(bi, g), 1 + d)),
            pl.BlockSpec((ROW_BLK, HG_KW), lambda bi, g, blk=blk: (blk(bi, g), 3)),
        ]
        big = pl.BlockSpec((None, ROW_BLK, RW_W), lambda bi, g, d=d, blk=blk: (d, blk(bi, g), 0))
        rw_in += [big] * 4 + [pl.BlockSpec((None, CPB, 1, RW_W), lambda bi, g, d=d, blk=blk: (d, blk(bi, g), 0, 0))]
        hg_out.append(pl.BlockSpec((ROW_BLK, HG_KW), lambda bi, g, blk=blk: (blk(bi, g), 0)))
        rw_out.append(pl.BlockSpec((ROW_BLK, RW_W), lambda bi, g, blk=blk: (blk(bi, g), 0)))
    out = jax.ShapeDtypeStruct((n, HG_KW), F32)
    return pl.pallas_call(
        _even_scan_kernel,
        grid=(batch, seq_blocks),
        in_specs=hg_in + [_const_spec((1, HG_KW))] + rw_in,
        out_specs=hg_out + rw_out,
        out_shape=[out] * 4,
        scratch_shapes=[pltpu.VMEM((2, HG_HEADS, HG_DK, HG_DK), F32),
                        pltpu.VMEM((2, RW_HEADS, RW_HEAD, RW_HEAD), F32)],
        compiler_params=_cparams(("parallel", "arbitrary")),
    )(ub, ub, ub, ub, ub, ub, lb.reshape(1, HG_KW), rp, y0, mx, s0, pt, rp, y0, mx, s0, pt)


def _even_out_kernel(h_ref, mod_ref, yf_ref, yb_ref, bonus_ref, grw_ref, of_ref, ob_ref, ghg_ref, gng_ref, gnb_ref,
                     hgn_ref, gm64_ref, gm128_ref, w_ref, g_ref, b_ref, out_ref, acc_ref, *, nblk, alpha):
    y = yf_ref[...] + yb_ref[...]
    mu = _group_sum(y, gm64_ref[...])
    yc = y - mu
    var = _group_sum(yc * yc, gm64_ref[...])
    ra = (yc * lax.rsqrt(var + RW_GN_EPS) * gng_ref[...] + gnb_ref[...] + bonus_ref[...]) * grw_ref[...]
    o = of_ref[...] + ob_ref[...]
    ms = _group_sum(o * o, gm128_ref[...])
    hb = o * lax.rsqrt(ms + RMS_EPS) * hgn_ref[...] * _silu(ghg_ref[...])
    acc_ref[...] = _dot(ra.astype(BF16), w_ref[:RW_W, :]) + _dot(hb.astype(BF16), w_ref[RW_W:, :])
    _post_norm_rows(h_ref, out_ref, acc_ref, mod_ref, g_ref, b_ref, nblk, alpha, 1.0)


def _even_out(h, mod3, yf, yb, bonus, g_rw, of, ob, ub, gn_g, gn_b, hg_norm, gm64, gm128, w_out, ln_g, ln_b, alpha,
              tm=256):
    n, d = h.shape
    nblk = tm // ROW_BLK
    kern = functools.partial(_even_out_kernel, nblk=nblk, alpha=alpha)
    row = lambda i: (i, 0)
    vec = _const_spec((1, RW_W))
    return pl.pallas_call(
        kern,
        grid=(n // tm,),
        in_specs=[
            pl.BlockSpec((tm, d), row),
            pl.BlockSpec((nblk, 3, d), lambda i: (i, 0, 0)),
            pl.BlockSpec((tm, RW_W), row),
            pl.BlockSpec((tm, RW_W), row),
            pl.BlockSpec((tm, RW_W), row),
            pl.BlockSpec((tm, RW_W), row),
            pl.BlockSpec((tm, HG_KW), row),
            pl.BlockSpec((tm, HG_KW), row),
            pl.BlockSpec((tm, HG_KW), lambda i: (i, 4)),
            vec, vec, vec,
            _const_spec((RW_W, RW_W)),
            _const_spec((HG_KW, HG_KW)),
            _const_spec((d, d)),
            _const_spec((1, d)),
            _const_spec((1, d)),
        ],
        out_specs=pl.BlockSpec((tm, d), row),
        out_shape=jax.ShapeDtypeStruct((n, d), F32),
        scratch_shapes=[pltpu.VMEM((tm, d), F32)],
        compiler_params=_cparams(("parallel",)),
    )(h, mod3, yf, yb, bonus, g_rw, of, ob, ub, gn_g.reshape(1, RW_W), gn_b.reshape(1, RW_W), hg_norm, gm64, gm128,
      w_out.astype(BF16), ln_g.reshape(1, d), ln_b.reshape(1, d))


def _rope(t, cos, sin_up, sin_dn):
    return t * cos + pltpu.roll(t, LANES - 16, 1) * sin_up + pltpu.roll(t, 16, 1) * sin_dn


def _odd_proj_kernel(h_ref, mod_ref, w_ref, cos_ref, sup_ref, sdn_ref, q_ref, k_ref, v_ref, xm_ref, *, nblk):
    _modulate_rows(h_ref, mod_ref, xm_ref, nblk)
    xm = xm_ref[...]
    cos, sup, sdn = cos_ref[...], sup_ref[...], sdn_ref[...]
    q = _dot(xm, w_ref[:, :DA_W])
    k = _dot(xm, w_ref[:, DA_W:2 * DA_W])
    for h in range(DA_HEADS):
        sl = slice(h * LANES, (h + 1) * LANES)
        q_ref[:, sl] = (_rope(q[:, sl], cos, sup, sdn) * Q_SCALE).astype(q_ref.dtype)
        k_ref[:, sl] = _rope(k[:, sl], cos, sup, sdn).astype(k_ref.dtype)
    v_ref[...] = _dot(xm, w_ref[:, 2 * DA_W:]).astype(v_ref.dtype)


def _odd_proj(h, mod3, w, cos, sup, sdn, tm=512):
    n, d = h.shape
    nblk = tm // ROW_BLK
    kern = functools.partial(_odd_proj_kernel, nblk=nblk)
    row = lambda i: (i, 0)
    out = jax.ShapeDtypeStruct((n, DA_W), BF16)
    return pl.pallas_call(
        kern,
        grid=(n // tm,),
        in_specs=[
            pl.BlockSpec((tm, d), row),
            pl.BlockSpec((nblk, 3, d), lambda i: (i, 0, 0)),
            _const_spec((d, 3 * DA_W)),
            pl.BlockSpec((tm, LANES), row),
            pl.BlockSpec((tm, LANES), row),
            pl.BlockSpec((tm, LANES), row),
        ],
        out_specs=[pl.BlockSpec((tm, DA_W), row)] * 3,
        out_shape=[out, out, out],
        scratch_shapes=[pltpu.VMEM((tm, d), BF16)],
        compiler_params=_cparams(("parallel",)),
    )(h, mod3, w, cos, sup, sdn)


def _diff_attn_kernel(q_ref, kl_ref, kc_ref, vl_ref, vc_ref, lam_ref, ng_ref, o_ref, k_ref, v_ref,
                      *, lam_init, bq):
    t_len = kl_ref.shape[0]

    @pl.when(pl.program_id(2) == 0)
    def _():
        k_ref[:t_len, :] = kl_ref[...]
        k_ref[t_len:, :] = kc_ref[...]
        v_ref[:t_len, :] = vl_ref[...]
        v_ref[t_len:, :] = vc_ref[...]

    lv = lam_ref[...]
    lam = (jnp.exp(jnp.sum(lv[0:1] * lv[1:2], axis=-1, keepdims=True))
           - jnp.exp(jnp.sum(lv[2:3] * lv[3:4], axis=-1, keepdims=True)) + lam_init)
    k = k_ref[...]
    halves = [slice(i * ATT_SUB, (i + 1) * ATT_SUB) for i in range(bq // ATT_SUB)]

    def scores(rows):
        q = q_ref[rows, :]
        return [_dot_nt(q[:, m * DA_HEAD:(m + 1) * DA_HEAD], k[:, m * DA_HEAD:(m + 1) * DA_HEAD])
                for m in range(2)]

    def readout(rows, s):
        es, ls = [], []
        for m in range(2):
            e = jnp.exp2(s[m] - jnp.max(s[m], axis=-1, keepdims=True))
            es.append(e)
            ls.append(jnp.sum(e, axis=-1, keepdims=True))
        a = es[0] - (lam * ls[0] / ls[1]) * es[1]
        o = _dot(a.astype(BF16), v_ref[...]) / ls[0]
        ms = jnp.mean(o * o, axis=-1, keepdims=True)
        o_ref[rows, :] = (o * lax.rsqrt(ms + RMS_EPS) * ng_ref[...] * (1.0 - lam_init)).astype(o_ref.dtype)

    s_next = scores(halves[0])
    for i, rows in enumerate(halves):
        s_cur = s_next
        if i + 1 < len(halves):
            s_next = scores(halves[i + 1])
        readout(rows, s_cur)


def _diff_attn(q, k, v, lam_vecs, norm_g, lam_init, batch, seq_blocks, ctx_blocks, bq=2 * ATT_SUB):
    t_len = (seq_blocks - ctx_blocks) * ROW_BLK
    c_len = ctx_blocks * ROW_BLK
    n_lat = batch * t_len
    assert t_len % bq == 0 and n_lat % c_len == 0
    nt = t_len // bq
    kern = functools.partial(_diff_attn_kernel, lam_init=lam_init, bq=bq)
    lat_spec = pl.BlockSpec((t_len, LANES), lambda b, h, i: (b, h))
    ctx_spec = pl.BlockSpec((c_len, LANES), lambda b, h, i: (n_lat // c_len + b, h))
    tile_spec = pl.BlockSpec((bq, LANES), lambda b, h, i: (b * nt + i, h))
    return pl.pallas_call(
        kern,
        grid=(batch, DA_HEADS, nt),
        in_specs=[tile_spec, lat_spec, ctx_spec, lat_spec, ctx_spec, _const_spec((4, DA_HEAD)),
                  _const_spec((1, LANES))],
        out_specs=tile_spec,
        out_shape=jax.ShapeDtypeStruct((n_lat, DA_W), BF16),
        scratch_shapes=[pltpu.VMEM((t_len + c_len, LANES), BF16), pltpu.VMEM((t_len + c_len, LANES), BF16)],
        compiler_params=_cparams(("parallel", "parallel", "arbitrary")),
    )(q, k, k, v, v, lam_vecs, norm_g.reshape(1, LANES))


def _odd_out_kernel(h_ref, mod_ref, x_ref, w_ref, g_ref, b_ref, out_ref, acc_ref, *, nblk, alpha):
    acc_ref[...] = _dot(x_ref[...], w_ref[...])
    _post_norm_rows(h_ref, out_ref, acc_ref, mod_ref, g_ref, b_ref, nblk, alpha, 1.0)


def _odd_out(h, mod3, x, w_out, ln_g, ln_b, alpha, tm=512):
    n, d = x.shape[0], h.shape[1]
    nblk = tm // ROW_BLK
    kern = functools.partial(_odd_out_kernel, nblk=nblk, alpha=alpha)
    row = lambda i: (i, 0)
    return pl.pallas_call(
        kern,
        grid=(n // tm,),
        in_specs=[
            pl.BlockSpec((tm, d), row),
            pl.BlockSpec((nblk, 3, d), lambda i: (i, 0, 0)),
            pl.BlockSpec((tm, DA_W), row),
            _const_spec((DA_W, d)),
            _const_spec((1, d)),
            _const_spec((1, d)),
        ],
        out_specs=pl.BlockSpec((tm, d), row),
        out_shape=jax.ShapeDtypeStruct((n, d), F32),
        scratch_shapes=[pltpu.VMEM((tm, d), F32)],
        compiler_params=_cparams(("parallel",)),
    )(h, mod3, x, w_out.astype(BF16), ln_g.reshape(1, d), ln_b.reshape(1, d))


def _rope_tables(batch, ctx_len, t):
    n_rows = t // GRID_W
    rowp = np.repeat(np.arange(n_rows), GRID_W).astype(np.float32)
    colp = np.tile(np.arange(GRID_W), n_rows).astype(np.float32)
    n_freq = DA_HEAD // 4
    inv = jnp.asarray(ROPE_BASE, F32) ** (-jnp.arange(n_freq, dtype=F32) / n_freq)
    ar = jnp.asarray(rowp)[:, None] * inv
    ac = jnp.asarray(colp)[:, None] * inv
    ang = jnp.concatenate([ar, ar, ac, ac], axis=-1)
    n_ctx = batch * ctx_len
    cos = jnp.concatenate([jnp.tile(jnp.cos(ang), (batch, 1)), jnp.ones((n_ctx, DA_HEAD), F32)], axis=0)
    sin = jnp.concatenate([jnp.tile(jnp.sin(ang), (batch, 1)), jnp.zeros((n_ctx, DA_HEAD), F32)], axis=0)
    first = (np.arange(DA_HEAD) % 32) < 16
    sup = jnp.where(first, -sin, 0.0)
    sdn = jnp.where(first, 0.0, sin)
    rep = lambda a: jnp.tile(a, (1, 2))
    return rep(cos), rep(sup), rep(sdn)


def _group_mean_matrix(width, group, scale=1.0):
    idx = np.arange(width) // group
    return jnp.asarray((idx[:, None] == idx[None, :]).astype(np.float32) * (scale / group)).astype(BF16)


def kernel(x, c, ctx, c_ctx, w_mod, b_mod, ln_g, ln_b, ffn_w_in, ffn_w_out, ev_w_in, ev_w_out,
           rw_mu, rw_w0, rw_w2, rw_a0, rw_a2, rw_g2, rw_k_k, rw_k_a, rw_r_k, rw_gn_g, rw_gn_b,
           hg_lb, hg_norm_g, od_w_in, od_w_out, da_lambda, da_norm_g):
    batch, t, d = x.shape
    ctx_len = ctx.shape[1]
    depth = w_mod.shape[0]
    assert t % ROW_BLK == 0 and ctx_len % ROW_BLK == 0 and t % GRID_W == 0
    alpha = (2 * depth) ** 0.25
    s_len = ctx_len + t
    seq_blocks = s_len // ROW_BLK
    ctx_blocks = ctx_len // ROW_BLK
    lat_blocks = seq_blocks - ctx_blocks
    n_lat = batch * t

    h = (x.reshape(n_lat, d), ctx.reshape(batch * ctx_len, d))

    rows_pad = -(-(batch + 1) // SUBLANES) * SUBLANES
    cvec = jnp.zeros((rows_pad, d), F32).at[:batch].set(c).at[batch].set(c_ctx)
    m_all = _modulation(cvec, w_mod, b_mod).reshape(depth, rows_pad, N_SUB, 3, d)
    blk_row = np.array([bi for bi in range(batch) for _ in range(lat_blocks)] + [batch] * (batch * ctx_blocks))
    m_blk = m_all[:, blk_row]

    lower_bounds = jnp.cumsum(jax.nn.softmax(hg_lb.astype(F32), axis=0), axis=0)
    gsum64 = _group_mean_matrix(RW_W, RW_HEAD, scale=RW_HEAD)
    gm64 = _group_mean_matrix(RW_W, RW_HEAD)
    gm128 = _group_mean_matrix(HG_KW, HG_DK)

    for i in range(depth):
        j = i // 2
        with_ctx_out = i < depth - 1
        ffn = functools.partial(_ffn_sublayer, alpha=alpha)
        h = ffn(h, m_blk[i, :, 0], ffn_w_in[i, 0], ffn_w_out[i, 0], ln_g[i, 0], ln_b[i, 0])
        if i % 2 == 0:
            pad_a = jnp.zeros((d, A_PAD - A_TOTAL), F32)
            w_pad = jnp.concatenate([ev_w_in[j][:, :A_TOTAL], pad_a, ev_w_in[j][:, A_TOTAL:]], axis=1).astype(BF16)
            ua, ub = _even_proj(h, m_blk[i, :, 1], w_pad)
            mu_pad = jnp.concatenate([rw_mu[j], jnp.zeros((A_PAD - A_TOTAL,), F32)]).reshape(1, A_PAD)
            w_lora = jnp.zeros((RW_LORA_PAD, 4 * RW_W), F32)
            w_lora = w_lora.at[0:64, 0:RW_W].set(rw_w2[j, 0]).at[64:128, RW_W:2 * RW_W].set(rw_w2[j, 1])
            w_lora = w_lora.at[128:192, 2 * RW_W:3 * RW_W].set(rw_a2[j])
            w_lora = w_lora.at[192:192 + RW_GATE_LORA, 3 * RW_W:].set(rw_g2[j]).astype(BF16)
            rp, y0, mx, s0, pt, g_rw, bonus = _rwkv_prep_chunks(
                ua, mu_pad, w_lora, rw_w0[j], rw_a0[j], rw_k_k[j], rw_k_a[j], rw_r_k[j].reshape(RW_W), gsum64,
                batch, lat_blocks, ctx_blocks)
            of, ob, yf, yb = _even_scan(ub, lower_bounds[i], rp, y0, mx, s0, pt, batch, ctx_blocks, seq_blocks)
            hg_norm = jnp.tile(hg_norm_g[j], HG_HEADS).reshape(1, HG_KW)
            h = _even_out(h, m_blk[i, :, 1], yf, yb, bonus, g_rw, of, ob, ub, rw_gn_g[j], rw_gn_b[j], hg_norm, gm64, gm128,
                          ev_w_out[j], ln_g[i, 1], ln_b[i, 1], alpha)
        else:
            lam_init = 0.8 - 0.6 * math.exp(-0.3 * i)
            cos, sup, sdn = _rope_tables(batch, ctx_len, t)
            q, k, v = _odd_proj(h, m_blk[i, :, 1], od_w_in[j].astype(BF16), cos, sup, sdn)
            att = _diff_attn(q, k, v, da_lambda[j], da_norm_g[j], lam_init, batch, seq_blocks, ctx_blocks)
            assert not with_ctx_out
            h = _odd_out(h, m_blk[i, :, 1], att, od_w_out[j], ln_g[i, 1], ln_b[i, 1], alpha)
        h = ffn(h, m_blk[i, :, 2], ffn_w_in[i, 1], ffn_w_out[i, 1], ln_g[i, 2], ln_b[i, 2],
                n_rows=None if with_ctx_out else n_lat)
    return h[:n_lat].reshape(batch, t, d)
```

```python
import functools
import math

import numpy as np
import jax
import jax.numpy as jnp
from jax import lax
from jax.experimental import pallas as pl
from jax.experimental.pallas import tpu as pltpu

F32 = jnp.float32
BF16 = jnp.bfloat16
HIGHEST = lax.Precision.HIGHEST

N_SUB = 3
LN_EPS = 1e-5
RMS_EPS = 1e-5
RW_HEADS = 8
RW_HEAD = 64
RW_W = RW_HEADS * RW_HEAD
RW_DECAY_LORA = 64
RW_A_LORA = 64
RW_GATE_LORA = 160
RW_GN_EPS = 64e-5
RW_MAIN = 3 * RW_W
RW_LORA = 2 * RW_DECAY_LORA + RW_A_LORA + RW_GATE_LORA
RW_LORA_PAD = 384
A_TOTAL = RW_MAIN + RW_LORA
A_PAD = RW_MAIN + RW_LORA_PAD
HG_HEADS = 4
HG_DK = 128
HG_KW = HG_HEADS * HG_DK
B_TOTAL = 5 * HG_KW
DA_HEADS = 8
DA_HEAD = 64
DA_W = DA_HEADS * 2 * DA_HEAD
GRID_W = 64
ROPE_BASE = 10000.0
Q_SCALE = DA_HEAD ** -0.5 * math.log2(math.e)

LANES = 128
SUBLANES = 8
MXU_TILE = 256
ROW_BLK = 256
CHUNK = 64
ATT_SUB = 256
VMEM_LIMIT = 56 * 1024 * 1024


def _cparams(sem):
    return pltpu.CompilerParams(dimension_semantics=sem, vmem_limit_bytes=VMEM_LIMIT)


def _const_spec(shape):
    nd = len(shape)
    return pl.BlockSpec(shape, lambda *_: (0,) * nd, pipeline_mode=pl.Buffered(1))


def _dot(a, b):
    return jnp.dot(a, b, preferred_element_type=F32)


def _dot_nt(a, b):
    return lax.dot_general(a, b, (((1,), (1,)), ((), ())), preferred_element_type=F32)


def _dot_tn(a, b):
    return lax.dot_general(a, b, (((0,), (0,)), ((), ())), preferred_element_type=F32)


def _dot_hi(a, b):
    return jnp.dot(a, b, preferred_element_type=F32, precision=HIGHEST)


def _split(x, pieces):
    out = []
    for _ in range(pieces):
        p = x.astype(BF16)
        out.append(p)
        x = x - p.astype(F32)
    return out


def _group_sum(x, gmat):
    pieces = _split(x, 2)
    slabs = [sum(_dot(p[:, c:c + MXU_TILE], gmat) for p in pieces) for c in range(0, x.shape[1], MXU_TILE)]
    return jnp.concatenate(slabs, axis=1)


def _dot_split(a, b):
    a_hi, a_lo = _split(a, 2)
    b_hi, b_lo = _split(b, 2)
    return _dot(a_hi, b_hi) + _dot(a_hi, b_lo) + _dot(a_lo, b_hi)


def _cumsum_rows(tri, x):
    return sum(_dot(tri, p) for p in _split(x, 3))


def _silu(x):
    return x * jax.nn.sigmoid(x)


def _layernorm(y, g, b):
    mu = jnp.mean(y, axis=-1, keepdims=True)
    yc = y - mu
    var = jnp.mean(yc * yc, axis=-1, keepdims=True)
    return yc * lax.rsqrt(var + LN_EPS) * g + b


def _mod_kernel(c_ref, w_ref, b_ref, o_ref):
    o_ref[...] = _dot_hi(_silu(c_ref[...]), w_ref[...]) + b_ref[...]


def _modulation(cvec, w_mod, b_mod):
    depth, d, width = w_mod.shape
    rows = cvec.shape[0]
    tn = 1024
    return pl.pallas_call(
        _mod_kernel,
        grid=(depth, width // tn),
        in_specs=[
            pl.BlockSpec((rows, d), lambda l, j: (0, 0)),
            pl.BlockSpec((None, d, tn), lambda l, j: (l, 0, j)),
            pl.BlockSpec((None, 1, tn), lambda l, j: (l, 0, j)),
        ],
        out_specs=pl.BlockSpec((None, rows, tn), lambda l, j: (l, 0, j)),
        out_shape=jax.ShapeDtypeStruct((depth, rows, width), F32),
        compiler_params=_cparams(("parallel", "parallel")),
    )(cvec, w_mod, b_mod.reshape(depth, 1, width))


def _modulate_rows(h_ref, mod_ref, xm_ref, nblk):
    for s in range(nblk):
        rows = pl.ds(s * ROW_BLK, ROW_BLK)
        x = h_ref[rows, :]
        xm_ref[rows, :] = (x * (1.0 + mod_ref[s, 1:2, :]) + mod_ref[s, 0:1, :]).astype(xm_ref.dtype)


def _post_norm_rows(h_ref, out_ref, acc_ref, mod_ref, g_ref, b_ref, nblk, alpha, weight):
    for s in range(nblk):
        rows = pl.ds(s * ROW_BLK, ROW_BLK)
        y = alpha * h_ref[rows, :] + (weight * mod_ref[s, 2:3, :]) * acc_ref[rows, :]
        out_ref[rows, :] = _layernorm(y, g_ref[...], b_ref[...])


def _ffn_two_source_kernel(ha_ref, hb_ref, *rest, n_first, **kw):
    *refs, h_sel = rest

    @pl.when(pl.program_id(0) < n_first)
    def _():
        h_sel[...] = ha_ref[...].astype(h_sel.dtype)

    @pl.when(pl.program_id(0) >= n_first)
    def _():
        h_sel[...] = hb_ref[...].astype(h_sel.dtype)

    _ffn_kernel(h_sel, *refs, **kw)


def _ffn_kernel(h_ref, mod_ref, wi_ref, wo_ref, g_ref, b_ref, out_ref, xm_ref, acc_ref, *, nblk, tf, alpha):
    _modulate_rows(h_ref, mod_ref, xm_ref, nblk)
    f = wo_ref.shape[0]
    for fc in range(f // tf):
        cols = slice(fc * tf, (fc + 1) * tf)
        xm = xm_ref[...]
        gt = _dot(xm, wi_ref[:, cols])
        up = _dot(xm, wi_ref[:, f + fc * tf:f + (fc + 1) * tf])
        act = (_silu(gt) * up).astype(BF16)
        part = _dot(act, wo_ref[cols, :])
        if fc == 0:
            acc_ref[...] = part
        else:
            acc_ref[...] += part
    _post_norm_rows(h_ref, out_ref, acc_ref, mod_ref, g_ref, b_ref, nblk, alpha, 0.5)


def _ffn_sublayer(h, mod3, w_in, w_out, ln_g, ln_b, alpha, n_rows=None, tm=512, tf=256):
    sources = h if isinstance(h, tuple) else (h,)
    d = sources[0].shape[1]
    n = sum(s.shape[0] for s in sources) if n_rows is None else n_rows
    f = w_out.shape[0]
    assert f % tf == 0 and n % tm == 0 and all(s.shape[0] % tm == 0 for s in sources)
    nblk = tm // ROW_BLK
    kw = dict(nblk=nblk, tf=tf, alpha=alpha)
    scratch = [pltpu.VMEM((tm, d), BF16), pltpu.VMEM((tm, d), F32)]
    if len(sources) == 1:
        kern = functools.partial(_ffn_kernel, **kw)
        h_specs = [pl.BlockSpec((tm, d), lambda i: (i, 0))]
    else:
        n_first = sources[0].shape[0] // tm
        last_second = sources[1].shape[0] // tm - 1
        kern = functools.partial(_ffn_two_source_kernel, n_first=n_first, **kw)
        h_specs = [pl.BlockSpec((tm, d), lambda i: (jnp.minimum(i, n_first - 1), 0)),
                   pl.BlockSpec((tm, d), lambda i: (jnp.clip(i - n_first, 0, last_second), 0))]
        scratch.append(pltpu.VMEM((tm, d), F32))
    return pl.pallas_call(
        kern,
        grid=(n // tm,),
        in_specs=h_specs + [
            pl.BlockSpec((nblk, 3, d), lambda i: (i, 0, 0)),
            _const_spec((d, 2 * f)),
            _const_spec((f, d)),
            _const_spec((1, d)),
            _const_spec((1, d)),
        ],
        out_specs=pl.BlockSpec((tm, d), lambda i: (i, 0)),
        out_shape=jax.ShapeDtypeStruct((n, d), F32),
        scratch_shapes=scratch,
        compiler_params=_cparams(("parallel",)),
    )(*sources, mod3, w_in.astype(BF16), w_out.astype(BF16), ln_g.reshape(1, d), ln_b.reshape(1, d))


def _even_proj_kernel(h_ref, mod_ref, w_ref, ua_ref, ub_ref, xm_ref, *, nblk):
    _modulate_rows(h_ref, mod_ref, xm_ref, nblk)
    xm = xm_ref[...]
    ua_ref[...] = _dot(xm, w_ref[:, :A_PAD])
    ub_ref[...] = _dot(xm, w_ref[:, A_PAD:])


def _even_proj(h, mod3, w_pad, tm=512):
    n, d = h.shape
    nblk = tm // ROW_BLK
    kern = functools.partial(_even_proj_kernel, nblk=nblk)
    return pl.pallas_call(
        kern,
        grid=(n // tm,),
        in_specs=[
            pl.BlockSpec((tm, d), lambda i: (i, 0)),
            pl.BlockSpec((nblk, 3, d), lambda i: (i, 0, 0)),
            _const_spec((d, A_PAD + B_TOTAL)),
        ],
        out_specs=[pl.BlockSpec((tm, A_PAD), lambda i: (i, 0)),
                   pl.BlockSpec((tm, B_TOTAL), lambda i: (i, 0))],
        out_shape=[jax.ShapeDtypeStruct((n, A_PAD), F32), jax.ShapeDtypeStruct((n, B_TOTAL), F32)],
        scratch_shapes=[pltpu.VMEM((tm, d), BF16)],
        compiler_params=_cparams(("parallel",)),
    )(h, mod3, w_pad)


def _rwkv_prep_kernel(u_ref, prev_ref, next_ref, mu_ref, wl_ref, w0_ref, a0_ref, kk_ref, ka_ref, rk_ref,
                      gsum_ref, r_out, k_out, v_out, z_out, b_out, lw_out, g_out, bonus_out,
                      *, lat_total, lat_blocks, ctx_blocks):
    i = pl.program_id(0)
    is_lat = i < lat_total
    j = jnp.where(is_lat, i % lat_blocks, (i - lat_total) % ctx_blocks)
    last = jnp.where(is_lat, lat_blocks - 1, ctx_blocks - 1)
    at_start = j == 0
    at_end = j == last
    u = u_ref[...]
    prev_row = jnp.where(at_start, 0.0, prev_ref[SUBLANES - 1:SUBLANES, :])
    next_row = jnp.where(at_end, 0.0, next_ref[0:1, :])
    rid = lax.broadcasted_iota(jnp.int32, u.shape, 0)
    up = jnp.where(rid == 0, prev_row, pltpu.roll(u, 1, 0))
    dn = jnp.where(rid == ROW_BLK - 1, next_row, pltpu.roll(u, ROW_BLK - 1, 0))
    u = u + mu_ref[...] * (0.5 * (up + dn) - u)

    r = u[:, 0:RW_W]
    k = u[:, RW_W:2 * RW_W]
    v = u[:, 2 * RW_W:3 * RW_W]
    lo = u[:, RW_MAIN:A_PAD]
    lane = lax.broadcasted_iota(jnp.int32, lo.shape, 1)
    lo = jnp.where(lane < 2 * RW_DECAY_LORA, jnp.tanh(lo),
                   jnp.where(lane < 2 * RW_DECAY_LORA + RW_A_LORA, lo, jax.nn.sigmoid(lo)))
    lob = lo.astype(BF16)
    lora = _dot(lob[:, :MXU_TILE], wl_ref[:MXU_TILE, :3 * RW_W])
    a = jax.nn.sigmoid(a0_ref[...] + lora[:, 2 * RW_W:3 * RW_W])
    g_out[...] = _dot(lob, wl_ref[:, 3 * RW_W:])
    for dd in range(2):
        x = -(w0_ref[dd:dd + 1, :] + lora[:, dd * RW_W:(dd + 1) * RW_W])
        softplus = jnp.maximum(x, 0.0) + jnp.log(1.0 + jnp.exp(-jnp.abs(x)))
        lw_out[dd] = -jnp.exp(-softplus - 0.5)
    kk = k * kk_ref[...]
    ss = _group_sum(kk * kk, gsum_ref[...])
    kk = kk / jnp.maximum(jnp.sqrt(ss), 1e-12)
    k = k * (1.0 + (a - 1.0) * ka_ref[...])
    r_out[...] = r
    k_out[...] = k
    v_out[...] = v
    z_out[...] = -kk
    b_out[...] = kk * a
    bonus_out[...] = _group_sum(r * k * rk_ref[...], gsum_ref[...]) * v


def _rwkv_prep_chunk_kernel(u_ref, prev_ref, next_ref, mu_ref, wl_ref, w0_ref, a0_ref, kk_ref, ka_ref, rk_ref,
                            gsum_ref, rp_ref, y0_ref, mx_ref, s0_ref, pt_ref, g_out, bonus_out,
                            r_s, k_s, v_s, z_s, b_s, lw_s, **layout):
    _rwkv_prep_kernel(u_ref, prev_ref, next_ref, mu_ref, wl_ref, w0_ref, a0_ref, kk_ref, ka_ref, rk_ref,
                      gsum_ref, r_s, k_s, v_s, z_s, b_s, lw_s, g_out, bonus_out, **layout)
    _rwkv_chunk_kernel(r_s, k_s, v_s, z_s, b_s, lw_s, rp_ref, y0_ref, mx_ref, s0_ref, pt_ref)


def _rwkv_prep_chunks(ua, mu_pad, w_lora, w0, a0, k_k, k_a, r_k, gsum, batch, lat_blocks, ctx_blocks):
    n = ua.shape[0]
    nb = n // ROW_BLK
    halo = ROW_BLK // SUBLANES
    last8 = n // SUBLANES - 1
    kern = functools.partial(_rwkv_prep_chunk_kernel, lat_total=batch * lat_blocks, lat_blocks=lat_blocks,
                             ctx_blocks=ctx_blocks)
    row = lambda i: (i, 0)
    drow = lambda i: (0, i, 0)
    vec = _const_spec((1, RW_W))
    dspec = pl.BlockSpec((2, ROW_BLK, RW_W), drow)
    out512 = jax.ShapeDtypeStruct((n, RW_W), F32)
    big = jax.ShapeDtypeStruct((2, n, RW_W), F32)
    tile = pltpu.VMEM((ROW_BLK, RW_W), F32)
    return pl.pallas_call(
        kern,
        grid=(nb,),
        in_specs=[
            pl.BlockSpec((ROW_BLK, A_PAD), row),
            pl.BlockSpec((SUBLANES, A_PAD), lambda i: (jnp.maximum(i * halo - 1, 0), 0)),
            pl.BlockSpec((SUBLANES, A_PAD), lambda i: (jnp.minimum((i + 1) * halo, last8), 0)),
            _const_spec((1, A_PAD)),
            _const_spec((RW_LORA_PAD, 4 * RW_W)),
            _const_spec((2, RW_W)),
            vec, vec, vec, vec,
            _const_spec((MXU_TILE, MXU_TILE)),
        ],
        out_specs=[dspec] * 4 + [pl.BlockSpec((2, CPB, 1, RW_W), lambda i: (0, i, 0, 0))]
        + [pl.BlockSpec((ROW_BLK, RW_W), row)] * 2,
        out_shape=[jax.ShapeDtypeStruct((2, n, RW_W), BF16), big, big, big,
                   jax.ShapeDtypeStruct((2, n // CHUNK, 1, RW_W), F32), out512, out512],
        scratch_shapes=[tile] * 5 + [pltpu.VMEM((2, ROW_BLK, RW_W), F32)],
        compiler_params=_cparams(("parallel",)),
    )(ua, ua, ua, mu_pad, w_lora, w0, a0.reshape(1, RW_W), k_k.reshape(1, RW_W), k_a.reshape(1, RW_W),
      r_k.reshape(1, RW_W), gsum)


CPB = ROW_BLK // CHUNK


def _block_index(bi, d, g, batch, ctx_blocks, seq_blocks):
    lat_blocks = seq_blocks - ctx_blocks
    in_ctx = g < ctx_blocks
    if d == 0:
        pos_ctx, pos_lat = g, g - ctx_blocks
    else:
        pos_ctx, pos_lat = ctx_blocks - 1 - g, seq_blocks - 1 - g
    return jnp.where(in_ctx, batch * lat_blocks + bi * ctx_blocks + pos_ctx, bi * lat_blocks + pos_lat)


def _chunk_order(d):
    return range(CPB) if d == 0 else range(CPB - 1, -1, -1)


def _incl_mask(d):
    row = lax.broadcasted_iota(jnp.int32, (CHUNK, CHUNK), 0)
    col = lax.broadcasted_iota(jnp.int32, (CHUNK, CHUNK), 1)
    return (row - col) * (1 - 2 * d) >= 0


def _rwkv_chunk_kernel(r_ref, k_ref, v_ref, z_ref, b_ref, lw_ref, rp_ref, y0_ref, mx_ref, s0_ref, pt_ref):
    zeros = jnp.zeros((CHUNK, RW_HEAD), BF16)
    row = lax.broadcasted_iota(jnp.int32, (2 * CHUNK, 2 * CHUNK), 0)
    col = lax.broadcasted_iota(jnp.int32, (2 * CHUNK, 2 * CHUNK), 1)
    chains = []
    for cc in range(r_ref.shape[0] // CHUNK):
        rows = slice(cc * CHUNK, (cc + 1) * CHUNK)
        r = r_ref[rows, :]
        k = k_ref[rows, :]
        z = z_ref[rows, :]
        b = b_ref[rows, :]
        vb = v_ref[rows, :].astype(BF16)
        for d in range(2):
            order = ((row & (CHUNK - 1)) - (col & (CHUNK - 1))) * (1 - 2 * d)
            mask = order + (row >> 6) > 0
            lw = lw_ref[d, rows, :]
            cs = _cumsum_rows(_incl_mask(d).astype(BF16), lw)
            tot = jnp.sum(lw, axis=0, keepdims=True)
            p_inc = jnp.exp(cs)
            p_inv = jnp.exp(-cs)
            p_end = jnp.exp(tot - cs)
            zt = z * jnp.exp(cs - lw)
            rt = r * p_inc
            zr = jnp.concatenate([zt, rt], axis=0).astype(BF16)
            bk = jnp.concatenate([b * p_inv, k * p_inv], axis=0).astype(BF16)
            be = (b * p_end).astype(BF16)
            ke = (k * p_end).astype(BF16)
            pt_ref[d, cc] = jnp.exp(tot)
            for h in range(RW_HEADS):
                sl = slice(h * RW_HEAD, (h + 1) * RW_HEAD)
                g = jnp.where(mask, _dot_nt(zr[:, sl], bk[:, sl]), 0.0)
                chains.append(dict(d=d, rows=rows, sl=sl, vh=vb[:, sl], zt=zt[:, sl], rt=rt[:, sl],
                                   be=be[:, sl], ke=ke[:, sl], gtop=g[:CHUNK].astype(BF16),
                                   gbot=g[CHUNK:].astype(BF16), p=g[:CHUNK, :CHUNK]))
    for c in chains:
        akv = _dot(c["gtop"], jnp.concatenate([zeros, c["vh"]], axis=0))
        c["x"] = jnp.concatenate([c["zt"], akv], axis=1)
    for it in range(6):
        for c in chains:
            pb = c["p"].astype(BF16)
            if it < 5:
                both = _dot(pb, jnp.concatenate([c["x"].astype(BF16), pb], axis=1))
                c["x"] = c["x"] + both[:, :2 * RW_HEAD]
                c["p"] = both[:, 2 * RW_HEAD:]
            else:
                c["x"] = c["x"] + _dot(pb, c["x"].astype(BF16))
    for c in chains:
        c["xb"] = c["x"].astype(BF16)
        rhs = jnp.concatenate([c["xb"], jnp.concatenate([zeros, c["vh"]], axis=1)], axis=0)
        c["ry"] = _dot(c["gbot"], rhs)
    for c in chains:
        c["ms"] = _dot_tn(c["xb"], c["be"])
        c["vk"] = _dot_tn(c["vh"], c["ke"])
    for c in chains:
        d, rows, sl = c["d"], c["rows"], c["sl"]
        rp_ref[d, rows, sl] = (c["rt"] + c["ry"][:, :RW_HEAD]).astype(rp_ref.dtype)
        y0_ref[d, rows, sl] = c["ry"][:, RW_HEAD:]
        mx_ref[d, rows, sl] = c["ms"][:RW_HEAD]
        s0_ref[d, rows, sl] = c["ms"][RW_HEAD:] + c["vk"]


def _rwkv_state_phases(rp0, y00, mx0, s00, pt0, rp1, y01, mx1, s01, pt1, yf_ref, yb_ref, s_ref):
    dirs = ((rp0, y00, mx0, s00, pt0, yf_ref), (rp1, y01, mx1, s01, pt1, yb_ref))
    heads = [(d, h) for d in range(2) for h in range(RW_HEADS)]
    state = {dh: s_ref[dh[0], dh[1]] for dh in heads}
    for step in range(CPB):
        for d, h in heads:
            rp, y0, mx, s0, pt, y_ref = dirs[d]
            cc = _chunk_order(d)[step]
            rows = slice(cc * CHUNK, (cc + 1) * CHUNK)
            sl = slice(h * RW_HEAD, (h + 1) * RW_HEAD)
            s = state[d, h]
            y_ref[rows, sl] = _dot_nt(rp[rows, sl], s.astype(BF16)) + y0[rows, sl]
            state[d, h] = s * pt[cc, :, sl] + _dot_split(s, mx[rows, sl]) + s0[rows, sl]
        if step == CPB - 1:
            for d, h in heads:
                s_ref[d, h] = state[d, h]
        yield


def _hgrn_phases(q0_ref, f0_ref, i0_ref, q1_ref, f1_ref, i1_ref, lb_ref, of_ref, ob_ref, s_ref):
    lb = lb_ref[...]
    dirs = ((q0_ref, f0_ref, i0_ref, of_ref), (q1_ref, f1_ref, i1_ref, ob_ref))
    groups = []
    for d, (q_ref, f_ref, i_ref, o_ref) in enumerate(dirs):
        for cc in _chunk_order(d):
            rows = slice(cc * CHUNK, (cc + 1) * CHUNK)
            fg = lb + (1.0 - lb) * jax.nn.sigmoid(f_ref[rows, :])
            groups.append(dict(d=d, rows=rows, o_ref=o_ref, q=_silu(q_ref[rows, :]), logf=jnp.log(fg),
                               k=1.0 - fg, vb=i_ref[rows, :].astype(BF16)))
    for g in groups:
        g["cs"] = _cumsum_rows(_incl_mask(g["d"]).astype(BF16), g["logf"])
    yield
    for g in groups:
        cs = g["cs"]
        tot = jnp.sum(g["logf"], axis=0, keepdims=True)
        g["qb"] = (g["q"] * jnp.exp(cs)).astype(BF16)
        g["kb"] = (g["k"] * jnp.exp(-cs)).astype(BF16)
        g["kd"] = (g["k"] * jnp.exp(tot - cs)).astype(BF16)
        g["gl"] = jnp.exp(tot)
    heads = [slice(h * HG_DK, (h + 1) * HG_DK) for h in range(HG_HEADS)]
    for g in groups:
        incl = _incl_mask(g["d"])
        g["sc"] = [jnp.where(incl, _dot_nt(g["qb"][:, sl], g["kb"][:, sl]), 0.0).astype(BF16) for sl in heads]
    yield
    for g in groups:
        g["intra"] = [_dot(sc, g["vb"][:, sl]) for sc, sl in zip(g["sc"], heads)]
        g["kv"] = [_dot_tn(g["vb"][:, sl], g["kd"][:, sl]) for sl in heads]
    yield
    state = {(d, h): s_ref[d, h] for d in range(2) for h in range(HG_HEADS)}
    for g in groups:
        d = g["d"]
        for h, sl in enumerate(heads):
            s = state[d, h]
            g["o_ref"][g["rows"], sl] = g["intra"][h] + _dot_nt(g["qb"][:, sl], s.astype(BF16))
            state[d, h] = s * g["gl"][:, sl] + g["kv"][h]
    for (d, h), s in state.items():
        s_ref[d, h] = s


N_HG_IN = 7
N_RW_IN = 10


def _even_scan_kernel(*refs):
    hg_in = refs[:N_HG_IN]
    rw_in = refs[N_HG_IN:N_HG_IN + N_RW_IN]
    of_ref, ob_ref, yf_ref, yb_ref, hg_state, rw_state = refs[N_HG_IN + N_RW_IN:]

    @pl.when(pl.program_id(1) == 0)
    def _():
        hg_state[...] = jnp.zeros_like(hg_state)
        rw_state[...] = jnp.zeros_like(rw_state)

    streams = [_rwkv_state_phases(*rw_in, yf_ref, yb_ref, rw_state),
               _hgrn_phases(*hg_in, of_ref, ob_ref, hg_state)]
    while streams:
        streams = [s for s in streams if next(s, True) is None]


def _even_scan(ub, lb, rp, y0, mx, s0, pt, batch, ctx_blocks, seq_blocks):
    n = ub.shape[0]
    hg_in, rw_in, hg_out, rw_out = [], [], [], []
    for d in range(2):
        blk = lambda bi, g, d=d: _block_index(bi, d, g, batch, ctx_blocks, seq_blocks)
        hg_in += [
            pl.BlockSpec((ROW_BLK, HG_KW), lambda bi, g, blk=blk: (blk(bi, g), 0)),
            pl.BlockSpec((ROW_BLK, HG_KW), lambda bi, g, blk=blk, d=d: (blk(bi, g), 1 + d)),
            pl.BlockSpec((ROW_BLK, HG_KW), lambda bi, g, blk=blk: (blk(bi, g), 3)),
        ]
        big = pl.BlockSpec((None, ROW_BLK, RW_W), lambda bi, g, d=d, blk=blk: (d, blk(bi, g), 0))
        rw_in += [big] * 4 + [pl.BlockSpec((None, CPB, 1, RW_W), lambda bi, g, d=d, blk=blk: (d, blk(bi, g), 0, 0))]
        hg_out.append(pl.BlockSpec((ROW_BLK, HG_KW), lambda bi, g, blk=blk: (blk(bi, g), 0)))
        rw_out.append(pl.BlockSpec((ROW_BLK, RW_W), lambda bi, g, blk=blk: (blk(bi, g), 0)))
    out = jax.ShapeDtypeStruct((n, HG_KW), F32)
    return pl.pallas_call(
        _even_scan_kernel,
        grid=(batch, seq_blocks),
        in_specs=hg_in + [_const_spec((1, HG_KW))] + rw_in,
        out_specs=hg_out + rw_out,
        out_shape=[out] * 4,
        scratch_shapes=[pltpu.VMEM((2, HG_HEADS, HG_DK, HG_DK), F32),
                        pltpu.VMEM((2, RW_HEADS, RW_HEAD, RW_HEAD), F32)],
        compiler_params=_cparams(("parallel", "arbitrary")),
    )(ub, ub, ub, ub, ub, ub, lb.reshape(1, HG_KW), rp, y0, mx, s0, pt, rp, y0, mx, s0, pt)


def _even_out_kernel(h_ref, mod_ref, yf_ref, yb_ref, bonus_ref, grw_ref, of_ref, ob_ref, ghg_ref, gng_ref, gnb_ref,
                     hgn_ref, gm64_ref, gm128_ref, w_ref, g_ref, b_ref, out_ref, acc_ref, *, nblk, alpha):
    y = yf_ref[...] + yb_ref[...]
    mu = _group_sum(y, gm64_ref[...])
    yc = y - mu
    var = _group_sum(yc * yc, gm64_ref[...])
    ra = (yc * lax.rsqrt(var + RW_GN_EPS) * gng_ref[...] + gnb_ref[...] + bonus_ref[...]) * grw_ref[...]
    o = of_ref[...] + ob_ref[...]
    ms = _group_sum(o * o, gm128_ref[...])
    hb = o * lax.rsqrt(ms + RMS_EPS) * hgn_ref[...] * _silu(ghg_ref[...])
    acc_ref[...] = _dot(ra.astype(BF16), w_ref[:RW_W, :]) + _dot(hb.astype(BF16), w_ref[RW_W:, :])
    _post_norm_rows(h_ref, out_ref, acc_ref, mod_ref, g_ref, b_ref, nblk, alpha, 1.0)


def _even_out(h, mod3, yf, yb, bonus, g_rw, of, ob, ub, gn_g, gn_b, hg_norm, gm64, gm128, w_out, ln_g, ln_b, alpha,
              tm=256):
    n, d = h.shape
    nblk = tm // ROW_BLK
    kern = functools.partial(_even_out_kernel, nblk=nblk, alpha=alpha)
    row = lambda i: (i, 0)
    vec = _const_spec((1, RW_W))
    return pl.pallas_call(
        kern,
        grid=(n // tm,),
        in_specs=[
            pl.BlockSpec((tm, d), row),
            pl.BlockSpec((nblk, 3, d), lambda i: (i, 0, 0)),
            pl.BlockSpec((tm, RW_W), row),
            pl.BlockSpec((tm, RW_W), row),
            pl.BlockSpec((tm, RW_W), row),
            pl.BlockSpec((tm, RW_W), row),
            pl.BlockSpec((tm, HG_KW), row),
            pl.BlockSpec((tm, HG_KW), row),
            pl.BlockSpec((tm, HG_KW), lambda i: (i, 4)),
            vec, vec, vec,
            _const_spec((MXU_TILE, MXU_TILE)),
            _const_spec((MXU_TILE, MXU_TILE)),
            _const_spec((d, d)),
            _const_spec((1, d)),
            _const_spec((1, d)),
        ],
        out_specs=pl.BlockSpec((tm, d), row),
        out_shape=jax.ShapeDtypeStruct((n, d), F32),
        scratch_shapes=[pltpu.VMEM((tm, d), F32)],
        compiler_params=_cparams(("parallel",)),
    )(h, mod3, yf, yb, bonus, g_rw, of, ob, ub, gn_g.reshape(1, RW_W), gn_b.reshape(1, RW_W), hg_norm, gm64, gm128,
      w_out.astype(BF16), ln_g.reshape(1, d), ln_b.reshape(1, d))


def _rope(t, cos, sin_up, sin_dn):
    return t * cos + pltpu.roll(t, LANES - 16, 1) * sin_up + pltpu.roll(t, 16, 1) * sin_dn


def _odd_proj_kernel(h_ref, mod_ref, w_ref, cos_ref, sup_ref, sdn_ref, q_ref, k_ref, v_ref, xm_ref, *, nblk):
    _modulate_rows(h_ref, mod_ref, xm_ref, nblk)
    xm = xm_ref[...]
    cos, sup, sdn = cos_ref[...], sup_ref[...], sdn_ref[...]
    q = _dot(xm, w_ref[:, :DA_W])
    k = _dot(xm, w_ref[:, DA_W:2 * DA_W])
    for h in range(DA_HEADS):
        sl = slice(h * LANES, (h + 1) * LANES)
        q_ref[:, sl] = (_rope(q[:, sl], cos, sup, sdn) * Q_SCALE).astype(q_ref.dtype)
        k_ref[:, sl] = _rope(k[:, sl], cos, sup, sdn).astype(k_ref.dtype)
    v_ref[...] = _dot(xm, w_ref[:, 2 * DA_W:]).astype(v_ref.dtype)


def _odd_proj(h, mod3, w, cos, sup, sdn, tm=512):
    n, d = h.shape
    nblk = tm // ROW_BLK
    kern = functools.partial(_odd_proj_kernel, nblk=nblk)
    row = lambda i: (i, 0)
    out = jax.ShapeDtypeStruct((n, DA_W), BF16)
    return pl.pallas_call(
        kern,
        grid=(n // tm,),
        in_specs=[
            pl.BlockSpec((tm, d), row),
            pl.BlockSpec((nblk, 3, d), lambda i: (i, 0, 0)),
            _const_spec((d, 3 * DA_W)),
            pl.BlockSpec((tm, LANES), row),
            pl.BlockSpec((tm, LANES), row),
            pl.BlockSpec((tm, LANES), row),
        ],
        out_specs=[pl.BlockSpec((tm, DA_W), row)] * 3,
        out_shape=[out, out, out],
        scratch_shapes=[pltpu.VMEM((tm, d), BF16)],
        compiler_params=_cparams(("parallel",)),
    )(h, mod3, w, cos, sup, sdn)


def _diff_attn_kernel(q_ref, kl_ref, kc_ref, vl_ref, vc_ref, lam_ref, ng_ref, o_ref, k_ref, v_ref,
                      *, lam_init, bq):
    t_len = kl_ref.shape[0]

    @pl.when(pl.program_id(2) == 0)
    def _():
        k_ref[:t_len, :] = kl_ref[...]
        k_ref[t_len:, :] = kc_ref[...]
        v_ref[:t_len, :] = vl_ref[...]
        v_ref[t_len:, :] = vc_ref[...]

    lv = lam_ref[...]
    lam = (jnp.exp(jnp.sum(lv[0:1] * lv[1:2], axis=-1, keepdims=True))
           - jnp.exp(jnp.sum(lv[2:3] * lv[3:4], axis=-1, keepdims=True)) + lam_init)
    k = k_ref[...]
    halves = [slice(i * ATT_SUB, (i + 1) * ATT_SUB) for i in range(bq // ATT_SUB)]

    def scores(rows):
        q = q_ref[rows, :]
        return [_dot_nt(q[:, m * DA_HEAD:(m + 1) * DA_HEAD], k[:, m * DA_HEAD:(m + 1) * DA_HEAD])
                for m in range(2)]

    def readout(rows, s):
        es, ls = [], []
        for m in range(2):
            e = jnp.exp2(s[m] - jnp.max(s[m], axis=-1, keepdims=True))
            es.append(e)
            ls.append(jnp.sum(e, axis=-1, keepdims=True))
        a = es[0] - (lam * ls[0] / ls[1]) * es[1]
        o = _dot(a.astype(BF16), v_ref[...]) / ls[0]
        ms = jnp.mean(o * o, axis=-1, keepdims=True)
        o_ref[rows, :] = (o * lax.rsqrt(ms + RMS_EPS) * ng_ref[...] * (1.0 - lam_init)).astype(o_ref.dtype)

    s_next = scores(halves[0])
    for i, rows in enumerate(halves):
        s_cur = s_next
        if i + 1 < len(halves):
            s_next = scores(halves[i + 1])
        readout(rows, s_cur)


def _diff_attn(q, k, v, lam_vecs, norm_g, lam_init, batch, seq_blocks, ctx_blocks, bq=2 * ATT_SUB):
    t_len = (seq_blocks - ctx_blocks) * ROW_BLK
    c_len = ctx_blocks * ROW_BLK
    n_lat = batch * t_len
    assert t_len % bq == 0 and n_lat % c_len == 0
    nt = t_len // bq
    kern = functools.partial(_diff_attn_kernel, lam_init=lam_init, bq=bq)
    lat_spec = pl.BlockSpec((t_len, LANES), lambda b, h, i: (b, h))
    ctx_spec = pl.BlockSpec((c_len, LANES), lambda b, h, i: (n_lat // c_len + b, h))
    tile_spec = pl.BlockSpec((bq, LANES), lambda b, h, i: (b * nt + i, h))
    return pl.pallas_call(
        kern,
        grid=(batch, DA_HEADS, nt),
        in_specs=[tile_spec, lat_spec, ctx_spec, lat_spec, ctx_spec, _const_spec((4, DA_HEAD)),
                  _const_spec((1, LANES))],
        out_specs=tile_spec,
        out_shape=jax.ShapeDtypeStruct((n_lat, DA_W), BF16),
        scratch_shapes=[pltpu.VMEM((t_len + c_len, LANES), BF16), pltpu.VMEM((t_len + c_len, LANES), BF16)],
        compiler_params=_cparams(("parallel", "parallel", "arbitrary")),
    )(q, k, k, v, v, lam_vecs, norm_g.reshape(1, LANES))


def _odd_out_kernel(h_ref, mod_ref, x_ref, w_ref, g_ref, b_ref, out_ref, acc_ref, *, nblk, alpha):
    acc_ref[...] = _dot(x_ref[...], w_ref[...])
    _post_norm_rows(h_ref, out_ref, acc_ref, mod_ref, g_ref, b_ref, nblk, alpha, 1.0)


def _odd_out(h, mod3, x, w_out, ln_g, ln_b, alpha, tm=512):
    n, d = x.shape[0], h.shape[1]
    nblk = tm // ROW_BLK
    kern = functools.partial(_odd_out_kernel, nblk=nblk, alpha=alpha)
    row = lambda i: (i, 0)
    return pl.pallas_call(
        kern,
        grid=(n // tm,),
        in_specs=[
            pl.BlockSpec((tm, d), row),
            pl.BlockSpec((nblk, 3, d), lambda i: (i, 0, 0)),
            pl.BlockSpec((tm, DA_W), row),
            _const_spec((DA_W, d)),
            _const_spec((1, d)),
            _const_spec((1, d)),
        ],
        out_specs=pl.BlockSpec((tm, d), row),
        out_shape=jax.ShapeDtypeStruct((n, d), F32),
        scratch_shapes=[pltpu.VMEM((tm, d), F32)],
        compiler_params=_cparams(("parallel",)),
    )(h, mod3, x, w_out.astype(BF16), ln_g.reshape(1, d), ln_b.reshape(1, d))


def _rope_tables(batch, ctx_len, t):
    n_rows = t // GRID_W
    rowp = np.repeat(np.arange(n_rows), GRID_W).astype(np.float32)
    colp = np.tile(np.arange(GRID_W), n_rows).astype(np.float32)
    n_freq = DA_HEAD // 4
    inv = jnp.asarray(ROPE_BASE, F32) ** (-jnp.arange(n_freq, dtype=F32) / n_freq)
    ar = jnp.asarray(rowp)[:, None] * inv
    ac = jnp.asarray(colp)[:, None] * inv
    ang = jnp.concatenate([ar, ar, ac, ac], axis=-1)
    n_ctx = batch * ctx_len
    cos = jnp.concatenate([jnp.tile(jnp.cos(ang), (batch, 1)), jnp.ones((n_ctx, DA_HEAD), F32)], axis=0)
    sin = jnp.concatenate([jnp.tile(jnp.sin(ang), (batch, 1)), jnp.zeros((n_ctx, DA_HEAD), F32)], axis=0)
    first = (np.arange(DA_HEAD) % 32) < 16
    sup = jnp.where(first, -sin, 0.0)
    sdn = jnp.where(first, 0.0, sin)
    rep = lambda a: jnp.tile(a, (1, 2))
    return rep(cos), rep(sup), rep(sdn)


def _group_mean_matrix(group, scale=1.0):
    assert MXU_TILE % group == 0
    idx = np.arange(MXU_TILE) // group
    return jnp.asarray((idx[:, None] == idx[None, :]).astype(np.float32) * (scale / group)).astype(BF16)


def kernel(x, c, ctx, c_ctx, w_mod, b_mod, ln_g, ln_b, ffn_w_in, ffn_w_out, ev_w_in, ev_w_out,
           rw_mu, rw_w0, rw_w2, rw_a0, rw_a2, rw_g2, rw_k_k, rw_k_a, rw_r_k, rw_gn_g, rw_gn_b,
           hg_lb, hg_norm_g, od_w_in, od_w_out, da_lambda, da_norm_g):
    batch, t, d = x.shape
    ctx_len = ctx.shape[1]
    depth = w_mod.shape[0]
    assert t % ROW_BLK == 0 and ctx_len % ROW_BLK == 0 and t % GRID_W == 0
    alpha = (2 * depth) ** 0.25
    s_len = ctx_len + t
    seq_blocks = s_len // ROW_BLK
    ctx_blocks = ctx_len // ROW_BLK
    lat_blocks = seq_blocks - ctx_blocks
    n_lat = batch * t

    h = (x.reshape(n_lat, d), ctx.reshape(batch * ctx_len, d))

    rows_pad = -(-(batch + 1) // SUBLANES) * SUBLANES
    cvec = jnp.zeros((rows_pad, d), F32).at[:batch].set(c).at[batch].set(c_ctx)
    m_all = _modulation(cvec, w_mod, b_mod).reshape(depth, rows_pad, N_SUB, 3, d)
    blk_row = np.array([bi for bi in range(batch) for _ in range(lat_blocks)] + [batch] * (batch * ctx_blocks))
    m_blk = m_all[:, blk_row]

    lower_bounds = jnp.cumsum(jax.nn.softmax(hg_lb.astype(F32), axis=0), axis=0)
    gsum64 = _group_mean_matrix(RW_HEAD, scale=RW_HEAD)
    gm64 = _group_mean_matrix(RW_HEAD)
    gm128 = _group_mean_matrix(HG_DK)

    for i in range(depth):
        j = i // 2
        with_ctx_out = i < depth - 1
        ffn = functools.partial(_ffn_sublayer, alpha=alpha)
        h = ffn(h, m_blk[i, :, 0], ffn_w_in[i, 0], ffn_w_out[i, 0], ln_g[i, 0], ln_b[i, 0])
        if i % 2 == 0:
            pad_a = jnp.zeros((d, A_PAD - A_TOTAL), F32)
            w_pad = jnp.concatenate([ev_w_in[j][:, :A_TOTAL], pad_a, ev_w_in[j][:, A_TOTAL:]], axis=1).astype(BF16)
            ua, ub = _even_proj(h, m_blk[i, :, 1], w_pad)
            mu_pad = jnp.concatenate([rw_mu[j], jnp.zeros((A_PAD - A_TOTAL,), F32)]).reshape(1, A_PAD)
            w_lora = jnp.zeros((RW_LORA_PAD, 4 * RW_W), F32)
            w_lora = w_lora.at[0:64, 0:RW_W].set(rw_w2[j, 0]).at[64:128, RW_W:2 * RW_W].set(rw_w2[j, 1])
            w_lora = w_lora.at[128:192, 2 * RW_W:3 * RW_W].set(rw_a2[j])
            w_lora = w_lora.at[192:192 + RW_GATE_LORA, 3 * RW_W:].set(rw_g2[j]).astype(BF16)
            rp, y0, mx, s0, pt, g_rw, bonus = _rwkv_prep_chunks(
                ua, mu_pad, w_lora, rw_w0[j], rw_a0[j], rw_k_k[j], rw_k_a[j], rw_r_k[j].reshape(RW_W), gsum64,
                batch, lat_blocks, ctx_blocks)
            of, ob, yf, yb = _even_scan(ub, lower_bounds[i], rp, y0, mx, s0, pt, batch, ctx_blocks, seq_blocks)
            hg_norm = jnp.tile(hg_norm_g[j], HG_HEADS).reshape(1, HG_KW)
            h = _even_out(h, m_blk[i, :, 1], yf, yb, bonus, g_rw, of, ob, ub, rw_gn_g[j], rw_gn_b[j], hg_norm, gm64, gm128,
                          ev_w_out[j], ln_g[i, 1], ln_b[i, 1], alpha)
        else:
            lam_init = 0.8 - 0.6 * math.exp(-0.3 * i)
            cos, sup, sdn = _rope_tables(batch, ctx_len, t)
            q, k, v = _odd_proj(h, m_blk[i, :, 1], od_w_in[j].astype(BF16), cos, sup, sdn)
            att = _diff_attn(q, k, v, da_lambda[j], da_norm_g[j], lam_init, batch, seq_blocks, ctx_blocks)
            assert not with_ctx_out
            h = _odd_out(h, m_blk[i, :, 1], att, od_w_out[j], ln_g[i, 1], ln_b[i, 1], alpha)
        h = ffn(h, m_blk[i, :, 2], ffn_w_in[i, 1], ffn_w_out[i, 1], ln_g[i, 2], ln_b[i, 2],
                n_rows=None if with_ctx_out else n_lat)
    return h[:n_lat].reshape(batch, t, d)
```

```python
import functools
import math

import numpy as np
import jax
import jax.numpy as jnp
from jax import lax
from jax.experimental import pallas as pl
from jax.experimental.pallas import tpu as pltpu

F32 = jnp.float32
BF16 = jnp.bfloat16
HIGHEST = lax.Precision.HIGHEST

N_SUB = 3
LN_EPS = 1e-5
RMS_EPS = 1e-5
RW_HEADS = 8
RW_HEAD = 64
RW_W = RW_HEADS * RW_HEAD
RW_DECAY_LORA = 64
RW_A_LORA = 64
RW_GATE_LORA = 160
RW_GN_EPS = 64e-5
RW_MAIN = 3 * RW_W
RW_LORA = 2 * RW_DECAY_LORA + RW_A_LORA + RW_GATE_LORA
RW_LORA_PAD = 384
A_TOTAL = RW_MAIN + RW_LORA
A_PAD = RW_MAIN + RW_LORA_PAD
HG_HEADS = 4
HG_DK = 128
HG_KW = HG_HEADS * HG_DK
B_TOTAL = 5 * HG_KW
DA_HEADS = 8
DA_HEAD = 64
DA_W = DA_HEADS * 2 * DA_HEAD
GRID_W = 64
ROPE_BASE = 10000.0
Q_SCALE = DA_HEAD ** -0.5 * math.log2(math.e)

LANES = 128
SUBLANES = 8
MXU_TILE = 256
ROW_BLK = 256
CHUNK = 64
ATT_SUB = 256
VMEM_LIMIT = 56 * 1024 * 1024


def _cparams(sem):
    return pltpu.CompilerParams(dimension_semantics=sem, vmem_limit_bytes=VMEM_LIMIT)


def _const_spec(shape):
    nd = len(shape)
    return pl.BlockSpec(shape, lambda *_: (0,) * nd, pipeline_mode=pl.Buffered(1))


def _dot(a, b):
    return jnp.dot(a, b, preferred_element_type=F32)


def _dot_nt(a, b):
    return lax.dot_general(a, b, (((1,), (1,)), ((), ())), preferred_element_type=F32)


def _dot_hi(a, b):
    return jnp.dot(a, b, preferred_element_type=F32, precision=HIGHEST)


def _split(x, pieces):
    out = []
    for _ in range(pieces):
        p = x.astype(BF16)
        out.append(p)
        x = x - p.astype(F32)
    return out


def _group_sum(x, gmat):
    pieces = _split(x, 2)
    slabs = [sum(_dot(p[:, c:c + MXU_TILE], gmat) for p in pieces) for c in range(0, x.shape[1], MXU_TILE)]
    return jnp.concatenate(slabs, axis=1)


def _dot_split(a, b):
    a_hi, a_lo = _split(a, 2)
    b_hi, b_lo = _split(b, 2)
    return _dot(a_hi, b_hi) + _dot(a_hi, b_lo) + _dot(a_lo, b_hi)


def _cumsum_rows(tri, x):
    return sum(_dot(tri, p) for p in _split(x, 3))


def _silu(x):
    return x * jax.nn.sigmoid(x)


def _layernorm(y, g, b):
    mu = jnp.mean(y, axis=-1, keepdims=True)
    yc = y - mu
    var = jnp.mean(yc * yc, axis=-1, keepdims=True)
    return yc * lax.rsqrt(var + LN_EPS) * g + b


def _mod_kernel(c_ref, w_ref, b_ref, o_ref):
    o_ref[...] = _dot_hi(_silu(c_ref[...]), w_ref[...]) + b_ref[...]


def _modulation(cvec, w_mod, b_mod):
    depth, d, width = w_mod.shape
    rows = cvec.shape[0]
    tn = 1024
    return pl.pallas_call(
        _mod_kernel,
        grid=(depth, width // tn),
        in_specs=[
            pl.BlockSpec((rows, d), lambda l, j: (0, 0)),
            pl.BlockSpec((None, d, tn), lambda l, j: (l, 0, j)),
            pl.BlockSpec((None, 1, tn), lambda l, j: (l, 0, j)),
        ],
        out_specs=pl.BlockSpec((None, rows, tn), lambda l, j: (l, 0, j)),
        out_shape=jax.ShapeDtypeStruct((depth, rows, width), F32),
        compiler_params=_cparams(("parallel", "parallel")),
    )(cvec, w_mod, b_mod.reshape(depth, 1, width))


def _modulate_rows(h_ref, mod_ref, xm_ref, nblk):
    for s in range(nblk):
        rows = pl.ds(s * ROW_BLK, ROW_BLK)
        x = h_ref[rows, :]
        xm_ref[rows, :] = (x * (1.0 + mod_ref[s, 1:2, :]) + mod_ref[s, 0:1, :]).astype(xm_ref.dtype)


def _post_norm_rows(h_ref, out_ref, acc_ref, mod_ref, g_ref, b_ref, nblk, alpha, weight):
    for s in range(nblk):
        rows = pl.ds(s * ROW_BLK, ROW_BLK)
        y = alpha * h_ref[rows, :] + (weight * mod_ref[s, 2:3, :]) * acc_ref[rows, :]
        out_ref[rows, :] = _layernorm(y, g_ref[...], b_ref[...])


def _ffn_two_source_kernel(ha_ref, hb_ref, *rest, n_first, **kw):
    *refs, h_sel = rest

    @pl.when(pl.program_id(0) < n_first)
    def _():
        h_sel[...] = ha_ref[...].astype(h_sel.dtype)

    @pl.when(pl.program_id(0) >= n_first)
    def _():
        h_sel[...] = hb_ref[...].astype(h_sel.dtype)

    _ffn_kernel(h_sel, *refs, **kw)


def _ffn_kernel(h_ref, mod_ref, wi_ref, wo_ref, g_ref, b_ref, out_ref, xm_ref, acc_ref, *, nblk, tf, alpha):
    _modulate_rows(h_ref, mod_ref, xm_ref, nblk)
    f = wo_ref.shape[0]
    for fc in range(f // tf):
        cols = slice(fc * tf, (fc + 1) * tf)
        xm = xm_ref[...]
        gt = _dot(xm, wi_ref[:, cols])
        up = _dot(xm, wi_ref[:, f + fc * tf:f + (fc + 1) * tf])
        act = (_silu(gt) * up).astype(BF16)
        part = _dot(act, wo_ref[cols, :])
        if fc == 0:
            acc_ref[...] = part
        else:
            acc_ref[...] += part
    _post_norm_rows(h_ref, out_ref, acc_ref, mod_ref, g_ref, b_ref, nblk, alpha, 0.5)


def _ffn_sublayer(h, mod3, w_in, w_out, ln_g, ln_b, alpha, n_rows=None, tm=512, tf=256):
    sources = h if isinstance(h, tuple) else (h,)
    d = sources[0].shape[1]
    n = sum(s.shape[0] for s in sources) if n_rows is None else n_rows
    f = w_out.shape[0]
    assert f % tf == 0 and n % tm == 0 and all(s.shape[0] % tm == 0 for s in sources)
    nblk = tm // ROW_BLK
    kw = dict(nblk=nblk, tf=tf, alpha=alpha)
    scratch = [pltpu.VMEM((tm, d), BF16), pltpu.VMEM((tm, d), F32)]
    if len(sources) == 1:
        kern = functools.partial(_ffn_kernel, **kw)
        h_specs = [pl.BlockSpec((tm, d), lambda i: (i, 0))]
    else:
        n_first = sources[0].shape[0] // tm
        last_second = sources[1].shape[0] // tm - 1
        kern = functools.partial(_ffn_two_source_kernel, n_first=n_first, **kw)
        h_specs = [pl.BlockSpec((tm, d), lambda i: (jnp.minimum(i, n_first - 1), 0)),
                   pl.BlockSpec((tm, d), lambda i: (jnp.clip(i - n_first, 0, last_second), 0))]
        scratch.append(pltpu.VMEM((tm, d), F32))
    return pl.pallas_call(
        kern,
        grid=(n // tm,),
        in_specs=h_specs + [
            pl.BlockSpec((nblk, 3, d), lambda i: (i, 0, 0)),
            _const_spec((d, 2 * f)),
            _const_spec((f, d)),
            _const_spec((1, d)),
            _const_spec((1, d)),
        ],
        out_specs=pl.BlockSpec((tm, d), lambda i: (i, 0)),
        out_shape=jax.ShapeDtypeStruct((n, d), F32),
        scratch_shapes=scratch,
        compiler_params=_cparams(("parallel",)),
    )(*sources, mod3, w_in.astype(BF16), w_out.astype(BF16), ln_g.reshape(1, d), ln_b.reshape(1, d))


def _even_proj_kernel(h_ref, mod_ref, w_ref, ua_ref, ub_ref, xm_ref, *, nblk):
    _modulate_rows(h_ref, mod_ref, xm_ref, nblk)
    xm = xm_ref[...]
    ua_ref[...] = _dot(xm, w_ref[:, :A_PAD])
    ub_ref[...] = _dot(xm, w_ref[:, A_PAD:])


def _even_proj(h, mod3, w_pad, tm=512):
    n, d = h.shape
    nblk = tm // ROW_BLK
    kern = functools.partial(_even_proj_kernel, nblk=nblk)
    return pl.pallas_call(
        kern,
        grid=(n // tm,),
        in_specs=[
            pl.BlockSpec((tm, d), lambda i: (i, 0)),
            pl.BlockSpec((nblk, 3, d), lambda i: (i, 0, 0)),
            _const_spec((d, A_PAD + B_TOTAL)),
        ],
        out_specs=[pl.BlockSpec((tm, A_PAD), lambda i: (i, 0)),
                   pl.BlockSpec((tm, B_TOTAL), lambda i: (i, 0))],
        out_shape=[jax.ShapeDtypeStruct((n, A_PAD), F32), jax.ShapeDtypeStruct((n, B_TOTAL), F32)],
        scratch_shapes=[pltpu.VMEM((tm, d), BF16)],
        compiler_params=_cparams(("parallel",)),
    )(h, mod3, w_pad)


def _rwkv_prep_kernel(u_ref, prev_ref, next_ref, mu_ref, wl_ref, w0_ref, a0_ref, kk_ref, ka_ref, rk_ref,
                      gsum_ref, r_out, k_out, v_out, z_out, b_out, lw_out, g_out, bonus_out,
                      *, lat_total, lat_blocks, ctx_blocks):
    i = pl.program_id(0)
    is_lat = i < lat_total
    j = jnp.where(is_lat, i % lat_blocks, (i - lat_total) % ctx_blocks)
    last = jnp.where(is_lat, lat_blocks - 1, ctx_blocks - 1)
    at_start = j == 0
    at_end = j == last
    u = u_ref[...]
    prev_row = jnp.where(at_start, 0.0, prev_ref[SUBLANES - 1:SUBLANES, :])
    next_row = jnp.where(at_end, 0.0, next_ref[0:1, :])
    rid = lax.broadcasted_iota(jnp.int32, u.shape, 0)
    up = jnp.where(rid == 0, prev_row, pltpu.roll(u, 1, 0))
    dn = jnp.where(rid == ROW_BLK - 1, next_row, pltpu.roll(u, ROW_BLK - 1, 0))
    u = u + mu_ref[...] * (0.5 * (up + dn) - u)

    r = u[:, 0:RW_W]
    k = u[:, RW_W:2 * RW_W]
    v = u[:, 2 * RW_W:3 * RW_W]
    lo = u[:, RW_MAIN:A_PAD]
    lane = lax.broadcasted_iota(jnp.int32, lo.shape, 1)
    lo = jnp.where(lane < 2 * RW_DECAY_LORA, jnp.tanh(lo),
                   jnp.where(lane < 2 * RW_DECAY_LORA + RW_A_LORA, lo, jax.nn.sigmoid(lo)))
    lob = lo.astype(BF16)
    lora = _dot(lob[:, :MXU_TILE], wl_ref[:MXU_TILE, :3 * RW_W])
    a = jax.nn.sigmoid(a0_ref[...] + lora[:, 2 * RW_W:3 * RW_W])
    g_out[...] = _dot(lob, wl_ref[:, 3 * RW_W:])
    for dd in range(2):
        x = -(w0_ref[dd:dd + 1, :] + lora[:, dd * RW_W:(dd + 1) * RW_W])
        softplus = jnp.maximum(x, 0.0) + jnp.log(1.0 + jnp.exp(-jnp.abs(x)))
        lw_out[dd] = -jnp.exp(-softplus - 0.5)
    kk = k * kk_ref[...]
    ss = _group_sum(kk * kk, gsum_ref[...])
    kk = kk / jnp.maximum(jnp.sqrt(ss), 1e-12)
    k = k * (1.0 + (a - 1.0) * ka_ref[...])
    r_out[...] = r
    k_out[...] = k
    v_out[...] = v
    z_out[...] = -kk
    b_out[...] = kk * a
    bonus_out[...] = _group_sum(r * k * rk_ref[...], gsum_ref[...]) * v


def _rwkv_prep_chunk_kernel(u_ref, prev_ref, next_ref, mu_ref, wl_ref, w0_ref, a0_ref, kk_ref, ka_ref, rk_ref,
                            gsum_ref, rp_ref, y0_ref, mx_ref, s0_ref, pt_ref, g_out, bonus_out,
                            r_s, k_s, v_s, z_s, b_s, lw_s, **layout):
    _rwkv_prep_kernel(u_ref, prev_ref, next_ref, mu_ref, wl_ref, w0_ref, a0_ref, kk_ref, ka_ref, rk_ref,
                      gsum_ref, r_s, k_s, v_s, z_s, b_s, lw_s, g_out, bonus_out, **layout)
    _rwkv_chunk_kernel(r_s, k_s, v_s, z_s, b_s, lw_s, rp_ref, y0_ref, mx_ref, s0_ref, pt_ref)


def _rwkv_prep_chunks(ua, mu_pad, w_lora, w0, a0, k_k, k_a, r_k, gsum, batch, lat_blocks, ctx_blocks):
    n = ua.shape[0]
    nb = n // ROW_BLK
    halo = ROW_BLK // SUBLANES
    last8 = n // SUBLANES - 1
    kern = functools.partial(_rwkv_prep_chunk_kernel, lat_total=batch * lat_blocks, lat_blocks=lat_blocks,
                             ctx_blocks=ctx_blocks)
    row = lambda i: (i, 0)
    drow = lambda i: (0, i, 0)
    vec = _const_spec((1, RW_W))
    dspec = pl.BlockSpec((2, ROW_BLK, RW_W), drow)
    out512 = jax.ShapeDtypeStruct((n, RW_W), F32)
    big = jax.ShapeDtypeStruct((2, n, RW_W), F32)
    tile = pltpu.VMEM((ROW_BLK, RW_W), F32)
    return pl.pallas_call(
        kern,
        grid=(nb,),
        in_specs=[
            pl.BlockSpec((ROW_BLK, A_PAD), row),
            pl.BlockSpec((SUBLANES, A_PAD), lambda i: (jnp.maximum(i * halo - 1, 0), 0)),
            pl.BlockSpec((SUBLANES, A_PAD), lambda i: (jnp.minimum((i + 1) * halo, last8), 0)),
            _const_spec((1, A_PAD)),
            _const_spec((RW_LORA_PAD, 4 * RW_W)),
            _const_spec((2, RW_W)),
            vec, vec, vec, vec,
            _const_spec((MXU_TILE, MXU_TILE)),
        ],
        out_specs=[dspec] * 4 + [pl.BlockSpec((2, CPB, 1, RW_W), lambda i: (0, i, 0, 0))]
        + [pl.BlockSpec((ROW_BLK, RW_W), row)] * 2,
        out_shape=[jax.ShapeDtypeStruct((2, n, RW_W), BF16), big, big, big,
                   jax.ShapeDtypeStruct((2, n // CHUNK, 1, RW_W), F32), out512, out512],
        scratch_shapes=[tile] * 5 + [pltpu.VMEM((2, ROW_BLK, RW_W), F32)],
        compiler_params=_cparams(("parallel",)),
    )(ua, ua, ua, mu_pad, w_lora, w0, a0.reshape(1, RW_W), k_k.reshape(1, RW_W), k_a.reshape(1, RW_W),
      r_k.reshape(1, RW_W), gsum)


CPB = ROW_BLK // CHUNK


def _block_index(bi, d, g, batch, ctx_blocks, seq_blocks):
    lat_blocks = seq_blocks - ctx_blocks
    in_ctx = g < ctx_blocks
    if d == 0:
        pos_ctx, pos_lat = g, g - ctx_blocks
    else:
        pos_ctx, pos_lat = ctx_blocks - 1 - g, seq_blocks - 1 - g
    return jnp.where(in_ctx, batch * lat_blocks + bi * ctx_blocks + pos_ctx, bi * lat_blocks + pos_lat)


def _chunk_order(d):
    return range(CPB) if d == 0 else range(CPB - 1, -1, -1)


def _incl_mask(d):
    row = lax.broadcasted_iota(jnp.int32, (CHUNK, CHUNK), 0)
    col = lax.broadcasted_iota(jnp.int32, (CHUNK, CHUNK), 1)
    return (row - col) * (1 - 2 * d) >= 0


def _rwkv_chunk_kernel(r_ref, k_ref, v_ref, z_ref, b_ref, lw_ref, rp_ref, y0_ref, mx_ref, s0_ref, pt_ref):
    zeros = jnp.zeros((CHUNK, RW_HEAD), BF16)
    row = lax.broadcasted_iota(jnp.int32, (2 * CHUNK, 2 * CHUNK), 0)
    col = lax.broadcasted_iota(jnp.int32, (2 * CHUNK, 2 * CHUNK), 1)
    chains = []
    for cc in range(r_ref.shape[0] // CHUNK):
        rows = slice(cc * CHUNK, (cc + 1) * CHUNK)
        r = r_ref[rows, :]
        k = k_ref[rows, :]
        z = z_ref[rows, :]
        b = b_ref[rows, :]
        vb = v_ref[rows, :].astype(BF16)
        for d in range(2):
            order = ((row & (CHUNK - 1)) - (col & (CHUNK - 1))) * (1 - 2 * d)
            mask = order + (row >> 6) > 0
            lw = lw_ref[d, rows, :]
            cs = _cumsum_rows(_incl_mask(d).astype(BF16), lw)
            tot = jnp.sum(lw, axis=0, keepdims=True)
            p_inc = jnp.exp(cs)
            p_inv = jnp.exp(-cs)
            p_end = jnp.exp(tot - cs)
            zt = z * jnp.exp(cs - lw)
            rt = r * p_inc
            zr = jnp.concatenate([zt, rt], axis=0).astype(BF16)
            bk = jnp.concatenate([b * p_inv, k * p_inv], axis=0).astype(BF16)
            be = (b * p_end).astype(BF16)
            ke = (k * p_end).astype(BF16)
            pt_ref[d, cc] = jnp.exp(tot)
            for h in range(RW_HEADS):
                sl = slice(h * RW_HEAD, (h + 1) * RW_HEAD)
                g = jnp.where(mask, _dot_nt(zr[:, sl], bk[:, sl]), 0.0)
                chains.append(dict(d=d, rows=rows, sl=sl, vh=vb[:, sl], zt=zt[:, sl], rt=rt[:, sl],
                                   be=be[:, sl], ke=ke[:, sl], gtop=g[:CHUNK].astype(BF16),
                                   gbot=g[CHUNK:].astype(BF16), p=g[:CHUNK, :CHUNK]))
    for c in chains:
        akv = _dot(c["gtop"], jnp.concatenate([zeros, c["vh"]], axis=0))
        c["x"] = jnp.concatenate([c["zt"], akv], axis=1)
    row64 = lax.broadcasted_iota(jnp.int32, (CHUNK, CHUNK), 0)
    col64 = lax.broadcasted_iota(jnp.int32, (CHUNK, CHUNK), 1)

    def coupling(level):
        return ((row64 >> (level + 1)) == (col64 >> (level + 1))) & ((row64 >> level) != (col64 >> level))

    eye = (row64 == col64).astype(F32)
    for c in chains:
        c["vt"] = c["vh"].T
        c["t"] = eye + jnp.where(coupling(0), c["p"], 0.0)
    for level in range(1, CHUNK.bit_length() - 1):
        for c in chains:
            off = jnp.where(coupling(level), c["p"], 0.0).astype(BF16)
            c["ot"] = _dot(off, c["t"].astype(BF16))
        for c in chains:
            c["t"] = c["t"] + _dot(c["t"].astype(BF16), c["ot"].astype(BF16))
    for c in chains:
        c["xb"] = _dot(c["t"].astype(BF16), c["x"].astype(BF16)).astype(BF16)
    for c in chains:
        c["xt"] = c["xb"].T
    for c in chains:
        rhs = jnp.concatenate([c["xb"], jnp.concatenate([zeros, c["vh"]], axis=1)], axis=0)
        c["ry"] = _dot(c["gbot"], rhs)
    for c in chains:
        c["ms"] = _dot(c["xt"], c["be"])
        c["vk"] = _dot(c["vt"], c["ke"])
    for c in chains:
        d, rows, sl = c["d"], c["rows"], c["sl"]
        rp_ref[d, rows, sl] = (c["rt"] + c["ry"][:, :RW_HEAD]).astype(rp_ref.dtype)
        y0_ref[d, rows, sl] = c["ry"][:, RW_HEAD:]
        mx_ref[d, rows, sl] = c["ms"][:RW_HEAD]
        s0_ref[d, rows, sl] = c["ms"][RW_HEAD:] + c["vk"]


def _rwkv_state_phases(rp0, y00, mx0, s00, pt0, rp1, y01, mx1, s01, pt1, yf_ref, yb_ref, s_ref):
    dirs = ((rp0, y00, mx0, s00, pt0, yf_ref), (rp1, y01, mx1, s01, pt1, yb_ref))
    heads = [(d, h) for d in range(2) for h in range(RW_HEADS)]
    state = {dh: s_ref[dh[0], dh[1]] for dh in heads}
    for step in range(CPB):
        for d, h in heads:
            rp, y0, mx, s0, pt, y_ref = dirs[d]
            cc = _chunk_order(d)[step]
            rows = slice(cc * CHUNK, (cc + 1) * CHUNK)
            sl = slice(h * RW_HEAD, (h + 1) * RW_HEAD)
            s = state[d, h]
            y_ref[rows, sl] = _dot_nt(rp[rows, sl], s.astype(BF16)) + y0[rows, sl]
            state[d, h] = s * pt[cc, :, sl] + _dot_split(s, mx[rows, sl]) + s0[rows, sl]
        if step == CPB - 1:
            for d, h in heads:
                s_ref[d, h] = state[d, h]
        yield


def _hgrn_phases(q0_ref, f0_ref, i0_ref, q1_ref, f1_ref, i1_ref, lb_ref, of_ref, ob_ref, s_ref):
    lb = lb_ref[...]
    dirs = ((q0_ref, f0_ref, i0_ref, of_ref), (q1_ref, f1_ref, i1_ref, ob_ref))
    groups = []
    for d, (q_ref, f_ref, i_ref, o_ref) in enumerate(dirs):
        for cc in _chunk_order(d):
            rows = slice(cc * CHUNK, (cc + 1) * CHUNK)
            fg = lb + (1.0 - lb) * jax.nn.sigmoid(f_ref[rows, :])
            groups.append(dict(d=d, rows=rows, o_ref=o_ref, q=_silu(q_ref[rows, :]), logf=jnp.log(fg),
                               k=1.0 - fg, vb=i_ref[rows, :].astype(BF16)))
    for g in groups:
        g["cs"] = _cumsum_rows(_incl_mask(g["d"]).astype(BF16), g["logf"])
    yield
    for g in groups:
        cs = g["cs"]
        tot = jnp.sum(g["logf"], axis=0, keepdims=True)
        g["qb"] = (g["q"] * jnp.exp(cs)).astype(BF16)
        g["kb"] = (g["k"] * jnp.exp(-cs)).astype(BF16)
        g["kd"] = (g["k"] * jnp.exp(tot - cs)).astype(BF16)
        g["gl"] = jnp.exp(tot)
    heads = [slice(h * HG_DK, (h + 1) * HG_DK) for h in range(HG_HEADS)]
    for g in groups:
        incl = _incl_mask(g["d"])
        g["sc"] = [jnp.where(incl, _dot_nt(g["qb"][:, sl], g["kb"][:, sl]), 0.0).astype(BF16) for sl in heads]
    yield
    for g in groups:
        g["vt"] = g["vb"].T
    for g in groups:
        g["intra"] = [_dot(sc, g["vb"][:, sl]) for sc, sl in zip(g["sc"], heads)]
        g["kv"] = [_dot(g["vt"][sl, :], g["kd"][:, sl]) for sl in heads]
    yield
    state = {(d, h): s_ref[d, h] for d in range(2) for h in range(HG_HEADS)}
    for g in groups:
        d = g["d"]
        for h, sl in enumerate(heads):
            s = state[d, h]
            g["o_ref"][g["rows"], sl] = g["intra"][h] + _dot_nt(g["qb"][:, sl], s.astype(BF16))
            state[d, h] = s * g["gl"][:, sl] + g["kv"][h]
    for (d, h), s in state.items():
        s_ref[d, h] = s


N_HG_IN = 7
N_RW_IN = 10


def _even_scan_kernel(*refs):
    hg_in = refs[:N_HG_IN]
    rw_in = refs[N_HG_IN:N_HG_IN + N_RW_IN]
    of_ref, ob_ref, yf_ref, yb_ref, hg_state, rw_state = refs[N_HG_IN + N_RW_IN:]

    @pl.when(pl.program_id(1) == 0)
    def _():
        hg_state[...] = jnp.zeros_like(hg_state)
        rw_state[...] = jnp.zeros_like(rw_state)

    streams = [_rwkv_state_phases(*rw_in, yf_ref, yb_ref, rw_state),
               _hgrn_phases(*hg_in, of_ref, ob_ref, hg_state)]
    while streams:
        streams = [s for s in streams if next(s, True) is None]


def _even_scan(ub, lb, rp, y0, mx, s0, pt, batch, ctx_blocks, seq_blocks):
    n = ub.shape[0]
    hg_in, rw_in, hg_out, rw_out = [], [], [], []
    for d in range(2):
        blk = lambda bi, g, d=d: _block_index(bi, d, g, batch, ctx_blocks, seq_blocks)
        hg_in += [
            pl.BlockSpec((ROW_BLK, HG_KW), lambda bi, g, blk=blk: (blk(bi, g), 0)),
            pl.BlockSpec((ROW_BLK, HG_KW), lambda bi, g, blk=blk, d=d: (blk(bi, g), 1 + d)),
            pl.BlockSpec((ROW_BLK, HG_KW), lambda bi, g, blk=blk: (blk(bi, g), 3)),
        ]
        big = pl.BlockSpec((None, ROW_BLK, RW_W), lambda bi, g, d=d, blk=blk: (d, blk(bi, g), 0))
        rw_in += [big] * 4 + [pl.BlockSpec((None, CPB, 1, RW_W), lambda bi, g, d=d, blk=blk: (d, blk(bi, g), 0, 0))]
        hg_out.append(pl.BlockSpec((ROW_BLK, HG_KW), lambda bi, g, blk=blk: (blk(bi, g), 0)))
        rw_out.append(pl.BlockSpec((ROW_BLK, RW_W), lambda bi, g, blk=blk: (blk(bi, g), 0)))
    out = jax.ShapeDtypeStruct((n, HG_KW), F32)
    return pl.pallas_call(
        _even_scan_kernel,
        grid=(batch, seq_blocks),
        in_specs=hg_in + [_const_spec((1, HG_KW))] + rw_in,
        out_specs=hg_out + rw_out,
        out_shape=[out] * 4,
        scratch_shapes=[pltpu.VMEM((2, HG_HEADS, HG_DK, HG_DK), F32),
                        pltpu.VMEM((2, RW_HEADS, RW_HEAD, RW_HEAD), F32)],
        compiler_params=_cparams(("parallel", "arbitrary")),
    )(ub, ub, ub, ub, ub, ub, lb.reshape(1, HG_KW), rp, y0, mx, s0, pt, rp, y0, mx, s0, pt)


def _even_out_kernel(h_ref, mod_ref, yf_ref, yb_ref, bonus_ref, grw_ref, of_ref, ob_ref, ghg_ref, gng_ref, gnb_ref,
                     hgn_ref, gm64_ref, gm128_ref, w_ref, g_ref, b_ref, out_ref, acc_ref, *, nblk, alpha):
    y = yf_ref[...] + yb_ref[...]
    mu = _group_sum(y, gm64_ref[...])
    yc = y - mu
    var = _group_sum(yc * yc, gm64_ref[...])
    ra = (yc * lax.rsqrt(var + RW_GN_EPS) * gng_ref[...] + gnb_ref[...] + bonus_ref[...]) * grw_ref[...]
    o = of_ref[...] + ob_ref[...]
    ms = _group_sum(o * o, gm128_ref[...])
    hb = o * lax.rsqrt(ms + RMS_EPS) * hgn_ref[...] * _silu(ghg_ref[...])
    acc_ref[...] = _dot(ra.astype(BF16), w_ref[:RW_W, :]) + _dot(hb.astype(BF16), w_ref[RW_W:, :])
    _post_norm_rows(h_ref, out_ref, acc_ref, mod_ref, g_ref, b_ref, nblk, alpha, 1.0)


def _even_out(h, mod3, yf, yb, bonus, g_rw, of, ob, ub, gn_g, gn_b, hg_norm, gm64, gm128, w_out, ln_g, ln_b, alpha,
              tm=256):
    n, d = h.shape
    nblk = tm // ROW_BLK
    kern = functools.partial(_even_out_kernel, nblk=nblk, alpha=alpha)
    row = lambda i: (i, 0)
    vec = _const_spec((1, RW_W))
    return pl.pallas_call(
        kern,
        grid=(n // tm,),
        in_specs=[
            pl.BlockSpec((tm, d), row),
            pl.BlockSpec((nblk, 3, d), lambda i: (i, 0, 0)),
            pl.BlockSpec((tm, RW_W), row),
            pl.BlockSpec((tm, RW_W), row),
            pl.BlockSpec((tm, RW_W), row),
            pl.BlockSpec((tm, RW_W), row),
            pl.BlockSpec((tm, HG_KW), row),
            pl.BlockSpec((tm, HG_KW), row),
            pl.BlockSpec((tm, HG_KW), lambda i: (i, 4)),
            vec, vec, vec,
            _const_spec((MXU_TILE, MXU_TILE)),
            _const_spec((MXU_TILE, MXU_TILE)),
            _const_spec((d, d)),
            _const_spec((1, d)),
            _const_spec((1, d)),
        ],
        out_specs=pl.BlockSpec((tm, d), row),
        out_shape=jax.ShapeDtypeStruct((n, d), F32),
        scratch_shapes=[pltpu.VMEM((tm, d), F32)],
        compiler_params=_cparams(("parallel",)),
    )(h, mod3, yf, yb, bonus, g_rw, of, ob, ub, gn_g.reshape(1, RW_W), gn_b.reshape(1, RW_W), hg_norm, gm64, gm128,
      w_out.astype(BF16), ln_g.reshape(1, d), ln_b.reshape(1, d))


def _rope(t, cos, sin_up, sin_dn):
    return t * cos + pltpu.roll(t, LANES - 16, 1) * sin_up + pltpu.roll(t, 16, 1) * sin_dn


def _odd_proj_kernel(h_ref, mod_ref, w_ref, cos_ref, sup_ref, sdn_ref, q_ref, k_ref, v_ref, xm_ref, *, nblk):
    _modulate_rows(h_ref, mod_ref, xm_ref, nblk)
    xm = xm_ref[...]
    cos, sup, sdn = cos_ref[...], sup_ref[...], sdn_ref[...]
    q = _dot(xm, w_ref[:, :DA_W])
    k = _dot(xm, w_ref[:, DA_W:2 * DA_W])
    for h in range(DA_HEADS):
        sl = slice(h * LANES, (h + 1) * LANES)
        q_ref[:, sl] = (_rope(q[:, sl], cos, sup, sdn) * Q_SCALE).astype(q_ref.dtype)
        k_ref[:, sl] = _rope(k[:, sl], cos, sup, sdn).astype(k_ref.dtype)
    v_ref[...] = _dot(xm, w_ref[:, 2 * DA_W:]).astype(v_ref.dtype)


def _odd_proj(h, mod3, w, cos, sup, sdn, tm=512):
    n, d = h.shape
    nblk = tm // ROW_BLK
    kern = functools.partial(_odd_proj_kernel, nblk=nblk)
    row = lambda i: (i, 0)
    out = jax.ShapeDtypeStruct((n, DA_W), BF16)
    return pl.pallas_call(
        kern,
        grid=(n // tm,),
        in_specs=[
            pl.BlockSpec((tm, d), row),
            pl.BlockSpec((nblk, 3, d), lambda i: (i, 0, 0)),
            _const_spec((d, 3 * DA_W)),
            pl.BlockSpec((tm, LANES), row),
            pl.BlockSpec((tm, LANES), row),
            pl.BlockSpec((tm, LANES), row),
        ],
        out_specs=[pl.BlockSpec((tm, DA_W), row)] * 3,
        out_shape=[out, out, out],
        scratch_shapes=[pltpu.VMEM((tm, d), BF16)],
        compiler_params=_cparams(("parallel",)),
    )(h, mod3, w, cos, sup, sdn)


def _diff_attn_kernel(q_ref, kl_ref, kc_ref, vl_ref, vc_ref, lam_ref, ng_ref, o_ref, k_ref, v_ref,
                      *, lam_init, bq):
    t_len = kl_ref.shape[0]

    @pl.when(pl.program_id(2) == 0)
    def _():
        k_ref[:t_len, :] = kl_ref[...]
        k_ref[t_len:, :] = kc_ref[...]
        v_ref[:t_len, :] = vl_ref[...]
        v_ref[t_len:, :] = vc_ref[...]

    lv = lam_ref[...]
    lam = (jnp.exp(jnp.sum(lv[0:1] * lv[1:2], axis=-1, keepdims=True))
           - jnp.exp(jnp.sum(lv[2:3] * lv[3:4], axis=-1, keepdims=True)) + lam_init)
    k = k_ref[...]
    halves = [slice(i * ATT_SUB, (i + 1) * ATT_SUB) for i in range(bq // ATT_SUB)]

    def scores(rows):
        q = q_ref[rows, :]
        return [_dot_nt(q[:, m * DA_HEAD:(m + 1) * DA_HEAD], k[:, m * DA_HEAD:(m + 1) * DA_HEAD])
                for m in range(2)]

    def readout(rows, s):
        es, ls = [], []
        for m in range(2):
            e = jnp.exp2(s[m] - jnp.max(s[m], axis=-1, keepdims=True))
            es.append(e)
            ls.append(jnp.sum(e, axis=-1, keepdims=True))
        a = es[0] - (lam * ls[0] / ls[1]) * es[1]
        o = _dot(a.astype(BF16), v_ref[...]) / ls[0]
        ms = jnp.mean(o * o, axis=-1, keepdims=True)
        o_ref[rows, :] = (o * lax.rsqrt(ms + RMS_EPS) * ng_ref[...] * (1.0 - lam_init)).astype(o_ref.dtype)

    s_next = scores(halves[0])
    for i, rows in enumerate(halves):
        s_cur = s_next
        if i + 1 < len(halves):
            s_next = scores(halves[i + 1])
        readout(rows, s_cur)


def _diff_attn(q, k, v, lam_vecs, norm_g, lam_init, batch, seq_blocks, ctx_blocks, bq=2 * ATT_SUB):
    t_len = (seq_blocks - ctx_blocks) * ROW_BLK
    c_len = ctx_blocks * ROW_BLK
    n_lat = batch * t_len
    assert t_len % bq == 0 and n_lat % c_len == 0
    nt = t_len // bq
    kern = functools.partial(_diff_attn_kernel, lam_init=lam_init, bq=bq)
    lat_spec = pl.BlockSpec((t_len, LANES), lambda b, h, i: (b, h))
    ctx_spec = pl.BlockSpec((c_len, LANES), lambda b, h, i: (n_lat // c_len + b, h))
    tile_spec = pl.BlockSpec((bq, LANES), lambda b, h, i: (b * nt + i, h))
    return pl.pallas_call(
        kern,
        grid=(batch, DA_HEADS, nt),
        in_specs=[tile_spec, lat_spec, ctx_spec, lat_spec, ctx_spec, _const_spec((4, DA_HEAD)),
                  _const_spec((1, LANES))],
        out_specs=tile_spec,
        out_shape=jax.ShapeDtypeStruct((n_lat, DA_W), BF16),
        scratch_shapes=[pltpu.VMEM((t_len + c_len, LANES), BF16), pltpu.VMEM((t_len + c_len, LANES), BF16)],
        compiler_params=_cparams(("parallel", "parallel", "arbitrary")),
    )(q, k, k, v, v, lam_vecs, norm_g.reshape(1, LANES))


def _odd_out_kernel(h_ref, mod_ref, x_ref, w_ref, g_ref, b_ref, out_ref, acc_ref, *, nblk, alpha):
    acc_ref[...] = _dot(x_ref[...], w_ref[...])
    _post_norm_rows(h_ref, out_ref, acc_ref, mod_ref, g_ref, b_ref, nblk, alpha, 1.0)


def _odd_out(h, mod3, x, w_out, ln_g, ln_b, alpha, tm=512):
    n, d = x.shape[0], h.shape[1]
    nblk = tm // ROW_BLK
    kern = functools.partial(_odd_out_kernel, nblk=nblk, alpha=alpha)
    row = lambda i: (i, 0)
    return pl.pallas_call(
        kern,
        grid=(n // tm,),
        in_specs=[
            pl.BlockSpec((tm, d), row),
            pl.BlockSpec((nblk, 3, d), lambda i: (i, 0, 0)),
            pl.BlockSpec((tm, DA_W), row),
            _const_spec((DA_W, d)),
            _const_spec((1, d)),
            _const_spec((1, d)),
        ],
        out_specs=pl.BlockSpec((tm, d), row),
        out_shape=jax.ShapeDtypeStruct((n, d), F32),
        scratch_shapes=[pltpu.VMEM((tm, d), F32)],
        compiler_params=_cparams(("parallel",)),
    )(h, mod3, x, w_out.astype(BF16), ln_g.reshape(1, d), ln_b.reshape(1, d))


def _rope_tables(batch, ctx_len, t):
    n_rows = t // GRID_W
    rowp = np.repeat(np.arange(n_rows), GRID_W).astype(np.float32)
    colp = np.tile(np.arange(GRID_W), n_rows).astype(np.float32)
    n_freq = DA_HEAD // 4
    inv = jnp.asarray(ROPE_BASE, F32) ** (-jnp.arange(n_freq, dtype=F32) / n_freq)
    ar = jnp.asarray(rowp)[:, None] * inv
    ac = jnp.asarray(colp)[:, None] * inv
    ang = jnp.concatenate([ar, ar, ac, ac], axis=-1)
    n_ctx = batch * ctx_len
    cos = jnp.concatenate([jnp.tile(jnp.cos(ang), (batch, 1)), jnp.ones((n_ctx, DA_HEAD), F32)], axis=0)
    sin = jnp.concatenate([jnp.tile(jnp.sin(ang), (batch, 1)), jnp.zeros((n_ctx, DA_HEAD), F32)], axis=0)
    first = (np.arange(DA_HEAD) % 32) < 16
    sup = jnp.where(first, -sin, 0.0)
    sdn = jnp.where(first, 0.0, sin)
    rep = lambda a: jnp.tile(a, (1, 2))
    return rep(cos), rep(sup), rep(sdn)


def _group_mean_matrix(group, scale=1.0):
    assert MXU_TILE % group == 0
    idx = np.arange(MXU_TILE) // group
    return jnp.asarray((idx[:, None] == idx[None, :]).astype(np.float32) * (scale / group)).astype(BF16)


def kernel(x, c, ctx, c_ctx, w_mod, b_mod, ln_g, ln_b, ffn_w_in, ffn_w_out, ev_w_in, ev_w_out,
           rw_mu, rw_w0, rw_w2, rw_a0, rw_a2, rw_g2, rw_k_k, rw_k_a, rw_r_k, rw_gn_g, rw_gn_b,
           hg_lb, hg_norm_g, od_w_in, od_w_out, da_lambda, da_norm_g):
    batch, t, d = x.shape
    ctx_len = ctx.shape[1]
    depth = w_mod.shape[0]
    assert t % ROW_BLK == 0 and ctx_len % ROW_BLK == 0 and t % GRID_W == 0
    alpha = (2 * depth) ** 0.25
    s_len = ctx_len + t
    seq_blocks = s_len // ROW_BLK
    ctx_blocks = ctx_len // ROW_BLK
    lat_blocks = seq_blocks - ctx_blocks
    n_lat = batch * t

    h = (x.reshape(n_lat, d), ctx.reshape(batch * ctx_len, d))

    rows_pad = -(-(batch + 1) // SUBLANES) * SUBLANES
    cvec = jnp.zeros((rows_pad, d), F32).at[:batch].set(c).at[batch].set(c_ctx)
    m_all = _modulation(cvec, w_mod, b_mod).reshape(depth, rows_pad, N_SUB, 3, d)
    blk_row = np.array([bi for bi in range(batch) for _ in range(lat_blocks)] + [batch] * (batch * ctx_blocks))
    m_blk = m_all[:, blk_row]

    lower_bounds = jnp.cumsum(jax.nn.softmax(hg_lb.astype(F32), axis=0), axis=0)
    gsum64 = _group_mean_matrix(RW_HEAD, scale=RW_HEAD)
    gm64 = _group_mean_matrix(RW_HEAD)
    gm128 = _group_mean_matrix(HG_DK)

    for i in range(depth):
        j = i // 2
        with_ctx_out = i < depth - 1
        ffn = functools.partial(_ffn_sublayer, alpha=alpha)
        h = ffn(h, m_blk[i, :, 0], ffn_w_in[i, 0], ffn_w_out[i, 0], ln_g[i, 0], ln_b[i, 0])
        if i % 2 == 0:
            pad_a = jnp.zeros((d, A_PAD - A_TOTAL), F32)
            w_pad = jnp.concatenate([ev_w_in[j][:, :A_TOTAL], pad_a, ev_w_in[j][:, A_TOTAL:]], axis=1).astype(BF16)
            ua, ub = _even_proj(h, m_blk[i, :, 1], w_pad)
            mu_pad = jnp.concatenate([rw_mu[j], jnp.zeros((A_PAD - A_TOTAL,), F32)]).reshape(1, A_PAD)
            w_lora = jnp.zeros((RW_LORA_PAD, 4 * RW_W), F32)
            w_lora = w_lora.at[0:64, 0:RW_W].set(rw_w2[j, 0]).at[64:128, RW_W:2 * RW_W].set(rw_w2[j, 1])
            w_lora = w_lora.at[128:192, 2 * RW_W:3 * RW_W].set(rw_a2[j])
            w_lora = w_lora.at[192:192 + RW_GATE_LORA, 3 * RW_W:].set(rw_g2[j]).astype(BF16)
            rp, y0, mx, s0, pt, g_rw, bonus = _rwkv_prep_chunks(
                ua, mu_pad, w_lora, rw_w0[j], rw_a0[j], rw_k_k[j], rw_k_a[j], rw_r_k[j].reshape(RW_W), gsum64,
                batch, lat_blocks, ctx_blocks)
            of, ob, yf, yb = _even_scan(ub, lower_bounds[i], rp, y0, mx, s0, pt, batch, ctx_blocks, seq_blocks)
            hg_norm = jnp.tile(hg_norm_g[j], HG_HEADS).reshape(1, HG_KW)
            h = _even_out(h, m_blk[i, :, 1], yf, yb, bonus, g_rw, of, ob, ub, rw_gn_g[j], rw_gn_b[j], hg_norm, gm64, gm128,
                          ev_w_out[j], ln_g[i, 1], ln_b[i, 1], alpha)
        else:
            lam_init = 0.8 - 0.6 * math.exp(-0.3 * i)
            cos, sup, sdn = _rope_tables(batch, ctx_len, t)
            q, k, v = _odd_proj(h, m_blk[i, :, 1], od_w_in[j].astype(BF16), cos, sup, sdn)
            att = _diff_attn(q, k, v, da_lambda[j], da_norm_g[j], lam_init, batch, seq_blocks, ctx_blocks)
            assert not with_ctx_out
            h = _odd_out(h, m_blk[i, :, 1], att, od_w_out[j], ln_g[i, 1], ln_b[i, 1], alpha)
        h = ffn(h, m_blk[i, :, 2], ffn_w_in[i, 1], ffn_w_out[i, 1], ln_g[i, 2], ln_b[i, 2],
                n_rows=None if with_ctx_out else n_lat)
    return h[:n_lat].reshape(batch, t, d)
```

```python
import functools
import math

import numpy as np
import jax
import jax.numpy as jnp
from jax import lax
from jax.experimental import pallas as pl
from jax.experimental.pallas import tpu as pltpu

F32 = jnp.float32
BF16 = jnp.bfloat16
HIGHEST = lax.Precision.HIGHEST

N_SUB = 3
LN_EPS = 1e-5
RMS_EPS = 1e-5
RW_HEADS = 8
RW_HEAD = 64
RW_W = RW_HEADS * RW_HEAD
RW_DECAY_LORA = 64
RW_A_LORA = 64
RW_GATE_LORA = 160
RW_GN_EPS = 64e-5
RW_MAIN = 3 * RW_W
RW_LORA = 2 * RW_DECAY_LORA + RW_A_LORA + RW_GATE_LORA
RW_LORA_PAD = 384
A_TOTAL = RW_MAIN + RW_LORA
A_PAD = RW_MAIN + RW_LORA_PAD
HG_HEADS = 4
HG_DK = 128
HG_KW = HG_HEADS * HG_DK
B_TOTAL = 5 * HG_KW
DA_HEADS = 8
DA_HEAD = 64
DA_W = DA_HEADS * 2 * DA_HEAD
GRID_W = 64
ROPE_BASE = 10000.0
Q_SCALE = DA_HEAD ** -0.5 * math.log2(math.e)

LANES = 128
SUBLANES = 8
MXU_TILE = 256
ROW_BLK = 256
CHUNK = 64
ATT_SUB = 256
VMEM_LIMIT = 56 * 1024 * 1024


def _cparams(sem):
    return pltpu.CompilerParams(dimension_semantics=sem, vmem_limit_bytes=VMEM_LIMIT)


def _const_spec(shape):
    nd = len(shape)
    return pl.BlockSpec(shape, lambda *_: (0,) * nd, pipeline_mode=pl.Buffered(1))


def _dot(a, b):
    return jnp.dot(a, b, preferred_element_type=F32)


def _dot_nt(a, b):
    return lax.dot_general(a, b, (((1,), (1,)), ((), ())), preferred_element_type=F32)


def _dot_hi(a, b):
    return jnp.dot(a, b, preferred_element_type=F32, precision=HIGHEST)


def _split(x, pieces):
    out = []
    for _ in range(pieces):
        p = x.astype(BF16)
        out.append(p)
        x = x - p.astype(F32)
    return out


def _group_sum(x, gmat):
    pieces = _split(x, 2)
    slabs = [sum(_dot(p[:, c:c + MXU_TILE], gmat) for p in pieces) for c in range(0, x.shape[1], MXU_TILE)]
    return jnp.concatenate(slabs, axis=1)


def _dot_split(a, b):
    a_hi, a_lo = _split(a, 2)
    b_hi, b_lo = _split(b, 2)
    return _dot(a_hi, b_hi) + _dot(a_hi, b_lo) + _dot(a_lo, b_hi)


def _cumsum_rows(tri, x):
    return sum(_dot(tri, p) for p in _split(x, 3))


def _silu(x):
    return x * jax.nn.sigmoid(x)


def _layernorm(y, g, b):
    mu = jnp.mean(y, axis=-1, keepdims=True)
    yc = y - mu
    var = jnp.mean(yc * yc, axis=-1, keepdims=True)
    return yc * lax.rsqrt(var + LN_EPS) * g + b


def _mod_kernel(c_ref, w_ref, b_ref, o_ref):
    o_ref[...] = _dot_hi(_silu(c_ref[...]), w_ref[...]) + b_ref[...]


def _modulation(cvec, w_mod, b_mod):
    depth, d, width = w_mod.shape
    rows = cvec.shape[0]
    tn = 1024
    return pl.pallas_call(
        _mod_kernel,
        grid=(depth, width // tn),
        in_specs=[
            pl.BlockSpec((rows, d), lambda l, j: (0, 0)),
            pl.BlockSpec((None, d, tn), lambda l, j: (l, 0, j)),
            pl.BlockSpec((None, 1, tn), lambda l, j: (l, 0, j)),
        ],
        out_specs=pl.BlockSpec((None, rows, tn), lambda l, j: (l, 0, j)),
        out_shape=jax.ShapeDtypeStruct((depth, rows, width), F32),
        compiler_params=_cparams(("parallel", "parallel")),
    )(cvec, w_mod, b_mod.reshape(depth, 1, width))


def _modulate_rows(h_ref, mod_ref, xm_ref, nblk):
    for s in range(nblk):
        rows = pl.ds(s * ROW_BLK, ROW_BLK)
        x = h_ref[rows, :]
        xm_ref[rows, :] = (x * (1.0 + mod_ref[s, 1:2, :]) + mod_ref[s, 0:1, :]).astype(xm_ref.dtype)


def _post_norm_rows(h_ref, out_ref, acc_ref, mod_ref, g_ref, b_ref, nblk, alpha, weight):
    for s in range(nblk):
        rows = pl.ds(s * ROW_BLK, ROW_BLK)
        y = alpha * h_ref[rows, :] + (weight * mod_ref[s, 2:3, :]) * acc_ref[rows, :]
        out_ref[rows, :] = _layernorm(y, g_ref[...], b_ref[...])


def _ffn_two_source_kernel(ha_ref, hb_ref, *rest, n_first, **kw):
    *refs, h_sel = rest

    @pl.when(pl.program_id(0) < n_first)
    def _():
        h_sel[...] = ha_ref[...].astype(h_sel.dtype)

    @pl.when(pl.program_id(0) >= n_first)
    def _():
        h_sel[...] = hb_ref[...].astype(h_sel.dtype)

    _ffn_kernel(h_sel, *refs, **kw)


def _ffn_kernel(h_ref, mod_ref, wi_ref, wo_ref, g_ref, b_ref, out_ref, xm_ref, acc_ref, *, nblk, tf, alpha):
    _modulate_rows(h_ref, mod_ref, xm_ref, nblk)
    f = wo_ref.shape[0]
    for fc in range(f // tf):
        cols = slice(fc * tf, (fc + 1) * tf)
        xm = xm_ref[...]
        gt = _dot(xm, wi_ref[:, cols])
        up = _dot(xm, wi_ref[:, f + fc * tf:f + (fc + 1) * tf])
        act = (_silu(gt) * up).astype(BF16)
        part = _dot(act, wo_ref[cols, :])
        if fc == 0:
            acc_ref[...] = part
        else:
            acc_ref[...] += part
    _post_norm_rows(h_ref, out_ref, acc_ref, mod_ref, g_ref, b_ref, nblk, alpha, 0.5)


def _ffn_sublayer(h, mod3, w_in, w_out, ln_g, ln_b, alpha, n_rows=None, tm=512, tf=256):
    sources = h if isinstance(h, tuple) else (h,)
    d = sources[0].shape[1]
    n = sum(s.shape[0] for s in sources) if n_rows is None else n_rows
    f = w_out.shape[0]
    assert f % tf == 0 and n % tm == 0 and all(s.shape[0] % tm == 0 for s in sources)
    nblk = tm // ROW_BLK
    kw = dict(nblk=nblk, tf=tf, alpha=alpha)
    scratch = [pltpu.VMEM((tm, d), BF16), pltpu.VMEM((tm, d), F32)]
    if len(sources) == 1:
        kern = functools.partial(_ffn_kernel, **kw)
        h_specs = [pl.BlockSpec((tm, d), lambda i: (i, 0))]
    else:
        n_first = sources[0].shape[0] // tm
        last_second = sources[1].shape[0] // tm - 1
        kern = functools.partial(_ffn_two_source_kernel, n_first=n_first, **kw)
        h_specs = [pl.BlockSpec((tm, d), lambda i: (jnp.minimum(i, n_first - 1), 0)),
                   pl.BlockSpec((tm, d), lambda i: (jnp.clip(i - n_first, 0, last_second), 0))]
        scratch.append(pltpu.VMEM((tm, d), F32))
    return pl.pallas_call(
        kern,
        grid=(n // tm,),
        in_specs=h_specs + [
            pl.BlockSpec((nblk, 3, d), lambda i: (i, 0, 0)),
            _const_spec((d, 2 * f)),
            _const_spec((f, d)),
            _const_spec((1, d)),
            _const_spec((1, d)),
        ],
        out_specs=pl.BlockSpec((tm, d), lambda i: (i, 0)),
        out_shape=jax.ShapeDtypeStruct((n, d), F32),
        scratch_shapes=scratch,
        compiler_params=_cparams(("parallel",)),
    )(*sources, mod3, w_in.astype(BF16), w_out.astype(BF16), ln_g.reshape(1, d), ln_b.reshape(1, d))


def _even_proj_kernel(h_ref, mod_ref, w_ref, ua_ref, ub_ref, xm_ref, *, nblk):
    _modulate_rows(h_ref, mod_ref, xm_ref, nblk)
    xm = xm_ref[...]
    ua_ref[...] = _dot(xm, w_ref[:, :A_PAD])
    ub_ref[...] = _dot(xm, w_ref[:, A_PAD:])


def _even_proj(h, mod3, w_pad, tm=512):
    n, d = h.shape
    nblk = tm // ROW_BLK
    kern = functools.partial(_even_proj_kernel, nblk=nblk)
    return pl.pallas_call(
        kern,
        grid=(n // tm,),
        in_specs=[
            pl.BlockSpec((tm, d), lambda i: (i, 0)),
            pl.BlockSpec((nblk, 3, d), lambda i: (i, 0, 0)),
            _const_spec((d, A_PAD + B_TOTAL)),
        ],
        out_specs=[pl.BlockSpec((tm, A_PAD), lambda i: (i, 0)),
                   pl.BlockSpec((tm, B_TOTAL), lambda i: (i, 0))],
        out_shape=[jax.ShapeDtypeStruct((n, A_PAD), F32), jax.ShapeDtypeStruct((n, B_TOTAL), F32)],
        scratch_shapes=[pltpu.VMEM((tm, d), BF16)],
        compiler_params=_cparams(("parallel",)),
    )(h, mod3, w_pad)


def _rwkv_prep_kernel(u_ref, prev_ref, next_ref, mu_ref, wl_ref, w0_ref, a0_ref, kk_ref, ka_ref, rk_ref,
                      gsum_ref, r_out, k_out, v_out, z_out, b_out, lw_out, g_out, bonus_out,
                      *, lat_total, lat_blocks, ctx_blocks):
    i = pl.program_id(0)
    is_lat = i < lat_total
    j = jnp.where(is_lat, i % lat_blocks, (i - lat_total) % ctx_blocks)
    last = jnp.where(is_lat, lat_blocks - 1, ctx_blocks - 1)
    at_start = j == 0
    at_end = j == last
    u = u_ref[...]
    prev_row = jnp.where(at_start, 0.0, prev_ref[SUBLANES - 1:SUBLANES, :])
    next_row = jnp.where(at_end, 0.0, next_ref[0:1, :])
    rid = lax.broadcasted_iota(jnp.int32, u.shape, 0)
    up = jnp.where(rid == 0, prev_row, pltpu.roll(u, 1, 0))
    dn = jnp.where(rid == ROW_BLK - 1, next_row, pltpu.roll(u, ROW_BLK - 1, 0))
    u = u + mu_ref[...] * (0.5 * (up + dn) - u)

    r = u[:, 0:RW_W]
    k = u[:, RW_W:2 * RW_W]
    v = u[:, 2 * RW_W:3 * RW_W]
    lo = u[:, RW_MAIN:A_PAD]
    lane = lax.broadcasted_iota(jnp.int32, lo.shape, 1)
    lo = jnp.where(lane < 2 * RW_DECAY_LORA, jnp.tanh(lo),
                   jnp.where(lane < 2 * RW_DECAY_LORA + RW_A_LORA, lo, jax.nn.sigmoid(lo)))
    lob = lo.astype(BF16)
    lora = _dot(lob[:, :MXU_TILE], wl_ref[:MXU_TILE, :3 * RW_W])
    a = jax.nn.sigmoid(a0_ref[...] + lora[:, 2 * RW_W:3 * RW_W])
    g_out[...] = _dot(lob, wl_ref[:, 3 * RW_W:]).astype(g_out.dtype)
    for dd in range(2):
        x = -(w0_ref[dd:dd + 1, :] + lora[:, dd * RW_W:(dd + 1) * RW_W])
        softplus = jnp.maximum(x, 0.0) + jnp.log(1.0 + jnp.exp(-jnp.abs(x)))
        lw_out[dd] = -jnp.exp(-softplus - 0.5)
    kk = k * kk_ref[...]
    ss = _group_sum(kk * kk, gsum_ref[...])
    kk = kk / jnp.maximum(jnp.sqrt(ss), 1e-12)
    k = k * (1.0 + (a - 1.0) * ka_ref[...])
    r_out[...] = r
    k_out[...] = k
    v_out[...] = v
    z_out[...] = -kk
    b_out[...] = kk * a
    bonus_out[...] = (_group_sum(r * k * rk_ref[...], gsum_ref[...]) * v).astype(bonus_out.dtype)


def _rwkv_prep_chunk_kernel(u_ref, prev_ref, next_ref, mu_ref, wl_ref, w0_ref, a0_ref, kk_ref, ka_ref, rk_ref,
                            gsum_ref, rp_ref, y0_ref, mx_ref, s0_ref, pt_ref, g_out, bonus_out,
                            r_s, k_s, v_s, z_s, b_s, lw_s, **layout):
    _rwkv_prep_kernel(u_ref, prev_ref, next_ref, mu_ref, wl_ref, w0_ref, a0_ref, kk_ref, ka_ref, rk_ref,
                      gsum_ref, r_s, k_s, v_s, z_s, b_s, lw_s, g_out, bonus_out, **layout)
    _rwkv_chunk_kernel(r_s, k_s, v_s, z_s, b_s, lw_s, rp_ref, y0_ref, mx_ref, s0_ref, pt_ref)


def _rwkv_prep_chunks(ua, mu_pad, w_lora, w0, a0, k_k, k_a, r_k, gsum, batch, lat_blocks, ctx_blocks):
    n = ua.shape[0]
    nb = n // ROW_BLK
    halo = ROW_BLK // SUBLANES
    last8 = n // SUBLANES - 1
    kern = functools.partial(_rwkv_prep_chunk_kernel, lat_total=batch * lat_blocks, lat_blocks=lat_blocks,
                             ctx_blocks=ctx_blocks)
    row = lambda i: (i, 0)
    drow = lambda i: (0, i, 0)
    vec = _const_spec((1, RW_W))
    dspec = pl.BlockSpec((2, ROW_BLK, RW_W), drow)
    out512 = jax.ShapeDtypeStruct((n, RW_W), BF16)
    big = jax.ShapeDtypeStruct((2, n, RW_W), F32)
    tile = pltpu.VMEM((ROW_BLK, RW_W), F32)
    return pl.pallas_call(
        kern,
        grid=(nb,),
        in_specs=[
            pl.BlockSpec((ROW_BLK, A_PAD), row),
            pl.BlockSpec((SUBLANES, A_PAD), lambda i: (jnp.maximum(i * halo - 1, 0), 0)),
            pl.BlockSpec((SUBLANES, A_PAD), lambda i: (jnp.minimum((i + 1) * halo, last8), 0)),
            _const_spec((1, A_PAD)),
            _const_spec((RW_LORA_PAD, 4 * RW_W)),
            _const_spec((2, RW_W)),
            vec, vec, vec, vec,
            _const_spec((MXU_TILE, MXU_TILE)),
        ],
        out_specs=[dspec] * 4 + [pl.BlockSpec((2, CPB, 1, RW_W), lambda i: (0, i, 0, 0))]
        + [pl.BlockSpec((ROW_BLK, RW_W), row)] * 2,
        out_shape=[jax.ShapeDtypeStruct((2, n, RW_W), BF16), big, big, big,
                   jax.ShapeDtypeStruct((2, n // CHUNK, 1, RW_W), F32), out512, out512],
        scratch_shapes=[tile] * 5 + [pltpu.VMEM((2, ROW_BLK, RW_W), F32)],
        compiler_params=_cparams(("parallel",)),
    )(ua, ua, ua, mu_pad, w_lora, w0, a0.reshape(1, RW_W), k_k.reshape(1, RW_W), k_a.reshape(1, RW_W),
      r_k.reshape(1, RW_W), gsum)


CPB = ROW_BLK // CHUNK


def _block_index(bi, d, g, batch, ctx_blocks, seq_blocks):
    lat_blocks = seq_blocks - ctx_blocks
    in_ctx = g < ctx_blocks
    if d == 0:
        pos_ctx, pos_lat = g, g - ctx_blocks
    else:
        pos_ctx, pos_lat = ctx_blocks - 1 - g, seq_blocks - 1 - g
    return jnp.where(in_ctx, batch * lat_blocks + bi * ctx_blocks + pos_ctx, bi * lat_blocks + pos_lat)


def _chunk_order(d):
    return range(CPB) if d == 0 else range(CPB - 1, -1, -1)


def _incl_mask(d):
    row = lax.broadcasted_iota(jnp.int32, (CHUNK, CHUNK), 0)
    col = lax.broadcasted_iota(jnp.int32, (CHUNK, CHUNK), 1)
    return (row - col) * (1 - 2 * d) >= 0


def _rwkv_chunk_kernel(r_ref, k_ref, v_ref, z_ref, b_ref, lw_ref, rp_ref, y0_ref, mx_ref, s0_ref, pt_ref):
    zeros = jnp.zeros((CHUNK, RW_HEAD), BF16)
    row = lax.broadcasted_iota(jnp.int32, (2 * CHUNK, 2 * CHUNK), 0)
    col = lax.broadcasted_iota(jnp.int32, (2 * CHUNK, 2 * CHUNK), 1)
    chains = []
    for cc in range(r_ref.shape[0] // CHUNK):
        rows = slice(cc * CHUNK, (cc + 1) * CHUNK)
        r = r_ref[rows, :]
        k = k_ref[rows, :]
        z = z_ref[rows, :]
        b = b_ref[rows, :]
        vb = v_ref[rows, :].astype(BF16)
        for d in range(2):
            order = ((row & (CHUNK - 1)) - (col & (CHUNK - 1))) * (1 - 2 * d)
            mask = order + (row >> 6) > 0
            lw = lw_ref[d, rows, :]
            cs = _cumsum_rows(_incl_mask(d).astype(BF16), lw)
            tot = jnp.sum(lw, axis=0, keepdims=True)
            p_inc = jnp.exp(cs)
            p_inv = jnp.exp(-cs)
            p_end = jnp.exp(tot - cs)
            zt = z * jnp.exp(cs - lw)
            rt = r * p_inc
            zr = jnp.concatenate([zt, rt], axis=0).astype(BF16)
            bk = jnp.concatenate([b * p_inv, k * p_inv], axis=0).astype(BF16)
            be = (b * p_end).astype(BF16)
            ke = (k * p_end).astype(BF16)
            pt_ref[d, cc] = jnp.exp(tot)
            for h in range(RW_HEADS):
                sl = slice(h * RW_HEAD, (h + 1) * RW_HEAD)
                g = jnp.where(mask, _dot_nt(zr[:, sl], bk[:, sl]), 0.0)
                chains.append(dict(d=d, rows=rows, sl=sl, vh=vb[:, sl], zt=zt[:, sl], rt=rt[:, sl],
                                   be=be[:, sl], ke=ke[:, sl], gtop=g[:CHUNK].astype(BF16),
                                   gbot=g[CHUNK:].astype(BF16), p=g[:CHUNK, :CHUNK]))
    for c in chains:
        akv = _dot(c["gtop"], jnp.concatenate([zeros, c["vh"]], axis=0))
        c["x"] = jnp.concatenate([c["zt"], akv], axis=1)
    row64 = lax.broadcasted_iota(jnp.int32, (CHUNK, CHUNK), 0)
    col64 = lax.broadcasted_iota(jnp.int32, (CHUNK, CHUNK), 1)

    def coupling(level):
        return ((row64 >> (level + 1)) == (col64 >> (level + 1))) & ((row64 >> level) != (col64 >> level))

    eye = (row64 == col64).astype(F32)
    for c in chains:
        c["vt"] = c["vh"].T
        c["t"] = eye + jnp.where(coupling(0), c["p"], 0.0)
    for level in range(1, CHUNK.bit_length() - 1):
        for c in chains:
            off = jnp.where(coupling(level), c["p"], 0.0).astype(BF16)
            c["ot"] = _dot(off, c["t"].astype(BF16))
        for c in chains:
            c["t"] = c["t"] + _dot(c["t"].astype(BF16), c["ot"].astype(BF16))
    for c in chains:
        c["xb"] = _dot(c["t"].astype(BF16), c["x"].astype(BF16)).astype(BF16)
    for c in chains:
        c["xt"] = c["xb"].T
    for c in chains:
        rhs = jnp.concatenate([c["xb"], jnp.concatenate([zeros, c["vh"]], axis=1)], axis=0)
        c["ry"] = _dot(c["gbot"], rhs)
    for c in chains:
        c["ms"] = _dot(c["xt"], c["be"])
        c["vk"] = _dot(c["vt"], c["ke"])
    for c in chains:
        d, rows, sl = c["d"], c["rows"], c["sl"]
        rp_ref[d, rows, sl] = (c["rt"] + c["ry"][:, :RW_HEAD]).astype(rp_ref.dtype)
        y0_ref[d, rows, sl] = c["ry"][:, RW_HEAD:]
        mx_ref[d, rows, sl] = c["ms"][:RW_HEAD]
        s0_ref[d, rows, sl] = c["ms"][RW_HEAD:] + c["vk"]


def _rwkv_state_phases(rp0, y00, mx0, s00, pt0, rp1, y01, mx1, s01, pt1, yf_ref, yb_ref, s_ref):
    dirs = ((rp0, y00, mx0, s00, pt0, yf_ref), (rp1, y01, mx1, s01, pt1, yb_ref))
    heads = [(d, h) for d in range(2) for h in range(RW_HEADS)]
    state = {dh: s_ref[dh[0], dh[1]] for dh in heads}
    for step in range(CPB):
        for d, h in heads:
            rp, y0, mx, s0, pt, y_ref = dirs[d]
            cc = _chunk_order(d)[step]
            rows = slice(cc * CHUNK, (cc + 1) * CHUNK)
            sl = slice(h * RW_HEAD, (h + 1) * RW_HEAD)
            s = state[d, h]
            y_ref[rows, sl] = (_dot_nt(rp[rows, sl], s.astype(BF16)) + y0[rows, sl]).astype(y_ref.dtype)
            state[d, h] = s * pt[cc, :, sl] + _dot_split(s, mx[rows, sl]) + s0[rows, sl]
        if step == CPB - 1:
            for d, h in heads:
                s_ref[d, h] = state[d, h]
        yield


def _hgrn_phases(q0_ref, f0_ref, i0_ref, q1_ref, f1_ref, i1_ref, lb_ref, of_ref, ob_ref, s_ref):
    lb = lb_ref[...]
    dirs = ((q0_ref, f0_ref, i0_ref, of_ref), (q1_ref, f1_ref, i1_ref, ob_ref))
    groups = []
    for d, (q_ref, f_ref, i_ref, o_ref) in enumerate(dirs):
        for cc in _chunk_order(d):
            rows = slice(cc * CHUNK, (cc + 1) * CHUNK)
            fg = lb + (1.0 - lb) * jax.nn.sigmoid(f_ref[rows, :])
            groups.append(dict(d=d, rows=rows, o_ref=o_ref, q=_silu(q_ref[rows, :]), logf=jnp.log(fg),
                               k=1.0 - fg, vb=i_ref[rows, :].astype(BF16)))
    for g in groups:
        g["cs"] = _cumsum_rows(_incl_mask(g["d"]).astype(BF16), g["logf"])
    yield
    for g in groups:
        cs = g["cs"]
        tot = jnp.sum(g["logf"], axis=0, keepdims=True)
        g["qb"] = (g["q"] * jnp.exp(cs)).astype(BF16)
        g["kb"] = (g["k"] * jnp.exp(-cs)).astype(BF16)
        g["kd"] = (g["k"] * jnp.exp(tot - cs)).astype(BF16)
        g["gl"] = jnp.exp(tot)
    heads = [slice(h * HG_DK, (h + 1) * HG_DK) for h in range(HG_HEADS)]
    for g in groups:
        incl = _incl_mask(g["d"])
        g["sc"] = [jnp.where(incl, _dot_nt(g["qb"][:, sl], g["kb"][:, sl]), 0.0).astype(BF16) for sl in heads]
    yield
    for g in groups:
        g["vt"] = g["vb"].T
    for g in groups:
        g["intra"] = [_dot(sc, g["vb"][:, sl]) for sc, sl in zip(g["sc"], heads)]
        g["kv"] = [_dot(g["vt"][sl, :], g["kd"][:, sl]) for sl in heads]
    yield
    state = {(d, h): s_ref[d, h] for d in range(2) for h in range(HG_HEADS)}
    for g in groups:
        d = g["d"]
        for h, sl in enumerate(heads):
            s = state[d, h]
            g["o_ref"][g["rows"], sl] = (g["intra"][h] + _dot_nt(g["qb"][:, sl], s.astype(BF16))).astype(
                g["o_ref"].dtype)
            state[d, h] = s * g["gl"][:, sl] + g["kv"][h]
    for (d, h), s in state.items():
        s_ref[d, h] = s


N_HG_IN = 7
N_RW_IN = 10


def _even_scan_kernel(*refs):
    hg_in = refs[:N_HG_IN]
    rw_in = refs[N_HG_IN:N_HG_IN + N_RW_IN]
    of_ref, ob_ref, yf_ref, yb_ref, hg_state, rw_state = refs[N_HG_IN + N_RW_IN:]

    @pl.when(pl.program_id(1) == 0)
    def _():
        hg_state[...] = jnp.zeros_like(hg_state)
        rw_state[...] = jnp.zeros_like(rw_state)

    streams = [_rwkv_state_phases(*rw_in, yf_ref, yb_ref, rw_state),
               _hgrn_phases(*hg_in, of_ref, ob_ref, hg_state)]
    while streams:
        streams = [s for s in streams if next(s, True) is None]


def _even_scan(ub, lb, rp, y0, mx, s0, pt, batch, ctx_blocks, seq_blocks):
    n = ub.shape[0]
    hg_in, rw_in, hg_out, rw_out = [], [], [], []
    for d in range(2):
        blk = lambda bi, g, d=d: _block_index(bi, d, g, batch, ctx_blocks, seq_blocks)
        hg_in += [
            pl.BlockSpec((ROW_BLK, HG_KW), lambda bi, g, blk=blk: (blk(bi, g), 0)),
            pl.BlockSpec((ROW_BLK, HG_KW), lambda bi, g, blk=blk, d=d: (blk(bi, g), 1 + d)),
            pl.BlockSpec((ROW_BLK, HG_KW), lambda bi, g, blk=blk: (blk(bi, g), 3)),
        ]
        big = pl.BlockSpec((None, ROW_BLK, RW_W), lambda bi, g, d=d, blk=blk: (d, blk(bi, g), 0))
        rw_in += [big] * 4 + [pl.BlockSpec((None, CPB, 1, RW_W), lambda bi, g, d=d, blk=blk: (d, blk(bi, g), 0, 0))]
        hg_out.append(pl.BlockSpec((ROW_BLK, HG_KW), lambda bi, g, blk=blk: (blk(bi, g), 0)))
        rw_out.append(pl.BlockSpec((ROW_BLK, RW_W), lambda bi, g, blk=blk: (blk(bi, g), 0)))
    out = jax.ShapeDtypeStruct((n, HG_KW), BF16)
    return pl.pallas_call(
        _even_scan_kernel,
        grid=(batch, seq_blocks),
        in_specs=hg_in + [_const_spec((1, HG_KW))] + rw_in,
        out_specs=hg_out + rw_out,
        out_shape=[out] * 4,
        scratch_shapes=[pltpu.VMEM((2, HG_HEADS, HG_DK, HG_DK), F32),
                        pltpu.VMEM((2, RW_HEADS, RW_HEAD, RW_HEAD), F32)],
        compiler_params=_cparams(("parallel", "arbitrary")),
    )(ub, ub, ub, ub, ub, ub, lb.reshape(1, HG_KW), rp, y0, mx, s0, pt, rp, y0, mx, s0, pt)


def _even_out_kernel(h_ref, mod_ref, yf_ref, yb_ref, bonus_ref, grw_ref, of_ref, ob_ref, ghg_ref, gng_ref, gnb_ref,
                     hgn_ref, gm64_ref, gm128_ref, w_ref, g_ref, b_ref, out_ref, acc_ref, *, nblk, alpha):
    y = yf_ref[...].astype(F32) + yb_ref[...].astype(F32)
    mu = _group_sum(y, gm64_ref[...])
    yc = y - mu
    var = _group_sum(yc * yc, gm64_ref[...])
    ra = ((yc * lax.rsqrt(var + RW_GN_EPS) * gng_ref[...] + gnb_ref[...] + bonus_ref[...].astype(F32))
          * grw_ref[...].astype(F32))
    o = of_ref[...].astype(F32) + ob_ref[...].astype(F32)
    ms = _group_sum(o * o, gm128_ref[...])
    hb = o * lax.rsqrt(ms + RMS_EPS) * hgn_ref[...] * _silu(ghg_ref[...])
    acc_ref[...] = _dot(ra.astype(BF16), w_ref[:RW_W, :]) + _dot(hb.astype(BF16), w_ref[RW_W:, :])
    _post_norm_rows(h_ref, out_ref, acc_ref, mod_ref, g_ref, b_ref, nblk, alpha, 1.0)


def _even_out(h, mod3, yf, yb, bonus, g_rw, of, ob, ub, gn_g, gn_b, hg_norm, gm64, gm128, w_out, ln_g, ln_b, alpha,
              tm=256):
    n, d = h.shape
    nblk = tm // ROW_BLK
    kern = functools.partial(_even_out_kernel, nblk=nblk, alpha=alpha)
    row = lambda i: (i, 0)
    vec = _const_spec((1, RW_W))
    return pl.pallas_call(
        kern,
        grid=(n // tm,),
        in_specs=[
            pl.BlockSpec((tm, d), row),
            pl.BlockSpec((nblk, 3, d), lambda i: (i, 0, 0)),
            pl.BlockSpec((tm, RW_W), row),
            pl.BlockSpec((tm, RW_W), row),
            pl.BlockSpec((tm, RW_W), row),
            pl.BlockSpec((tm, RW_W), row),
            pl.BlockSpec((tm, HG_KW), row),
            pl.BlockSpec((tm, HG_KW), row),
            pl.BlockSpec((tm, HG_KW), lambda i: (i, 4)),
            vec, vec, vec,
            _const_spec((MXU_TILE, MXU_TILE)),
            _const_spec((MXU_TILE, MXU_TILE)),
            _const_spec((d, d)),
            _const_spec((1, d)),
            _const_spec((1, d)),
        ],
        out_specs=pl.BlockSpec((tm, d), row),
        out_shape=jax.ShapeDtypeStruct((n, d), F32),
        scratch_shapes=[pltpu.VMEM((tm, d), F32)],
        compiler_params=_cparams(("parallel",)),
    )(h, mod3, yf, yb, bonus, g_rw, of, ob, ub, gn_g.reshape(1, RW_W), gn_b.reshape(1, RW_W), hg_norm, gm64, gm128,
      w_out.astype(BF16), ln_g.reshape(1, d), ln_b.reshape(1, d))


def _rope(t, cos, sin_up, sin_dn):
    return t * cos + pltpu.roll(t, LANES - 16, 1) * sin_up + pltpu.roll(t, 16, 1) * sin_dn


def _odd_proj_kernel(h_ref, mod_ref, w_ref, cos_ref, sup_ref, sdn_ref, q_ref, k_ref, v_ref, xm_ref, *, nblk):
    _modulate_rows(h_ref, mod_ref, xm_ref, nblk)
    xm = xm_ref[...]
    cos, sup, sdn = cos_ref[...], sup_ref[...], sdn_ref[...]
    q = _dot(xm, w_ref[:, :DA_W])
    k = _dot(xm, w_ref[:, DA_W:2 * DA_W])
    for h in range(DA_HEADS):
        sl = slice(h * LANES, (h + 1) * LANES)
        q_ref[:, sl] = (_rope(q[:, sl], cos, sup, sdn) * Q_SCALE).astype(q_ref.dtype)
        k_ref[:, sl] = _rope(k[:, sl], cos, sup, sdn).astype(k_ref.dtype)
    v_ref[...] = _dot(xm, w_ref[:, 2 * DA_W:]).astype(v_ref.dtype)


def _odd_proj(h, mod3, w, cos, sup, sdn, tm=512):
    n, d = h.shape
    nblk = tm // ROW_BLK
    kern = functools.partial(_odd_proj_kernel, nblk=nblk)
    row = lambda i: (i, 0)
    out = jax.ShapeDtypeStruct((n, DA_W), BF16)
    return pl.pallas_call(
        kern,
        grid=(n // tm,),
        in_specs=[
            pl.BlockSpec((tm, d), row),
            pl.BlockSpec((nblk, 3, d), lambda i: (i, 0, 0)),
            _const_spec((d, 3 * DA_W)),
            pl.BlockSpec((tm, LANES), row),
            pl.BlockSpec((tm, LANES), row),
            pl.BlockSpec((tm, LANES), row),
        ],
        out_specs=[pl.BlockSpec((tm, DA_W), row)] * 3,
        out_shape=[out, out, out],
        scratch_shapes=[pltpu.VMEM((tm, d), BF16)],
        compiler_params=_cparams(("parallel",)),
    )(h, mod3, w, cos, sup, sdn)


def _diff_attn_kernel(q_ref, kl_ref, kc_ref, vl_ref, vc_ref, lam_ref, ng_ref, o_ref, k_ref, v_ref,
                      *, lam_init, bq):
    t_len = kl_ref.shape[0]

    @pl.when(pl.program_id(2) == 0)
    def _():
        k_ref[:t_len, :] = kl_ref[...]
        k_ref[t_len:, :] = kc_ref[...]
        v_ref[:t_len, :] = vl_ref[...]
        v_ref[t_len:, :] = vc_ref[...]

    lv = lam_ref[...]
    lam = (jnp.exp(jnp.sum(lv[0:1] * lv[1:2], axis=-1, keepdims=True))
           - jnp.exp(jnp.sum(lv[2:3] * lv[3:4], axis=-1, keepdims=True)) + lam_init)
    k = k_ref[...]
    halves = [slice(i * ATT_SUB, (i + 1) * ATT_SUB) for i in range(bq // ATT_SUB)]

    def scores(rows):
        q = q_ref[rows, :]
        return [_dot_nt(q[:, m * DA_HEAD:(m + 1) * DA_HEAD], k[:, m * DA_HEAD:(m + 1) * DA_HEAD])
                for m in range(2)]

    def readout(rows, s):
        es, ls = [], []
        for m in range(2):
            e = jnp.exp2(s[m] - jnp.max(s[m], axis=-1, keepdims=True))
            es.append(e)
            ls.append(jnp.sum(e, axis=-1, keepdims=True))
        a = es[0] - (lam * ls[0] / ls[1]) * es[1]
        o = _dot(a.astype(BF16), v_ref[...]) / ls[0]
        ms = jnp.mean(o * o, axis=-1, keepdims=True)
        o_ref[rows, :] = (o * lax.rsqrt(ms + RMS_EPS) * ng_ref[...] * (1.0 - lam_init)).astype(o_ref.dtype)

    s_next = scores(halves[0])
    for i, rows in enumerate(halves):
        s_cur = s_next
        if i + 1 < len(halves):
            s_next = scores(halves[i + 1])
        readout(rows, s_cur)


def _diff_attn(q, k, v, lam_vecs, norm_g, lam_init, batch, seq_blocks, ctx_blocks, bq=2 * ATT_SUB):
    t_len = (seq_blocks - ctx_blocks) * ROW_BLK
    c_len = ctx_blocks * ROW_BLK
    n_lat = batch * t_len
    assert t_len % bq == 0 and n_lat % c_len == 0
    nt = t_len // bq
    kern = functools.partial(_diff_attn_kernel, lam_init=lam_init, bq=bq)
    lat_spec = pl.BlockSpec((t_len, LANES), lambda b, h, i: (b, h))
    ctx_spec = pl.BlockSpec((c_len, LANES), lambda b, h, i: (n_lat // c_len + b, h))
    tile_spec = pl.BlockSpec((bq, LANES), lambda b, h, i: (b * nt + i, h))
    return pl.pallas_call(
        kern,
        grid=(batch, DA_HEADS, nt),
        in_specs=[tile_spec, lat_spec, ctx_spec, lat_spec, ctx_spec, _const_spec((4, DA_HEAD)),
                  _const_spec((1, LANES))],
        out_specs=tile_spec,
        out_shape=jax.ShapeDtypeStruct((n_lat, DA_W), BF16),
        scratch_shapes=[pltpu.VMEM((t_len + c_len, LANES), BF16), pltpu.VMEM((t_len + c_len, LANES), BF16)],
        compiler_params=_cparams(("parallel", "parallel", "arbitrary")),
    )(q, k, k, v, v, lam_vecs, norm_g.reshape(1, LANES))


def _odd_out_kernel(h_ref, mod_ref, x_ref, w_ref, g_ref, b_ref, out_ref, acc_ref, *, nblk, alpha):
    acc_ref[...] = _dot(x_ref[...], w_ref[...])
    _post_norm_rows(h_ref, out_ref, acc_ref, mod_ref, g_ref, b_ref, nblk, alpha, 1.0)


def _odd_out(h, mod3, x, w_out, ln_g, ln_b, alpha, tm=512):
    n, d = x.shape[0], h.shape[1]
    nblk = tm // ROW_BLK
    kern = functools.partial(_odd_out_kernel, nblk=nblk, alpha=alpha)
    row = lambda i: (i, 0)
    return pl.pallas_call(
        kern,
        grid=(n // tm,),
        in_specs=[
            pl.BlockSpec((tm, d), row),
            pl.BlockSpec((nblk, 3, d), lambda i: (i, 0, 0)),
            pl.BlockSpec((tm, DA_W), row),
            _const_spec((DA_W, d)),
            _const_spec((1, d)),
            _const_spec((1, d)),
        ],
        out_specs=pl.BlockSpec((tm, d), row),
        out_shape=jax.ShapeDtypeStruct((n, d), F32),
        scratch_shapes=[pltpu.VMEM((tm, d), F32)],
        compiler_params=_cparams(("parallel",)),
    )(h, mod3, x, w_out.astype(BF16), ln_g.reshape(1, d), ln_b.reshape(1, d))


def _rope_tables(batch, ctx_len, t):
    n_rows = t // GRID_W
    rowp = np.repeat(np.arange(n_rows), GRID_W).astype(np.float32)
    colp = np.tile(np.arange(GRID_W), n_rows).astype(np.float32)
    n_freq = DA_HEAD // 4
    inv = jnp.asarray(ROPE_BASE, F32) ** (-jnp.arange(n_freq, dtype=F32) / n_freq)
    ar = jnp.asarray(rowp)[:, None] * inv
    ac = jnp.asarray(colp)[:, None] * inv
    ang = jnp.concatenate([ar, ar, ac, ac], axis=-1)
    n_ctx = batch * ctx_len
    cos = jnp.concatenate([jnp.tile(jnp.cos(ang), (batch, 1)), jnp.ones((n_ctx, DA_HEAD), F32)], axis=0)
    sin = jnp.concatenate([jnp.tile(jnp.sin(ang), (batch, 1)), jnp.zeros((n_ctx, DA_HEAD), F32)], axis=0)
    first = (np.arange(DA_HEAD) % 32) < 16
    sup = jnp.where(first, -sin, 0.0)
    sdn = jnp.where(first, 0.0, sin)
    rep = lambda a: jnp.tile(a, (1, 2))
    return rep(cos), rep(sup), rep(sdn)


def _group_mean_matrix(group, scale=1.0):
    assert MXU_TILE % group == 0
    idx = np.arange(MXU_TILE) // group
    return jnp.asarray((idx[:, None] == idx[None, :]).astype(np.float32) * (scale / group)).astype(BF16)


def kernel(x, c, ctx, c_ctx, w_mod, b_mod, ln_g, ln_b, ffn_w_in, ffn_w_out, ev_w_in, ev_w_out,
           rw_mu, rw_w0, rw_w2, rw_a0, rw_a2, rw_g2, rw_k_k, rw_k_a, rw_r_k, rw_gn_g, rw_gn_b,
           hg_lb, hg_norm_g, od_w_in, od_w_out, da_lambda, da_norm_g):
    batch, t, d = x.shape
    ctx_len = ctx.shape[1]
    depth = w_mod.shape[0]
    assert t % ROW_BLK == 0 and ctx_len % ROW_BLK == 0 and t % GRID_W == 0
    alpha = (2 * depth) ** 0.25
    s_len = ctx_len + t
    seq_blocks = s_len // ROW_BLK
    ctx_blocks = ctx_len // ROW_BLK
    lat_blocks = seq_blocks - ctx_blocks
    n_lat = batch * t

    h = (x.reshape(n_lat, d), ctx.reshape(batch * ctx_len, d))

    rows_pad = -(-(batch + 1) // SUBLANES) * SUBLANES
    cvec = jnp.zeros((rows_pad, d), F32).at[:batch].set(c).at[batch].set(c_ctx)
    m_all = _modulation(cvec, w_mod, b_mod).reshape(depth, rows_pad, N_SUB, 3, d)
    blk_row = np.array([bi for bi in range(batch) for _ in range(lat_blocks)] + [batch] * (batch * ctx_blocks))
    m_blk = m_all[:, blk_row]

    lower_bounds = jnp.cumsum(jax.nn.softmax(hg_lb.astype(F32), axis=0), axis=0)
    gsum64 = _group_mean_matrix(RW_HEAD, scale=RW_HEAD)
    gm64 = _group_mean_matrix(RW_HEAD)
    gm128 = _group_mean_matrix(HG_DK)

    for i in range(depth):
        j = i // 2
        with_ctx_out = i < depth - 1
        ffn = functools.partial(_ffn_sublayer, alpha=alpha)
        h = ffn(h, m_blk[i, :, 0], ffn_w_in[i, 0], ffn_w_out[i, 0], ln_g[i, 0], ln_b[i, 0])
        if i % 2 == 0:
            pad_a = jnp.zeros((d, A_PAD - A_TOTAL), F32)
            w_pad = jnp.concatenate([ev_w_in[j][:, :A_TOTAL], pad_a, ev_w_in[j][:, A_TOTAL:]], axis=1).astype(BF16)
            ua, ub = _even_proj(h, m_blk[i, :, 1], w_pad)
            mu_pad = jnp.concatenate([rw_mu[j], jnp.zeros((A_PAD - A_TOTAL,), F32)]).reshape(1, A_PAD)
            w_lora = jnp.zeros((RW_LORA_PAD, 4 * RW_W), F32)
            w_lora = w_lora.at[0:64, 0:RW_W].set(rw_w2[j, 0]).at[64:128, RW_W:2 * RW_W].set(rw_w2[j, 1])
            w_lora = w_lora.at[128:192, 2 * RW_W:3 * RW_W].set(rw_a2[j])
            w_lora = w_lora.at[192:192 + RW_GATE_LORA, 3 * RW_W:].set(rw_g2[j]).astype(BF16)
            rp, y0, mx, s0, pt, g_rw, bonus = _rwkv_prep_chunks(
                ua, mu_pad, w_lora, rw_w0[j], rw_a0[j], rw_k_k[j], rw_k_a[j], rw_r_k[j].reshape(RW_W), gsum64,
                batch, lat_blocks, ctx_blocks)
            of, ob, yf, yb = _even_scan(ub, lower_bounds[i], rp, y0, mx, s0, pt, batch, ctx_blocks, seq_blocks)
            hg_norm = jnp.tile(hg_norm_g[j], HG_HEADS).reshape(1, HG_KW)
            h = _even_out(h, m_blk[i, :, 1], yf, yb, bonus, g_rw, of, ob, ub, rw_gn_g[j], rw_gn_b[j], hg_norm, gm64, gm128,
                          ev_w_out[j], ln_g[i, 1], ln_b[i, 1], alpha)
        else:
            lam_init = 0.8 - 0.6 * math.exp(-0.3 * i)
            cos, sup, sdn = _rope_tables(batch, ctx_len, t)
            q, k, v = _odd_proj(h, m_blk[i, :, 1], od_w_in[j].astype(BF16), cos, sup, sdn)
            att = _diff_attn(q, k, v, da_lambda[j], da_norm_g[j], lam_init, batch, seq_blocks, ctx_blocks)
            assert not with_ctx_out
            h = _odd_out(h, m_blk[i, :, 1], att, od_w_out[j], ln_g[i, 1], ln_b[i, 1], alpha)
        h = ffn(h, m_blk[i, :, 2], ffn_w_in[i, 1], ffn_w_out[i, 1], ln_g[i, 2], ln_b[i, 2],
                n_rows=None if with_ctx_out else n_lat)
    return h[:n_lat].reshape(batch, t, d)
```
